```python
import functools
import jax
import jax.numpy as jnp
from jax import lax
import numpy as np

D_MODEL = 1024
BATCH = 32
SEQ = 256
DEPTH = 4
DEC_BATCH = 4
DEC_SEQ = 2048
PAST_LEN = 256

GRID_W = 64
N_MIXERS = 3
NORM_EPS = 1e-6
ROPE_THETA = 10000.0
Q_BLOCK = 128

MLA_HEADS = D_MODEL // 64
MLA_NOPE = 64
MLA_ROPE = 32
MLA_V = 64
MLA_Q_RANK = 3 * D_MODEL // 8
MLA_KV_RANK = D_MODEL // 4
MLA_SCALE = (MLA_NOPE + MLA_ROPE) ** -0.5

SWA_HEADS = D_MODEL // 64
SWA_KV_HEADS = SWA_HEADS // 4
SWA_GROUP = SWA_HEADS // SWA_KV_HEADS
SWA_HEAD_DIM = 64
SWA_WINDOW = 128
SWA_SCALE = SWA_HEAD_DIM ** -0.5

CONV_WIDTH = 31
CONV_PAD = CONV_WIDTH // 2

N_EXPERTS = 32
TOP_K = 4
D_EXPERT = D_MODEL
SWIGLU_ALPHA = 1.702
SWIGLU_LIMIT = 7.0
MOE_BLOCK = 128

kernel_name = "hybrid_mla_swa_conformer_moe_diffusion_step"


def rms_norm(x, g):
    xf = x.astype(jnp.float32)
    y = xf * lax.rsqrt(jnp.mean(xf * xf, axis=-1, keepdims=True) + NORM_EPS)
    return (y * g.astype(jnp.float32)).astype(x.dtype)


def layer_norm(x, g, b):
    xf = x.astype(jnp.float32)
    mu = jnp.mean(xf, axis=-1, keepdims=True)
    var = jnp.mean(jnp.square(xf - mu), axis=-1, keepdims=True)
    y = (xf - mu) * lax.rsqrt(var + NORM_EPS)
    return (y * g.astype(jnp.float32) + b.astype(jnp.float32)).astype(x.dtype)


def adaln_chunks(cond, w, b):
    m = jax.nn.silu(cond) @ w + b
    return jnp.split(m, 6, axis=-1)


def modulate(h, shift, scale):
    return h * (1.0 + scale) + shift


def axial_rope(n_tok, rot_dim):
    rows = n_tok // GRID_W
    row = jnp.repeat(jnp.arange(rows, dtype=jnp.float32), GRID_W)
    col = jnp.tile(jnp.arange(GRID_W, dtype=jnp.float32), rows)
    n_freq = rot_dim // 4
    inv_freq = ROPE_THETA ** (-jnp.arange(n_freq, dtype=jnp.float32) / n_freq)
    ang = jnp.concatenate([row[:, None] * inv_freq, col[:, None] * inv_freq], axis=-1)
    return jnp.cos(ang), jnp.sin(ang)


def apply_rope(x, cos, sin):
    half = x.shape[-1] // 2
    bshape = (cos.shape[0],) + (1,) * (x.ndim - 3) + (half,)
    cos = cos.reshape(bshape)
    sin = sin.reshape(bshape)
    xf = x.astype(jnp.float32)
    x1, x2 = xf[..., :half], xf[..., half:]
    return jnp.concatenate([x1 * cos - x2 * sin, x2 * cos + x1 * sin], axis=-1).astype(x.dtype)


def attention_probs(s, sink):
    if sink is None:
        return jax.nn.softmax(s, axis=-1)
    sink_col = jnp.broadcast_to(sink.astype(jnp.float32)[None, :, :, None, None], s.shape[:-1] + (1,))
    return jax.nn.softmax(jnp.concatenate([s, sink_col], axis=-1), axis=-1)[..., :-1]


def dense_attention(q, k, v, scale, sink=None):
    B, Sq, N, G, dk = q.shape
    nb = Sq // Q_BLOCK
    qb = q.reshape(B, nb, Q_BLOCK, N, G, dk).transpose(1, 0, 2, 3, 4, 5)

    def one_block(qblk):
        s = jnp.einsum("bqngd,bknd->bngqk", qblk, k).astype(jnp.float32) * scale
        p = attention_probs(s, sink).astype(v.dtype)
        return jnp.einsum("bngqk,bknd->bqngd", p, v)

    out = lax.map(one_block, qb)
    return out.transpose(1, 0, 2, 3, 4, 5).reshape(B, Sq, N, G, v.shape[-1])


def window_context_attention(q, k, v, k_ctx, v_ctx, sink, scale):
    B, S, N, G, dh = q.shape
    P = k_ctx.shape[1]
    span = Q_BLOCK + 2 * SWA_WINDOW
    pad = ((0, 0), (SWA_WINDOW, SWA_WINDOW), (0, 0), (0, 0))
    k_pad = jnp.pad(k, pad)
    v_pad = jnp.pad(v, pad)
    q_off = jnp.arange(Q_BLOCK)
    k_off = jnp.arange(span)

    def one_block(i):
        start = i * Q_BLOCK
        qb = lax.dynamic_slice_in_dim(q, start, Q_BLOCK, axis=1)
        kb = lax.dynamic_slice_in_dim(k_pad, start, span, axis=1)
        vb = lax.dynamic_slice_in_dim(v_pad, start, span, axis=1)
        kpos = start - SWA_WINDOW + k_off
        qpos = start + q_off
        valid = ((jnp.abs(kpos[None, :] - qpos[:, None]) <= SWA_WINDOW)
                 & (kpos >= 0)[None, :] & (kpos < S)[None, :])
        s_ctx = jnp.einsum("bqngd,bknd->bngqk", qb, k_ctx).astype(jnp.float32) * scale
        s_loc = jnp.einsum("bqngd,bknd->bngqk", qb, kb).astype(jnp.float32) * scale
        s_loc = jnp.where(valid, s_loc, -jnp.inf)
        p = attention_probs(jnp.concatenate([s_ctx, s_loc], axis=-1), sink).astype(v.dtype)
        return (jnp.einsum("bngqk,bknd->bqngd", p[..., :P], v_ctx)
                + jnp.einsum("bngqk,bknd->bqngd", p[..., P:], vb))

    out = lax.map(one_block, jnp.arange(S // Q_BLOCK))
    return out.transpose(1, 0, 2, 3, 4, 5).reshape(B, S, N, G, v.shape[-1])


def mla_project(h, mp):
    B, n, _ = h.shape
    a = h @ mp["wa"]
    cq = a[..., :MLA_Q_RANK]
    ckv = a[..., MLA_Q_RANK:MLA_Q_RANK + MLA_KV_RANK]
    krope = a[..., MLA_Q_RANK + MLA_KV_RANK:]
    q = (rms_norm(cq, mp["q_norm"]) @ mp["wuq"]).reshape(B, n, MLA_HEADS, MLA_NOPE + MLA_ROPE)
    return q, rms_norm(ckv, mp["kv_norm"]), krope


def mla_expand(ckv, mp):
    B, n, _ = ckv.shape
    kv = (ckv @ mp["wukv"]).reshape(B, n, MLA_HEADS, MLA_NOPE + MLA_V)
    return kv[..., :MLA_NOPE], kv[..., MLA_NOPE:]


def mla_attention(q, k_nope, k_rope, v):
    B, Sk, H, _ = k_nope.shape
    k = jnp.concatenate([k_nope, jnp.broadcast_to(k_rope[:, :, None, :], (B, Sk, H, MLA_ROPE))], axis=-1)
    o = dense_attention(q[:, :, :, None, :], k, v, MLA_SCALE)
    return o.reshape(o.shape[0], o.shape[1], H * MLA_V)


def mla_context(h, mp):
    q, ckv, krope = mla_project(h, mp)
    k_nope, v = mla_expand(ckv, mp)
    return mla_attention(q, k_nope, krope, v) @ mp["wo"], (ckv, krope)


def mla_latent(h, mp, cache):
    cache_ckv, cache_krope = cache
    q, ckv, krope = mla_project(h, mp)
    cos, sin = axial_rope(h.shape[1], MLA_ROPE)
    q = jnp.concatenate([q[..., :MLA_NOPE], apply_rope(q[..., MLA_NOPE:], cos, sin)], axis=-1)
    krope = apply_rope(krope, cos, sin)
    ckv_all = jnp.concatenate([cache_ckv, ckv], axis=1)
    krope_all = jnp.concatenate([cache_krope, krope], axis=1)
    k_nope, v = mla_expand(ckv_all, mp)
    return mla_attention(q, k_nope, krope_all, v) @ mp["wo"], ()


def swa_project(h, sp):
    B, n, _ = h.shape
    qkv = h @ sp["wqkv"] + sp["bqkv"]
    nq = SWA_HEADS * SWA_HEAD_DIM
    nkv = SWA_KV_HEADS * SWA_HEAD_DIM
    q = qkv[..., :nq].reshape(B, n, SWA_KV_HEADS, SWA_GROUP, SWA_HEAD_DIM)
    k = qkv[..., nq:nq + nkv].reshape(B, n, SWA_KV_HEADS, SWA_HEAD_DIM)
    v = qkv[..., nq + nkv:].reshape(B, n, SWA_KV_HEADS, SWA_HEAD_DIM)
    return q, k, v


def swa_out(o, sp):
    B, n = o.shape[:2]
    return o.reshape(B, n, SWA_HEADS * SWA_HEAD_DIM) @ sp["wo"] + sp["bo"]


def swa_context(h, sp):
    q, k, v = swa_project(h, sp)
    sink = sp["sink"].reshape(SWA_KV_HEADS, SWA_GROUP)
    return swa_out(dense_attention(q, k, v, SWA_SCALE, sink), sp), (k, v)


def swa_latent(h, sp, cache):
    cache_k, cache_v = cache
    q, k, v = swa_project(h, sp)
    cos, sin = axial_rope(h.shape[1], SWA_HEAD_DIM)
    q = apply_rope(q, cos, sin)
    k = apply_rope(k, cos, sin)
    sink = sp["sink"].reshape(SWA_KV_HEADS, SWA_GROUP)
    o = window_context_attention(q, k, v, cache_k, cache_v, sink, SWA_SCALE)
    return swa_out(o, sp), ()


def conv_module(h, conv_p):
    u = h @ conv_p["w1"] + conv_p["b1"]
    a, g = jnp.split(u, 2, axis=-1)
    glu = a * jax.nn.sigmoid(g)
    y = lax.conv_general_dilated(glu, conv_p["dw"][:, None, :], window_strides=(1,),
                                 padding=[(CONV_PAD, CONV_PAD)],
                                 dimension_numbers=("NWC", "WIO", "NWC"),
                                 feature_group_count=glu.shape[-1]) + conv_p["dwb"]
    y = jax.nn.silu(layer_norm(y, conv_p["ln_g"], conv_p["ln_b"]))
    return y @ conv_p["w2"] + conv_p["b2"]


def conv_mixer(h, conv_p):
    return conv_module(h, conv_p), ()


def conv_latent(h, conv_p, cache):
    return conv_module(h, conv_p), ()


def moe_ffn(h, cp):
    B_, n, D = h.shape
    x = h.reshape(B_ * n, D)
    T = x.shape[0]
    logits = (x @ cp["router_w"] + cp["router_b"]).astype(jnp.float32)
    top_val, top_idx = lax.top_k(logits, TOP_K)
    gate = jax.nn.softmax(top_val, axis=-1)
    TK = T * TOP_K
    flat_e = top_idx.reshape(TK)
    flat_tok = jnp.arange(TK, dtype=jnp.int32) // TOP_K
    order = jnp.argsort(flat_e)
    e_sorted = flat_e[order]
    counts = jnp.bincount(flat_e, length=N_EXPERTS)
    padded = (counts + MOE_BLOCK - 1) // MOE_BLOCK * MOE_BLOCK
    pad_end = jnp.cumsum(padded)
    pad_start = pad_end - padded
    start = jnp.cumsum(counts) - counts
    dest = pad_start[e_sorted] + jnp.arange(TK) - start[e_sorted]
    n_blocks = (TK + N_EXPERTS * (MOE_BLOCK - 1) + MOE_BLOCK - 1) // MOE_BLOCK
    cap = n_blocks * MOE_BLOCK
    slot_tok = jnp.zeros((cap,), jnp.int32).at[dest].set(flat_tok[order])
    slot_w = jnp.zeros((cap,), jnp.float32).at[dest].set(gate.reshape(TK)[order])
    block_e = jnp.minimum(jnp.searchsorted(pad_end, jnp.arange(n_blocks) * MOE_BLOCK, side="right"),
                          N_EXPERTS - 1)
    xs = x[slot_tok].reshape(n_blocks, MOE_BLOCK, D)

    def expert_block(args):
        xb, e = args
        g = xb @ cp["wg"][e] + cp["bg"][e]
        u = xb @ cp["wu"][e] + cp["bu"][e]
        g = jnp.minimum(g, SWIGLU_LIMIT)
        u = jnp.clip(u, -SWIGLU_LIMIT, SWIGLU_LIMIT)
        act = (u + 1.0) * (g * jax.nn.sigmoid(SWIGLU_ALPHA * g))
        return act @ cp["wd"][e] + cp["bd"][e]

    ys = lax.map(expert_block, (xs, block_e)).reshape(cap, D)
    out = jnp.zeros_like(x).at[slot_tok].add(ys * slot_w[:, None].astype(ys.dtype))
    return out.reshape(B_, n, D)


def trunk_layer(x, cond, cp, mixer_fn):
    sh1, sc1, g1, sh2, sc2, g2 = adaln_chunks(cond, cp["mod_w"], cp["mod_b"])
    mix, st = mixer_fn(modulate(rms_norm(x, cp["norm1"]), sh1, sc1))
    x = x + g1 * mix
    x = x + g2 * moe_ffn(modulate(rms_norm(x, cp["norm2"]), sh2, sc2), cp)
    return x, st


def setup_inputs(seed: int = 0) -> dict:
    ks = list(jax.random.split(jax.random.key(seed), 160))

    def nrm(shape, scale):
        return jax.random.normal(ks.pop(), shape, jnp.float32) * scale

    def gain(n):
        return 1.0 + nrm((n,), 0.02)

    D = D_MODEL
    inp = {}
    inp["x_prompt"] = nrm((BATCH, SEQ, D), 1.0)
    inp["x_sample"] = nrm((DEC_BATCH, DEC_SEQ, D), 1.0)
    inp["cache_l0_ckv"] = nrm((DEC_BATCH, PAST_LEN, MLA_KV_RANK), 1.0)
    inp["cache_l0_krope"] = nrm((DEC_BATCH, PAST_LEN, MLA_ROPE), 1.0)
    inp["cache_l1_k"] = nrm((DEC_BATCH, PAST_LEN, SWA_KV_HEADS, SWA_HEAD_DIM), 1.0)
    inp["cache_l1_v"] = nrm((DEC_BATCH, PAST_LEN, SWA_KV_HEADS, SWA_HEAD_DIM), 1.0)
    inp["cache_l3_ckv"] = nrm((DEC_BATCH, PAST_LEN, MLA_KV_RANK), 1.0)
    inp["cache_l3_krope"] = nrm((DEC_BATCH, PAST_LEN, MLA_ROPE), 1.0)
    inp["c"] = nrm((DEC_BATCH, D), 1.0)
    inp["c_ctx"] = nrm((D,), 1.0)
    for l in range(DEPTH):
        pre = f"l{l}_"
        kind = l % N_MIXERS
        if kind == 0:
            inp[pre + "mla_wa"] = nrm((D, MLA_Q_RANK + MLA_KV_RANK + MLA_ROPE), D ** -0.5)
            inp[pre + "mla_q_norm"] = gain(MLA_Q_RANK)
            inp[pre + "mla_wuq"] = nrm((MLA_Q_RANK, MLA_HEADS * (MLA_NOPE + MLA_ROPE)), MLA_Q_RANK ** -0.5)
            inp[pre + "mla_kv_norm"] = gain(MLA_KV_RANK)
            inp[pre + "mla_wukv"] = nrm((MLA_KV_RANK, MLA_HEADS * (MLA_NOPE + MLA_V)), MLA_KV_RANK ** -0.5)
            inp[pre + "mla_wo"] = nrm((MLA_HEADS * MLA_V, D), (MLA_HEADS * MLA_V) ** -0.5)
        elif kind == 1:
            n_qkv = (SWA_HEADS + 2 * SWA_KV_HEADS) * SWA_HEAD_DIM
            inp[pre + "swa_wqkv"] = nrm((D, n_qkv), D ** -0.5)
            inp[pre + "swa_bqkv"] = nrm((n_qkv,), 0.01)
            inp[pre + "swa_sink"] = nrm((SWA_HEADS,), 0.5)
            inp[pre + "swa_wo"] = nrm((SWA_HEADS * SWA_HEAD_DIM, D), (SWA_HEADS * SWA_HEAD_DIM) ** -0.5)
            inp[pre + "swa_bo"] = nrm((D,), 0.01)
        else:
            inp[pre + "conv_w1"] = nrm((D, 2 * D), D ** -0.5)
            inp[pre + "conv_b1"] = nrm((2 * D,), 0.01)
            inp[pre + "conv_dw"] = nrm((CONV_WIDTH, D), CONV_WIDTH ** -0.5)
            inp[pre + "conv_dwb"] = nrm((D,), 0.01)
            inp[pre + "conv_ln_g"] = gain(D)
            inp[pre + "conv_ln_b"] = nrm((D,), 0.01)
            inp[pre + "conv_w2"] = nrm((D, D), D ** -0.5)
            inp[pre + "conv_b2"] = nrm((D,), 0.01)
        inp[pre + "mod_w"] = nrm((D, 6 * D), 0.5 * D ** -0.5)
        inp[pre + "mod_b"] = nrm((6 * D,), 0.01)
        inp[pre + "norm1"] = gain(D)
        inp[pre + "norm2"] = gain(D)
        inp[pre + "router_w"] = nrm((D, N_EXPERTS), D ** -0.5)
        inp[pre + "router_b"] = nrm((N_EXPERTS,), 0.01)
        inp[pre + "moe_wg"] = nrm((N_EXPERTS, D, D_EXPERT), D ** -0.5)
        inp[pre + "moe_bg"] = nrm((N_EXPERTS, D_EXPERT), 0.01)
        inp[pre + "moe_wu"] = nrm((N_EXPERTS, D, D_EXPERT), D ** -0.5)
        inp[pre + "moe_bu"] = nrm((N_EXPERTS, D_EXPERT), 0.01)
        inp[pre + "moe_wd"] = nrm((N_EXPERTS, D_EXPERT, D), D_EXPERT ** -0.5)
        inp[pre + "moe_bd"] = nrm((N_EXPERTS, D), 0.01)
    inp["final_norm"] = gain(D)
    return inp


def _common(mod_w, mod_b, norm1, norm2, router_w, router_b, wg, bg, wu, bu, wd, bd):
    return dict(mod_w=mod_w, mod_b=mod_b, norm1=norm1, norm2=norm2, router_w=router_w,
                router_b=router_b, wg=wg, bg=bg, wu=wu, bu=bu, wd=wd, bd=bd)


def reference(x_prompt, x_sample,
              cache_l0_ckv, cache_l0_krope, cache_l1_k, cache_l1_v, cache_l3_ckv, cache_l3_krope,
              c, c_ctx,
              l0_mla_wa, l0_mla_q_norm, l0_mla_wuq, l0_mla_kv_norm, l0_mla_wukv, l0_mla_wo,
              l0_mod_w, l0_mod_b, l0_norm1, l0_norm2, l0_router_w, l0_router_b,
              l0_moe_wg, l0_moe_bg, l0_moe_wu, l0_moe_bu, l0_moe_wd, l0_moe_bd,
              l1_swa_wqkv, l1_swa_bqkv, l1_swa_sink, l1_swa_wo, l1_swa_bo,
              l1_mod_w, l1_mod_b, l1_norm1, l1_norm2, l1_router_w, l1_router_b,
              l1_moe_wg, l1_moe_bg, l1_moe_wu, l1_moe_bu, l1_moe_wd, l1_moe_bd,
              l2_conv_w1, l2_conv_b1, l2_conv_dw, l2_conv_dwb, l2_conv_ln_g, l2_conv_ln_b, l2_conv_w2, l2_conv_b2,
              l2_mod_w, l2_mod_b, l2_norm1, l2_norm2, l2_router_w, l2_router_b,
              l2_moe_wg, l2_moe_bg, l2_moe_wu, l2_moe_bu, l2_moe_wd, l2_moe_bd,
              l3_mla_wa, l3_mla_q_norm, l3_mla_wuq, l3_mla_kv_norm, l3_mla_wukv, l3_mla_wo,
              l3_mod_w, l3_mod_b, l3_norm1, l3_norm2, l3_router_w, l3_router_b,
              l3_moe_wg, l3_moe_bg, l3_moe_wu, l3_moe_bu, l3_moe_wd, l3_moe_bd,
              final_norm):
    mixers = [
        dict(wa=l0_mla_wa, q_norm=l0_mla_q_norm, wuq=l0_mla_wuq, kv_norm=l0_mla_kv_norm, wukv=l0_mla_wukv, wo=l0_mla_wo),
        dict(wqkv=l1_swa_wqkv, bqkv=l1_swa_bqkv, sink=l1_swa_sink, wo=l1_swa_wo, bo=l1_swa_bo),
        dict(w1=l2_conv_w1, b1=l2_conv_b1, dw=l2_conv_dw, dwb=l2_conv_dwb, ln_g=l2_conv_ln_g, ln_b=l2_conv_ln_b,
             w2=l2_conv_w2, b2=l2_conv_b2),
        dict(wa=l3_mla_wa, q_norm=l3_mla_q_norm, wuq=l3_mla_wuq, kv_norm=l3_mla_kv_norm, wukv=l3_mla_wukv, wo=l3_mla_wo),
    ]
    commons = [
        _common(l0_mod_w, l0_mod_b, l0_norm1, l0_norm2, l0_router_w, l0_router_b,
                l0_moe_wg, l0_moe_bg, l0_moe_wu, l0_moe_bu, l0_moe_wd, l0_moe_bd),
        _common(l1_mod_w, l1_mod_b, l1_norm1, l1_norm2, l1_router_w, l1_router_b,
                l1_moe_wg, l1_moe_bg, l1_moe_wu, l1_moe_bu, l1_moe_wd, l1_moe_bd),
        _common(l2_mod_w, l2_mod_b, l2_norm1, l2_norm2, l2_router_w, l2_router_b,
                l2_moe_wg, l2_moe_bg, l2_moe_wu, l2_moe_bu, l2_moe_wd, l2_moe_bd),
        _common(l3_mod_w, l3_mod_b, l3_norm1, l3_norm2, l3_router_w, l3_router_b,
                l3_moe_wg, l3_moe_bg, l3_moe_wu, l3_moe_bu, l3_moe_wd, l3_moe_bd),
    ]
    caches = [(cache_l0_ckv, cache_l0_krope), (cache_l1_k, cache_l1_v), (), (cache_l3_ckv, cache_l3_krope)]

    cond_ctx = c_ctx[None, None, :]
    xp = x_prompt
    states = []
    for l in range(DEPTH):
        kind = l % N_MIXERS
        if kind == 0:
            fn = functools.partial(mla_context, mp=mixers[l])
        elif kind == 1:
            fn = functools.partial(swa_context, sp=mixers[l])
        else:
            fn = functools.partial(conv_mixer, conv_p=mixers[l])
        xp, st = trunk_layer(xp, cond_ctx, commons[l], fn)
        states.append(st)
    y_prompt = rms_norm(xp, final_norm)

    cond_lat = c[:, None, :]
    xs = x_sample
    for l in range(DEPTH):
        kind = l % N_MIXERS
        if kind == 0:
            fn = functools.partial(mla_latent, mp=mixers[l], cache=caches[l])
        elif kind == 1:
            fn = functools.partial(swa_latent, sp=mixers[l], cache=caches[l])
        else:
            fn = functools.partial(conv_latent, conv_p=mixers[l], cache=caches[l])
        xs, _ = trunk_layer(xs, cond_lat, commons[l], fn)
    y_sample = rms_norm(xs, final_norm)

    state_l0_ckv, state_l0_krope = states[0]
    state_l1_k, state_l1_v = states[1]
    state_l3_ckv, state_l3_krope = states[3]
    return (y_prompt, y_sample, state_l0_ckv, state_l0_krope, state_l1_k, state_l1_v, state_l3_ckv, state_l3_krope)
```

```python
import functools

import jax
import jax.numpy as jnp
from jax import lax
from jax.experimental import pallas as pl
from jax.experimental.pallas import tpu as pltpu

F32 = jnp.float32
BF16 = jnp.bfloat16

D_MODEL = 1024
BATCH = 32
SEQ = 256
DEPTH = 4
DEC_BATCH = 4
DEC_SEQ = 2048
PAST_LEN = 256
GRID_W = 64
NORM_EPS = 1e-6
ROPE_THETA = 10000.0

MLA_HEADS = 16
MLA_NOPE = 64
MLA_ROPE = 32
MLA_V = 64
MLA_Q_RANK = 384
MLA_KV_RANK = 256
MLA_SCALE = (MLA_NOPE + MLA_ROPE) ** -0.5

SWA_HEADS = 16
SWA_KV_HEADS = 4
SWA_GROUP = 4
SWA_HEAD_DIM = 64
SWA_WINDOW = 128
SWA_SCALE = SWA_HEAD_DIM ** -0.5

CONV_WIDTH = 31
CONV_PAD = CONV_WIDTH // 2

N_EXPERTS = 32
TOP_K = 4
SWIGLU_ALPHA = 1.702
SWIGLU_LIMIT = 7.0

LANES = 128
T_CTX = BATCH * SEQ
T_LAT = DEC_BATCH * DEC_SEQ
T_ALL = T_CTX + T_LAT
CTX_COND = DEC_BATCH
N_COND = 8

TM = 512
TQ = 256
CONV_TM = 256
CONV_HALO = 16
MOE_BM = 256
MOE_BLOCKS = (T_ALL * TOP_K + N_EXPERTS * (MOE_BM - 1) + MOE_BM - 1) // MOE_BM
MOE_CAP = MOE_BLOCKS * MOE_BM
VMEM_LIMIT = 56 * 1024 * 1024


def _params(*sem):
    return pltpu.CompilerParams(dimension_semantics=sem, vmem_limit_bytes=VMEM_LIMIT)


def _cond_of_row(row0):
    return jnp.where(row0 < T_CTX, CTX_COND, (row0 - T_CTX) // DEC_SEQ)


def _pos_block(i, tm):
    row0 = i * tm
    return jnp.where(row0 < T_CTX, DEC_SEQ // tm, ((row0 - T_CTX) % DEC_SEQ) // tm)


def _rms(x, g):
    ms = jnp.mean(x * x, axis=-1, keepdims=True)
    return x * lax.rsqrt(ms + NORM_EPS) * g


def _modulated(x, g, mod, j):
    return _rms(x, g) * (1.0 + mod[j + 1:j + 2]) + mod[j:j + 1]


def _sigmoid(x):
    return 1.0 / (1.0 + jnp.exp(-x))


def _dot(a, b):
    return jnp.dot(a, b, preferred_element_type=F32)


def _dot_nt(a, b):
    return lax.dot_general(a, b, (((1,), (1,)), ((), ())), preferred_element_type=F32)


def _rope_slab(a, c, s, shift):
    return a * c + (pltpu.roll(a, shift, 1) - pltpu.roll(a, LANES - shift, 1)) * s


def _mod_kernel(c_ref, w_ref, b_ref, o_ref):
    c = c_ref[...]
    s = (c * _sigmoid(c)).astype(BF16)
    o_ref[...] = _dot(s, w_ref[...].astype(BF16)) + b_ref[...]


def _modulation(cond, w, b):
    tn = 1536
    out = pl.pallas_call(
        _mod_kernel,
        grid=(6 * D_MODEL // tn,),
        in_specs=[pl.BlockSpec((N_COND, D_MODEL), lambda j: (0, 0)),
                  pl.BlockSpec((D_MODEL, tn), lambda j: (0, j)),
                  pl.BlockSpec((1, tn), lambda j: (0, j))],
        out_specs=pl.BlockSpec((N_COND, tn), lambda j: (0, j)),
        out_shape=jax.ShapeDtypeStruct((N_COND, 6 * D_MODEL), F32),
        compiler_params=_params("arbitrary"),
        name="modulation",
    )(cond, w, b.reshape(1, -1))
    return out.reshape(N_COND, 6, D_MODEL)


def _mod_spec(tm, base=0):
    return pl.BlockSpec((1, 6, D_MODEL), lambda i, *_: (_cond_of_row((i + base) * tm), 0, 0))


def _full(shape):
    nd = len(shape)
    return pl.BlockSpec(shape, lambda *_: (0,) * nd, pipeline_mode=pl.Buffered(1))


def _rope_cos_sin(n_tok, rot_dim):
    rows = n_tok // GRID_W
    row = jnp.repeat(jnp.arange(rows, dtype=F32), GRID_W)
    col = jnp.tile(jnp.arange(GRID_W, dtype=F32), rows)
    n_freq = rot_dim // 4
    inv_freq = ROPE_THETA ** (-jnp.arange(n_freq, dtype=F32) / n_freq)
    ang = jnp.concatenate([row[:, None] * inv_freq, col[:, None] * inv_freq], axis=-1)
    return jnp.cos(ang), jnp.sin(ang)


def _rope_tables(rot_dim, lead, tm):
    cos, sin = _rope_cos_sin(DEC_SEQ, rot_dim)
    gap = rot_dim // 2 if lead else 0
    tail = LANES - lead - gap - rot_dim
    n = DEC_SEQ

    def slab(lead_val, rot):
        return jnp.concatenate([jnp.full((rot.shape[0], lead), lead_val, F32),
                                jnp.zeros((rot.shape[0], gap), F32), rot,
                                jnp.zeros((rot.shape[0], tail), F32)], axis=-1)

    c = slab(1.0, jnp.concatenate([cos, cos], axis=-1))
    s = slab(0.0, jnp.concatenate([sin, sin], axis=-1))
    c_id = slab(1.0, jnp.ones((tm, rot_dim), F32))
    s_id = jnp.zeros((tm, LANES), F32)
    del n
    return jnp.concatenate([c, c_id], axis=0), jnp.concatenate([s, s_id], axis=0)


MLA_A_COLS = MLA_Q_RANK + MLA_KV_RANK + 2 * LANES
MLA_ROPE_LEAD = MLA_NOPE
MLA_ROPE_OFF = MLA_NOPE + MLA_ROPE // 2


def _mla_proj_kernel(x_ref, mod_ref, n1_ref, wa_ref, qn_ref, wuq_ref, kvn_ref,
                     cq_ref, sq_ref, ck_ref, sk_ref,
                     q_ref, ckv_ref, kp_ref, kst_ref):
    h = _modulated(x_ref[...], n1_ref[...], mod_ref[0], 0).astype(BF16)
    a = _dot(h, wa_ref[...])
    cq = _rms(a[:, :MLA_Q_RANK], qn_ref[...]).astype(BF16)
    ckv_ref[...] = _rms(a[:, MLA_Q_RANK:MLA_Q_RANK + MLA_KV_RANK], kvn_ref[...])
    k0 = MLA_Q_RANK + MLA_KV_RANK
    kp = a[:, k0:k0 + LANES]
    kp_ref[...] = _rope_slab(kp, ck_ref[...], sk_ref[...], MLA_ROPE // 2).astype(BF16)
    kst_ref[...] = a[:, k0 + LANES:k0 + LANES + MLA_ROPE]
    qa = _dot(cq, wuq_ref[...])
    cq_t = cq_ref[...]
    sq_t = sq_ref[...]
    for hd in range(MLA_HEADS):
        sl = slice(hd * LANES, (hd + 1) * LANES)
        q_ref[:, sl] = _rope_slab(qa[:, sl], cq_t, sq_t, MLA_ROPE // 2).astype(BF16)


def _mla_proj(x, mods, norm1, wa_ext, q_norm, wuq_ext, kv_norm, tabs):
    cq, sq, ck, sk = tabs
    n_tiles = T_ALL // TM
    row = lambda i: (i, 0)
    tab = pl.BlockSpec((TM, LANES), lambda i: (_pos_block(i, TM), 0))
    return pl.pallas_call(
        _mla_proj_kernel,
        grid=(n_tiles,),
        in_specs=[pl.BlockSpec((TM, D_MODEL), row), _mod_spec(TM), _full((1, D_MODEL)),
                  _full((D_MODEL, MLA_A_COLS)), _full((1, MLA_Q_RANK)),
                  _full((MLA_Q_RANK, MLA_HEADS * LANES)), _full((1, MLA_KV_RANK)),
                  tab, tab, tab, tab],
        out_specs=[pl.BlockSpec((TM, MLA_HEADS * LANES), row), pl.BlockSpec((TM, MLA_KV_RANK), row),
                   pl.BlockSpec((TM, LANES), row), pl.BlockSpec((TM, MLA_ROPE), row)],
        out_shape=[jax.ShapeDtypeStruct((T_ALL, MLA_HEADS * LANES), BF16),
                   jax.ShapeDtypeStruct((T_ALL, MLA_KV_RANK), F32),
                   jax.ShapeDtypeStruct((T_ALL, LANES), BF16),
                   jax.ShapeDtypeStruct((T_ALL, MLA_ROPE), F32)],
        compiler_params=_params("arbitrary"),
        name="mla_proj",
    )(x, mods, norm1.reshape(1, -1), wa_ext, q_norm.reshape(1, -1), wuq_ext, kv_norm.reshape(1, -1),
      cq, sq, ck, sk)


def _mla_attn_kernel(*refs, n_lat, n_cache):
    if n_cache:
        (q_ref, ckv_ref, kp_ref, cckv_ref, ckp_ref, wk_ref, wv_ref, wo_ref, x_ref, mod_ref,
         o_ref, k_s, v_s, o_s) = refs
    else:
        (q_ref, ckv_ref, kp_ref, wk_ref, wv_ref, wo_ref, x_ref, mod_ref, o_ref, k_s, v_s, o_s) = refs

    def expand(ckv_of, kp_of, off, rows):
        rc = min(rows, 256)
        for r0 in range(0, rows, rc):
            c = ckv_of(r0, rc).astype(BF16)
            kp = kp_of(r0, rc).astype(F32)
            kn = _dot(c, wk_ref[...])
            v_s[off + r0:off + r0 + rc, :] = _dot(c, wv_ref[...]).astype(BF16)
            for hd in range(MLA_HEADS):
                sl = slice(hd * LANES, (hd + 1) * LANES)
                k_s[off + r0:off + r0 + rc, sl] = (kn[:, sl] + kp).astype(BF16)

    @pl.when(pl.program_id(1) == 0)
    def _():
        if n_cache:
            expand(lambda r, n: cckv_ref[0, r:r + n, :], lambda r, n: ckp_ref[0, r:r + n, :], 0, n_cache)
        expand(lambda r, n: ckv_ref[r:r + n, :], lambda r, n: kp_ref[r:r + n, :], n_cache, n_lat)

    def head(hd, v_pair):
        sl = slice(hd * LANES, (hd + 1) * LANES)
        s = _dot_nt(q_ref[:, sl], k_s[:, sl])
        m = jnp.max(s, axis=-1, keepdims=True)
        p = jnp.exp(s - m)
        l = jnp.sum(p, axis=-1, keepdims=True)
        return _dot(p.astype(BF16), v_pair) / l

    low_half = lax.broadcasted_iota(jnp.int32, (q_ref.shape[0], LANES), 1) < MLA_V
    for hp in range(MLA_HEADS // 2):
        sl = slice(hp * LANES, (hp + 1) * LANES)
        v_pair = v_s[:, sl]
        o_s[:, sl] = jnp.where(low_half, head(2 * hp, v_pair), head(2 * hp + 1, v_pair)).astype(BF16)
    mix = _dot(o_s[...], wo_ref[...])
    o_ref[...] = x_ref[...] + mod_ref[0][2:3] * mix


def _mla_attn(x, mods, q, ckv, kp, wk_ext, wv_ext, wo_ext, cache=None):
    hw = MLA_HEADS * LANES
    if cache is None:
        n_lat, n_cache, nb, nq, base = SEQ, 0, BATCH, SEQ // TQ, 0
    else:
        n_lat, n_cache, nb, nq, base = DEC_SEQ, PAST_LEN, DEC_BATCH, DEC_SEQ // TQ, T_CTX // TQ
    seq_base = base * TQ // n_lat
    tile = lambda b, qi: (base + b * nq + qi, 0)
    seq = lambda b, qi: (seq_base + b, 0)
    in_specs = [pl.BlockSpec((TQ, hw), tile),
                pl.BlockSpec((n_lat, MLA_KV_RANK), seq, pipeline_mode=pl.Buffered(1)),
                pl.BlockSpec((n_lat, LANES), seq, pipeline_mode=pl.Buffered(1))]
    args = [q, ckv, kp]
    if cache is not None:
        in_specs += [pl.BlockSpec((1, n_cache, MLA_KV_RANK), lambda b, qi: (b, 0, 0)),
                     pl.BlockSpec((1, n_cache, LANES), lambda b, qi: (b, 0, 0))]
        args += list(cache)
    vw = MLA_HEADS * MLA_V
    in_specs += [_full((MLA_KV_RANK, hw)), _full((MLA_KV_RANK, vw)), _full((vw, D_MODEL)),
                 pl.BlockSpec((TQ, D_MODEL), tile),
                 pl.BlockSpec((1, 6, D_MODEL), lambda b, qi: (_cond_of_row((base + b * nq + qi) * TQ), 0, 0))]
    args += [wk_ext, wv_ext, wo_ext, x, mods]
    sk = n_cache + n_lat
    return pl.pallas_call(
        functools.partial(_mla_attn_kernel, n_lat=n_lat, n_cache=n_cache),
        grid=(nb, nq),
        in_specs=in_specs,
        out_specs=pl.BlockSpec((TQ, D_MODEL), tile),
        out_shape=jax.ShapeDtypeStruct((T_ALL, D_MODEL), F32),
        scratch_shapes=[pltpu.VMEM((sk, hw), BF16), pltpu.VMEM((sk, vw), BF16), pltpu.VMEM((TQ, vw), BF16)],
        input_output_aliases={len(args) - 2: 0},
        compiler_params=_params("arbitrary", "arbitrary"),
        name="mla_attn_lat" if cache is not None else "mla_attn_ctx",
    )(*args)


SWA_QW = SWA_HEADS * LANES
SWA_KW = SWA_KV_HEADS * LANES
SWA_CW = 2 * SWA_KV_HEADS * SWA_HEAD_DIM


def _swa_proj_kernel(*refs, rope):
    if rope:
        (x_ref, mod_ref, n1_ref, w_ref, b_ref, cq_ref, sq_ref, ck_ref, sk_ref, q_ref, k_ref, v_ref) = refs
    else:
        (x_ref, mod_ref, n1_ref, w_ref, b_ref, q_ref, k_ref, v_ref, kv_ref) = refs
    h = _modulated(x_ref[...], n1_ref[...], mod_ref[0], 0).astype(BF16)
    a = _dot(h, w_ref[...]) + b_ref[...]
    if rope:
        cq, sq, ck, sk = cq_ref[...], sq_ref[...], ck_ref[...], sk_ref[...]
        for hd in range(SWA_HEADS):
            sl = slice(hd * LANES, (hd + 1) * LANES)
            q_ref[:, sl] = _rope_slab(a[:, sl], cq, sq, SWA_HEAD_DIM // 2).astype(BF16)
        for hd in range(SWA_KV_HEADS):
            sl = slice(hd * LANES, (hd + 1) * LANES)
            k_ref[:, sl] = _rope_slab(a[:, SWA_QW + hd * LANES:SWA_QW + (hd + 1) * LANES], ck, sk,
                                      SWA_HEAD_DIM // 2).astype(BF16)
    else:
        q_ref[...] = (a[:, :SWA_QW] * SWA_SCALE).astype(BF16)
        k_ref[...] = a[:, SWA_QW:SWA_QW + SWA_KW].astype(BF16)
        kv_ref[...] = a[:, SWA_QW + 2 * SWA_KW:]
    v_ref[...] = a[:, SWA_QW + SWA_KW:SWA_QW + 2 * SWA_KW].astype(BF16)


def _swa_proj(x, mods, norm1, w_ext, b_ext, tabs, rope):
    base = T_CTX // TM if rope else 0
    n_tiles = (T_LAT if rope else T_CTX) // TM
    n_rows = n_tiles * TM
    ncols = w_ext.shape[1]
    row = lambda i: (i, 0)
    in_specs = [pl.BlockSpec((TM, D_MODEL), lambda i: (i + base, 0)), _mod_spec(TM, base), _full((1, D_MODEL)),
                _full((D_MODEL, ncols)), _full((1, ncols))]
    args = [x, mods, norm1.reshape(1, -1), w_ext, b_ext]
    out_specs = [pl.BlockSpec((TM, SWA_QW), row), pl.BlockSpec((TM, SWA_KW), row), pl.BlockSpec((TM, SWA_KW), row)]
    out_shape = [jax.ShapeDtypeStruct((n_rows, SWA_QW), BF16), jax.ShapeDtypeStruct((n_rows, SWA_KW), BF16),
                 jax.ShapeDtypeStruct((n_rows, SWA_KW), BF16)]
    if rope:
        tab = pl.BlockSpec((TM, LANES), lambda i: (_pos_block(i + base, TM), 0))
        in_specs += [tab] * 4
        args += list(tabs)
    else:
        out_specs.append(pl.BlockSpec((TM, SWA_CW), row))
        out_shape.append(jax.ShapeDtypeStruct((n_rows, SWA_CW), F32))
    return pl.pallas_call(
        functools.partial(_swa_proj_kernel, rope=rope),
        grid=(n_tiles,),
        in_specs=in_specs, out_specs=out_specs, out_shape=out_shape,
        compiler_params=_params("arbitrary"),
        name="swa_proj_lat" if rope else "swa_proj_ctx",
    )(*args)


def _swa_ctx_kernel(q_ref, k_ref, v_ref, sink_ref, wo_ref, bo_ref, x_ref, mod_ref, o_ref, o_s):
    for n in range(SWA_KV_HEADS):
        kn = k_ref[:, n * LANES:(n + 1) * LANES]
        vn = v_ref[:, n * LANES:(n + 1) * LANES]
        for g in range(SWA_GROUP):
            j = n * SWA_GROUP + g
            sl = slice(j * LANES, (j + 1) * LANES)
            s = _dot_nt(q_ref[:, sl], kn)
            sink = sink_ref[j]
            m = jnp.maximum(jnp.max(s, axis=-1, keepdims=True), sink)
            p = jnp.exp(s - m)
            l = jnp.sum(p, axis=-1, keepdims=True) + jnp.exp(sink - m)
            o_s[:, sl] = (_dot(p.astype(BF16), vn) / l).astype(BF16)
    mix = _dot(o_s[...], wo_ref[...]) + bo_ref[...]
    o_ref[...] = x_ref[...] + mod_ref[0][2:3] * mix


def _swa_lat_kernel(q_ref, k_ref, v_ref, ck_ref, cv_ref, sink_ref, wo_ref, bo_ref, x_ref, mod_ref, o_ref, o_s):
    span = TQ + 2 * SWA_WINDOW
    start = pl.program_id(1) * TQ
    ks = pl.multiple_of(jnp.clip(start - SWA_WINDOW, 0, DEC_SEQ - span), SWA_WINDOW)
    qpos = start + lax.broadcasted_iota(jnp.int32, (TQ, span), 0)
    kpos = ks + lax.broadcasted_iota(jnp.int32, (TQ, span), 1)
    valid = jnp.abs(kpos - qpos) <= SWA_WINDOW
    for n in range(SWA_KV_HEADS):
        nl = slice(n * LANES, (n + 1) * LANES)
        kc = ck_ref[0, :, nl]
        vc = cv_ref[0, :, nl]
        kl = k_ref[pl.ds(ks, span), nl]
        vl = v_ref[pl.ds(ks, span), nl]
        for g in range(SWA_GROUP):
            j = n * SWA_GROUP + g
            sl = slice(j * LANES, (j + 1) * LANES)
            qj = q_ref[:, sl]
            s_c = _dot_nt(qj, kc)
            s_l = jnp.where(valid, _dot_nt(qj, kl), -jnp.inf)
            sink = sink_ref[j]
            m = jnp.maximum(jnp.maximum(jnp.max(s_c, axis=-1, keepdims=True),
                                        jnp.max(s_l, axis=-1, keepdims=True)), sink)
            p_c = jnp.exp(s_c - m)
            p_l = jnp.exp(s_l - m)
            l = (jnp.sum(p_c, axis=-1, keepdims=True) + jnp.sum(p_l, axis=-1, keepdims=True)
                 + jnp.exp(sink - m))
            o = _dot(p_c.astype(BF16), vc) + _dot(p_l.astype(BF16), vl)
            o_s[:, sl] = (o / l).astype(BF16)
    mix = _dot(o_s[...], wo_ref[...]) + bo_ref[...]
    o_ref[...] = x_ref[...] + mod_ref[0][2:3] * mix


def _swa_attn(x, mods, q, k, v, sink, wo_ext, bo, cache=None):
    smem = pl.BlockSpec(memory_space=pltpu.SMEM)
    if cache is None:
        nq, base = 1, 0
        tile = lambda b, qi: (b, 0)
        in_specs = [pl.BlockSpec((TQ, SWA_QW), tile), pl.BlockSpec((SEQ, SWA_KW), tile),
                    pl.BlockSpec((SEQ, SWA_KW), tile)]
        args = [q, k, v]
        kern, nb, name = _swa_ctx_kernel, BATCH, "swa_attn_ctx"
    else:
        nq, base = DEC_SEQ // TQ, T_CTX // TQ
        tile = lambda b, qi: (b * nq + qi, 0)
        seq = lambda b, qi: (b, 0)
        cspec = pl.BlockSpec((1, PAST_LEN, SWA_KW), lambda b, qi: (b, 0, 0))
        in_specs = [pl.BlockSpec((TQ, SWA_QW), tile), pl.BlockSpec((DEC_SEQ, SWA_KW), seq),
                    pl.BlockSpec((DEC_SEQ, SWA_KW), seq), cspec, cspec]
        args = [q, k, v, cache[0], cache[1]]
        kern, nb, name = _swa_lat_kernel, DEC_BATCH, "swa_attn_lat"
    xtile = lambda b, qi: (base + b * nq + qi, 0)
    in_specs += [smem, _full((SWA_QW, D_MODEL)), _full((1, D_MODEL)), pl.BlockSpec((TQ, D_MODEL), xtile),
                 pl.BlockSpec((1, 6, D_MODEL), lambda b, qi: (_cond_of_row((base + b * nq + qi) * TQ), 0, 0))]
    args += [sink, wo_ext, bo.reshape(1, -1), x, mods]
    return pl.pallas_call(
        kern,
        grid=(nb, nq),
        in_specs=in_specs,
        out_specs=pl.BlockSpec((TQ, D_MODEL), xtile),
        out_shape=jax.ShapeDtypeStruct((T_ALL, D_MODEL), F32),
        scratch_shapes=[pltpu.VMEM((TQ, SWA_QW), BF16)],
        input_output_aliases={len(args) - 2: 0},
        compiler_params=_params("arbitrary", "arbitrary"),
        name=name,
    )(*args)


CONV_EXT = CONV_TM + 2 * CONV_HALO


def _conv_kernel(xp_ref, x_ref, xn_ref, mod_ref, n1_ref, w1_ref, b1_ref, dw_ref, dwb_ref, lng_ref, lnb_ref,
                 w2_ref, b2_ref, o_ref, ext_s, y_s):
    i = pl.program_id(0)
    tiles_per_seq = DEC_SEQ // CONV_TM
    j = (i - T_CTX // CONV_TM) % tiles_per_seq
    latent = i >= T_CTX // CONV_TM
    left_ok = jnp.logical_and(latent, j > 0)
    right_ok = jnp.logical_and(latent, j < tiles_per_seq - 1)
    mod = mod_ref[0]

    def glu_of(xv):
        h = _modulated(xv, n1_ref[...], mod, 0).astype(BF16)
        u = _dot(h, w1_ref[...]) + b1_ref[...]
        return u[:, :D_MODEL] * _sigmoid(u[:, D_MODEL:])

    ext_s[CONV_HALO:CONV_HALO + CONV_TM, :] = glu_of(x_ref[...])
    ext_s[0:CONV_HALO, :] = jnp.where(left_ok, glu_of(xp_ref[...]), 0.0)
    ext_s[CONV_HALO + CONV_TM:, :] = jnp.where(right_ok, glu_of(xn_ref[...]), 0.0)

    rows = 128
    first = CONV_HALO - CONV_PAD
    for c in range(D_MODEL // LANES):
        cl = slice(c * LANES, (c + 1) * LANES)
        for r in range(CONV_TM // rows):
            acc = jnp.zeros((rows, LANES), F32)
            for w in range(CONV_WIDTH):
                r0 = r * rows + first + w
                acc = acc + ext_s[r0:r0 + rows, cl] * dw_ref[w:w + 1, cl]
            y_s[r * rows:(r + 1) * rows, cl] = acc
    y = y_s[...] + dwb_ref[...]
    mu = jnp.mean(y, axis=-1, keepdims=True)
    yc = y - mu
    var = jnp.mean(yc * yc, axis=-1, keepdims=True)
    yn = yc * lax.rsqrt(var + NORM_EPS) * lng_ref[...] + lnb_ref[...]
    z = (yn * _sigmoid(yn)).astype(BF16)
    mix = _dot(z, w2_ref[...]) + b2_ref[...]
    o_ref[...] = x_ref[...] + mod[2:3] * mix


def _conv_layer(x, mods, norm1, w1, b1, dw, dwb, ln_g, ln_b, w2, b2):
    n_tiles = T_ALL // CONV_TM
    per = CONV_TM // CONV_HALO
    n_halo_blocks = T_ALL // CONV_HALO
    row = lambda i: (i, 0)
    vec = lambda v: v.reshape(1, -1)
    return pl.pallas_call(
        _conv_kernel,
        grid=(n_tiles,),
        in_specs=[pl.BlockSpec((CONV_HALO, D_MODEL), lambda i: (jnp.maximum(i * per - 1, 0), 0)),
                  pl.BlockSpec((CONV_TM, D_MODEL), row),
                  pl.BlockSpec((CONV_HALO, D_MODEL), lambda i: (jnp.minimum((i + 1) * per, n_halo_blocks - 1), 0)),
                  _mod_spec(CONV_TM), _full((1, D_MODEL)),
                  _full((D_MODEL, 2 * D_MODEL)), _full((1, 2 * D_MODEL)),
                  _full((CONV_WIDTH, D_MODEL)), _full((1, D_MODEL)), _full((1, D_MODEL)), _full((1, D_MODEL)),
                  _full((D_MODEL, D_MODEL)), _full((1, D_MODEL))],
        out_specs=pl.BlockSpec((CONV_TM, D_MODEL), row),
        out_shape=jax.ShapeDtypeStruct((T_ALL, D_MODEL), F32),
        scratch_shapes=[pltpu.VMEM((CONV_EXT, D_MODEL), F32), pltpu.VMEM((CONV_TM, D_MODEL), F32)],
        compiler_params=_params("arbitrary"),
        name="conv_module",
    )(x, x, x, mods, vec(norm1), w1.astype(BF16), vec(b1), dw, vec(dwb), vec(ln_g), vec(ln_b),
      w2.astype(BF16), vec(b2))


def _router_kernel(x_ref, mod_ref, n2_ref, rw_ref, rb_ref, h_ref, idx_ref, gate_ref):
    h = _modulated(x_ref[...], n2_ref[...], mod_ref[0], 3).astype(BF16)
    h_ref[...] = h
    cur = _dot_nt(rw_ref[...], h) + rb_ref[...]
    e_iota = lax.broadcasted_iota(jnp.int32, cur.shape, 0)
    vals, idxs = [], []
    for _ in range(TOP_K):
        m = jnp.max(cur, axis=0, keepdims=True)
        idx = jnp.min(jnp.where(cur == m, e_iota, N_EXPERTS), axis=0, keepdims=True)
        vals.append(m)
        idxs.append(idx)
        cur = jnp.where(e_iota == idx, -jnp.inf, cur)
    ex = [jnp.exp(v - vals[0]) for v in vals]
    tot = ex[0] + ex[1] + ex[2] + ex[3]
    for k in range(TOP_K):
        idx_ref[k:k + 1, :] = idxs[k]
        gate_ref[k:k + 1, :] = ex[k] / tot


def _router(x, mods, norm2, router_w, router_b):
    n_tiles = T_ALL // TM
    row = lambda i: (i, 0)
    col = lambda i: (0, i)
    return pl.pallas_call(
        _router_kernel,
        grid=(n_tiles,),
        in_specs=[pl.BlockSpec((TM, D_MODEL), row), _mod_spec(TM), _full((1, D_MODEL)),
                  _full((N_EXPERTS, D_MODEL)), _full((N_EXPERTS, 1))],
        out_specs=[pl.BlockSpec((TM, D_MODEL), row), pl.BlockSpec((TOP_K, TM), col),
                   pl.BlockSpec((TOP_K, TM), col)],
        out_shape=[jax.ShapeDtypeStruct((T_ALL, D_MODEL), BF16),
                   jax.ShapeDtypeStruct((TOP_K, T_ALL), jnp.int32),
                   jax.ShapeDtypeStruct((TOP_K, T_ALL), F32)],
        compiler_params=_params("arbitrary"),
        name="router",
    )(x, mods, norm2.reshape(1, -1), router_w.T.astype(BF16), router_b.reshape(-1, 1))


def _dispatch_plan(idx_t):
    tk = TOP_K * T_ALL
    e_flat = idx_t.reshape(tk)
    onehot = (e_flat[:, None] == jnp.arange(N_EXPERTS, dtype=jnp.int32)[None, :]).astype(jnp.int32)
    csum = jnp.cumsum(onehot, axis=0)
    rank = jnp.sum(csum * onehot, axis=1) - 1
    counts = csum[-1]
    padded = (counts + MOE_BM - 1) // MOE_BM * MOE_BM
    pad_end = jnp.cumsum(padded)
    pad_start = pad_end - padded
    dest = pad_start[e_flat] + rank
    tok = jnp.arange(tk, dtype=jnp.int32) % T_ALL
    slot_tok = jnp.zeros((MOE_CAP,), jnp.int32).at[dest].set(tok)
    block_e = jnp.minimum(jnp.searchsorted(pad_end, jnp.arange(MOE_BLOCKS, dtype=jnp.int32) * MOE_BM, side="right"),
                          N_EXPERTS - 1).astype(jnp.int32)
    n_used = (pad_end[-1] // MOE_BM).astype(jnp.int32).reshape(1)
    return slot_tok, dest.astype(jnp.int32), block_e, n_used


def _moe_kernel(be_ref, nu_ref, xs_ref, wg_ref, wu_ref, wd_ref, bg_ref, bu_ref, bd_ref, ys_ref, wg_s, wu_s, wd_s):
    i = pl.program_id(0)
    prev = be_ref[jnp.maximum(i - 1, 0)]
    fresh = jnp.logical_or(i == 0, be_ref[i] != prev)

    @pl.when(fresh)
    def _():
        wg_s[...] = wg_ref[0].astype(BF16)
        wu_s[...] = wu_ref[0].astype(BF16)
        wd_s[...] = wd_ref[0].astype(BF16)

    @pl.when(i < nu_ref[0])
    def _():
        x = xs_ref[...]
        g = _dot(x, wg_s[...]) + bg_ref[0]
        u = _dot(x, wu_s[...]) + bu_ref[0]
        g = jnp.minimum(g, SWIGLU_LIMIT)
        u = jnp.clip(u, -SWIGLU_LIMIT, SWIGLU_LIMIT)
        act = (u + 1.0) * (g * _sigmoid(SWIGLU_ALPHA * g))
        ys_ref[...] = _dot(act.astype(BF16), wd_s[...]) + bd_ref[0]

    @pl.when(i >= nu_ref[0])
    def _():
        ys_ref[...] = jnp.zeros_like(ys_ref)


def _moe_experts(xs, block_e, n_used, wg, bg, wu, bu, wd, bd):
    wspec = pl.BlockSpec((1, D_MODEL, D_MODEL), lambda i, be, nu: (be[i], 0, 0))
    bspec = pl.BlockSpec((1, 1, D_MODEL), lambda i, be, nu: (be[i], 0, 0))
    row = lambda i, be, nu: (i, 0)
    b3 = lambda b: b.reshape(N_EXPERTS, 1, D_MODEL)
    return pl.pallas_call(
        _moe_kernel,
        grid_spec=pltpu.PrefetchScalarGridSpec(
            num_scalar_prefetch=2,
            grid=(MOE_BLOCKS,),
            in_specs=[pl.BlockSpec((MOE_BM, D_MODEL), row), wspec, wspec, wspec, bspec, bspec, bspec],
            out_specs=pl.BlockSpec((MOE_BM, D_MODEL), row),
            scratch_shapes=[pltpu.VMEM((D_MODEL, D_MODEL), BF16)] * 3),
        out_shape=jax.ShapeDtypeStruct((MOE_CAP, D_MODEL), F32),
        compiler_params=_params("arbitrary"),
        name="moe_experts",
    )(block_e, n_used, xs, wg, wu, wd, b3(bg), b3(bu), b3(bd))


def _combine_kernel(x_ref, yg_ref, gate_ref, mod_ref, o_ref):
    gate = gate_ref[...]
    acc = yg_ref[0] * gate[:, 0:1]
    for k in range(1, TOP_K):
        acc = acc + yg_ref[k] * gate[:, k:k + 1]
    o_ref[...] = x_ref[...] + mod_ref[0][5:6] * acc


def _combine(x, yg, gates, mods):
    row = lambda i: (i, 0)
    return pl.pallas_call(
        _combine_kernel,
        grid=(T_ALL // TM,),
        in_specs=[pl.BlockSpec((TM, D_MODEL), row), pl.BlockSpec((TOP_K, TM, D_MODEL), lambda i: (0, i, 0)),
                  pl.BlockSpec((TM, TOP_K), row), _mod_spec(TM)],
        out_specs=pl.BlockSpec((TM, D_MODEL), row),
        out_shape=jax.ShapeDtypeStruct((T_ALL, D_MODEL), F32),
        input_output_aliases={0: 0},
        compiler_params=_params("arbitrary"),
        name="moe_combine",
    )(x, yg, gates, mods)


def _moe_layer(x, mods, norm2, router_w, router_b, wg, bg, wu, bu, wd, bd):
    h, idx_t, gate_t = _router(x, mods, norm2, router_w, router_b)
    slot_tok, dest, block_e, n_used = _dispatch_plan(idx_t)
    xs = h.at[slot_tok].get(mode="promise_in_bounds")
    ys = _moe_experts(xs, block_e, n_used, wg, bg, wu, bu, wd, bd)
    yg = ys.at[dest].get(mode="promise_in_bounds").reshape(TOP_K, T_ALL, D_MODEL)
    return _combine(x, yg, gate_t.T, mods)


def _final_norm_kernel(x_ref, g_ref, o_ref):
    o_ref[...] = _rms(x_ref[...], g_ref[...])


def _final_norm(x, g, base_rows, n_rows):
    base = base_rows // TM
    return pl.pallas_call(
        _final_norm_kernel,
        grid=(n_rows // TM,),
        in_specs=[pl.BlockSpec((TM, D_MODEL), lambda i: (i + base, 0)), _full((1, D_MODEL))],
        out_specs=pl.BlockSpec((TM, D_MODEL), lambda i: (i, 0)),
        out_shape=jax.ShapeDtypeStruct((n_rows, D_MODEL), F32),
        compiler_params=_params("arbitrary"),
        name="final_norm",
    )(x, g.reshape(1, -1))


def _pad_lanes(w, lead_pad, width):
    tail = width - lead_pad - w.shape[-1]
    cfg = [(0, 0)] * (w.ndim - 1) + [(lead_pad, tail)]
    return jnp.pad(w, cfg)


def _mla_weights(wa, wuq, wukv, wo):
    krope_w = wa[:, MLA_Q_RANK + MLA_KV_RANK:]
    wa_ext = jnp.concatenate([wa[:, :MLA_Q_RANK + MLA_KV_RANK],
                              _pad_lanes(krope_w, MLA_ROPE_OFF, LANES),
                              _pad_lanes(krope_w, 0, LANES)], axis=1).astype(BF16)
    q3 = wuq.reshape(MLA_Q_RANK, MLA_HEADS, MLA_NOPE + MLA_ROPE)
    wuq_ext = jnp.concatenate([q3[..., :MLA_NOPE], _pad_lanes(q3[..., MLA_NOPE:], MLA_ROPE // 2, LANES - MLA_NOPE)],
                              axis=-1).reshape(MLA_Q_RANK, MLA_HEADS * LANES).astype(BF16)
    kv3 = wukv.reshape(MLA_KV_RANK, MLA_HEADS, MLA_NOPE + MLA_V)
    wk_ext = _pad_lanes(kv3[..., :MLA_NOPE], 0, LANES).reshape(MLA_KV_RANK, MLA_HEADS * LANES).astype(BF16)
    wv = kv3[..., MLA_NOPE:].reshape(MLA_KV_RANK, MLA_HEADS * MLA_V).astype(BF16)
    return wa_ext, wuq_ext, wk_ext, wv, wo.astype(BF16)


def _swa_weights(wqkv, bqkv, wo):
    nq = SWA_HEADS * SWA_HEAD_DIM
    nkv = SWA_KV_HEADS * SWA_HEAD_DIM

    def slabs(w, heads):
        lead = w.shape[:-1]
        return _pad_lanes(w.reshape(lead + (heads, SWA_HEAD_DIM)), 0, LANES).reshape(lead + (heads * LANES,))

    def ext(w, with_compact):
        parts = [slabs(w[..., :nq], SWA_HEADS), slabs(w[..., nq:nq + nkv], SWA_KV_HEADS),
                 slabs(w[..., nq + nkv:], SWA_KV_HEADS)]
        if with_compact:
            parts.append(w[..., nq:])
        return jnp.concatenate(parts, axis=-1)

    b2 = bqkv.reshape(1, -1)
    wo3 = wo.reshape(SWA_HEADS, SWA_HEAD_DIM, D_MODEL)
    wo_ext = jnp.pad(wo3, ((0, 0), (0, LANES - SWA_HEAD_DIM), (0, 0))).reshape(SWA_QW, D_MODEL).astype(BF16)
    return (ext(wqkv, True).astype(BF16), ext(b2, True), ext(wqkv, False).astype(BF16), ext(b2, False), wo_ext)


def kernel(x_prompt, x_sample, cache_l0_ckv, cache_l0_krope, cache_l1_k, cache_l1_v, cache_l3_ckv, cache_l3_krope, c, c_ctx, l0_mla_wa, l0_mla_q_norm, l0_mla_wuq, l0_mla_kv_norm, l0_mla_wukv, l0_mla_wo, l0_mod_w, l0_mod_b, l0_norm1, l0_norm2, l0_router_w, l0_router_b, l0_moe_wg, l0_moe_bg, l0_moe_wu, l0_moe_bu, l0_moe_wd, l0_moe_bd, l1_swa_wqkv, l1_swa_bqkv, l1_swa_sink, l1_swa_wo, l1_swa_bo, l1_mod_w, l1_mod_b, l1_norm1, l1_norm2, l1_router_w, l1_router_b, l1_moe_wg, l1_moe_bg, l1_moe_wu, l1_moe_bu, l1_moe_wd, l1_moe_bd, l2_conv_w1, l2_conv_b1, l2_conv_dw, l2_conv_dwb, l2_conv_ln_g, l2_conv_ln_b, l2_conv_w2, l2_conv_b2, l2_mod_w, l2_mod_b, l2_norm1, l2_norm2, l2_router_w, l2_router_b, l2_moe_wg, l2_moe_bg, l2_moe_wu, l2_moe_bu, l2_moe_wd, l2_moe_bd, l3_mla_wa, l3_mla_q_norm, l3_mla_wuq, l3_mla_kv_norm, l3_mla_wukv, l3_mla_wo, l3_mod_w, l3_mod_b, l3_norm1, l3_norm2, l3_router_w, l3_router_b, l3_moe_wg, l3_moe_bg, l3_moe_wu, l3_moe_bu, l3_moe_wd, l3_moe_bd, final_norm):
    x = jnp.concatenate([x_prompt.reshape(T_CTX, D_MODEL), x_sample.reshape(T_LAT, D_MODEL)], axis=0)
    cond = jnp.concatenate([c, c_ctx[None, :], jnp.zeros((N_COND - DEC_BATCH - 1, D_MODEL), F32)], axis=0)

    mla_c, mla_s = _rope_tables(MLA_ROPE, MLA_ROPE_LEAD, TM)
    mla_tabs = (mla_c * MLA_SCALE, mla_s * MLA_SCALE, mla_c, mla_s)
    swa_c, swa_s = _rope_tables(SWA_HEAD_DIM, 0, TM)
    swa_tabs = (swa_c * SWA_SCALE, swa_s * SWA_SCALE, swa_c, swa_s)

    def mla_layer(x, mods, norm1, wa, q_norm, wuq, kv_norm, wukv, wo, cache_ckv, cache_krope):
        wa_ext, wuq_ext, wk_ext, wv_ext, wo_ext = _mla_weights(wa, wuq, wukv, wo)
        q, ckv, kp, kst = _mla_proj(x, mods, norm1, wa_ext, q_norm, wuq_ext, kv_norm, mla_tabs)
        x = _mla_attn(x, mods, q, ckv, kp, wk_ext, wv_ext, wo_ext)
        cache = (cache_ckv, _pad_lanes(cache_krope, MLA_ROPE_OFF, LANES).astype(BF16))
        x = _mla_attn(x, mods, q, ckv, kp, wk_ext, wv_ext, wo_ext, cache)
        return x, (ckv[:T_CTX].reshape(BATCH, SEQ, MLA_KV_RANK), kst[:T_CTX].reshape(BATCH, SEQ, MLA_ROPE))

    def moe(x, mods, norm2, rw, rb, wg, bg, wu, bu, wd, bd):
        return _moe_layer(x, mods, norm2, rw, rb, wg, bg, wu, bu, wd, bd)

    mods = _modulation(cond, l0_mod_w, l0_mod_b)
    x, (st0_ckv, st0_krope) = mla_layer(x, mods, l0_norm1, l0_mla_wa, l0_mla_q_norm, l0_mla_wuq, l0_mla_kv_norm,
                                        l0_mla_wukv, l0_mla_wo, cache_l0_ckv, cache_l0_krope)
    x = moe(x, mods, l0_norm2, l0_router_w, l0_router_b, l0_moe_wg, l0_moe_bg, l0_moe_wu, l0_moe_bu,
            l0_moe_wd, l0_moe_bd)

    mods = _modulation(cond, l1_mod_w, l1_mod_b)
    w_ctx, b_ctx, w_lat, b_lat, swa_wo_ext = _swa_weights(l1_swa_wqkv, l1_swa_bqkv, l1_swa_wo)
    q_c, k_c, v_c, kv_c = _swa_proj(x, mods, l1_norm1, w_ctx, b_ctx, None, rope=False)
    q_l, k_l, v_l = _swa_proj(x, mods, l1_norm1, w_lat, b_lat, swa_tabs, rope=True)
    x = _swa_attn(x, mods, q_c, k_c, v_c, l1_swa_sink, swa_wo_ext, l1_swa_bo)
    nkv = SWA_KV_HEADS * SWA_HEAD_DIM
    cache_k = _pad_lanes(cache_l1_k, 0, LANES).reshape(DEC_BATCH, PAST_LEN, SWA_KW).astype(BF16)
    cache_v = _pad_lanes(cache_l1_v, 0, LANES).reshape(DEC_BATCH, PAST_LEN, SWA_KW).astype(BF16)
    x = _swa_attn(x, mods, q_l, k_l, v_l, l1_swa_sink, swa_wo_ext, l1_swa_bo, (cache_k, cache_v))
    st1_k = kv_c[:, :nkv].reshape(BATCH, SEQ, SWA_KV_HEADS, SWA_HEAD_DIM)
    st1_v = kv_c[:, nkv:].reshape(BATCH, SEQ, SWA_KV_HEADS, SWA_HEAD_DIM)
    x = moe(x, mods, l1_norm2, l1_router_w, l1_router_b, l1_moe_wg, l1_moe_bg, l1_moe_wu, l1_moe_bu,
            l1_moe_wd, l1_moe_bd)

    mods = _modulation(cond, l2_mod_w, l2_mod_b)
    x = _conv_layer(x, mods, l2_norm1, l2_conv_w1, l2_conv_b1, l2_conv_dw, l2_conv_dwb, l2_conv_ln_g,
                    l2_conv_ln_b, l2_conv_w2, l2_conv_b2)
    x = moe(x, mods, l2_norm2, l2_router_w, l2_router_b, l2_moe_wg, l2_moe_bg, l2_moe_wu, l2_moe_bu,
            l2_moe_wd, l2_moe_bd)

    mods = _modulation(cond, l3_mod_w, l3_mod_b)
    x, (st3_ckv, st3_krope) = mla_layer(x, mods, l3_norm1, l3_mla_wa, l3_mla_q_norm, l3_mla_wuq, l3_mla_kv_norm,
                                        l3_mla_wukv, l3_mla_wo, cache_l3_ckv, cache_l3_krope)
    x = moe(x, mods, l3_norm2, l3_router_w, l3_router_b, l3_moe_wg, l3_moe_bg, l3_moe_wu, l3_moe_bu,
            l3_moe_wd, l3_moe_bd)

    y_prompt = _final_norm(x, final_norm, 0, T_CTX).reshape(BATCH, SEQ, D_MODEL)
    y_sample = _final_norm(x, final_norm, T_CTX, T_LAT).reshape(DEC_BATCH, DEC_SEQ, D_MODEL)
    return (y_prompt, y_sample, st0_ckv, st0_krope, st1_k, st1_v, st3_ckv, st3_krope)
```

```python
import functools

import jax
import jax.numpy as jnp
from jax import lax
from jax.experimental import pallas as pl
from jax.experimental.pallas import tpu as pltpu

F32 = jnp.float32
BF16 = jnp.bfloat16

D_MODEL = 1024
BATCH = 32
SEQ = 256
DEPTH = 4
DEC_BATCH = 4
DEC_SEQ = 2048
PAST_LEN = 256
GRID_W = 64
NORM_EPS = 1e-6
ROPE_THETA = 10000.0

MLA_HEADS = 16
MLA_NOPE = 64
MLA_ROPE = 32
MLA_V = 64
MLA_Q_RANK = 384
MLA_KV_RANK = 256
MLA_SCALE = (MLA_NOPE + MLA_ROPE) ** -0.5

SWA_HEADS = 16
SWA_KV_HEADS = 4
SWA_GROUP = 4
SWA_HEAD_DIM = 64
SWA_WINDOW = 128
SWA_SCALE = SWA_HEAD_DIM ** -0.5

CONV_WIDTH = 31
CONV_PAD = CONV_WIDTH // 2

N_EXPERTS = 32
TOP_K = 4
SWIGLU_ALPHA = 1.702
SWIGLU_LIMIT = 7.0

LANES = 128
T_CTX = BATCH * SEQ
T_LAT = DEC_BATCH * DEC_SEQ
T_ALL = T_CTX + T_LAT
CTX_COND = DEC_BATCH
N_COND = 8

TM = 512
TQ = 256
CONV_TM = 256
CONV_HALO = 16
MOE_BM = 256
MOE_BLOCKS = (T_ALL * TOP_K + N_EXPERTS * (MOE_BM - 1) + MOE_BM - 1) // MOE_BM
MOE_CAP = MOE_BLOCKS * MOE_BM
VMEM_LIMIT = 56 * 1024 * 1024


def _params(*sem):
    return pltpu.CompilerParams(dimension_semantics=sem, vmem_limit_bytes=VMEM_LIMIT)


def _cond_of_row(row0):
    return jnp.where(row0 < T_CTX, CTX_COND, (row0 - T_CTX) // DEC_SEQ)


def _pos_block(i, tm):
    row0 = i * tm
    return jnp.where(row0 < T_CTX, DEC_SEQ // tm, ((row0 - T_CTX) % DEC_SEQ) // tm)


def _rms(x, g):
    ms = jnp.mean(x * x, axis=-1, keepdims=True)
    return x * lax.rsqrt(ms + NORM_EPS) * g


def _modulated(x, g, mod, j):
    return _rms(x, g) * (1.0 + mod[j + 1:j + 2]) + mod[j:j + 1]


def _sigmoid(x):
    return 1.0 / (1.0 + jnp.exp(-x))


def _dot(a, b):
    return jnp.dot(a, b, preferred_element_type=F32)


def _dot_nt(a, b):
    return lax.dot_general(a, b, (((1,), (1,)), ((), ())), preferred_element_type=F32)


def _rope_slab(a, c, s, shift):
    return a * c + (pltpu.roll(a, shift, 1) - pltpu.roll(a, LANES - shift, 1)) * s


def _mod_kernel(c_ref, w_ref, b_ref, o_ref):
    c = c_ref[...]
    s = (c * _sigmoid(c)).astype(BF16)
    o_ref[...] = _dot(s, w_ref[...].astype(BF16)) + b_ref[...]


def _modulation(cond, w, b):
    tn = 1536
    out = pl.pallas_call(
        _mod_kernel,
        grid=(6 * D_MODEL // tn,),
        in_specs=[pl.BlockSpec((N_COND, D_MODEL), lambda j: (0, 0)),
                  pl.BlockSpec((D_MODEL, tn), lambda j: (0, j)),
                  pl.BlockSpec((1, tn), lambda j: (0, j))],
        out_specs=pl.BlockSpec((N_COND, tn), lambda j: (0, j)),
        out_shape=jax.ShapeDtypeStruct((N_COND, 6 * D_MODEL), F32),
        compiler_params=_params("arbitrary"),
        name="modulation",
    )(cond, w, b.reshape(1, -1))
    return out.reshape(N_COND, 6, D_MODEL)


def _mod_spec(tm, base=0):
    return pl.BlockSpec((1, 6, D_MODEL), lambda i, *_: (_cond_of_row((i + base) * tm), 0, 0))


def _full(shape):
    nd = len(shape)
    return pl.BlockSpec(shape, lambda *_: (0,) * nd, pipeline_mode=pl.Buffered(1))


def _rope_cos_sin(n_tok, rot_dim):
    rows = n_tok // GRID_W
    row = jnp.repeat(jnp.arange(rows, dtype=F32), GRID_W)
    col = jnp.tile(jnp.arange(GRID_W, dtype=F32), rows)
    n_freq = rot_dim // 4
    inv_freq = ROPE_THETA ** (-jnp.arange(n_freq, dtype=F32) / n_freq)
    ang = jnp.concatenate([row[:, None] * inv_freq, col[:, None] * inv_freq], axis=-1)
    return jnp.cos(ang), jnp.sin(ang)


def _rope_tables(rot_dim, lead, tm):
    cos, sin = _rope_cos_sin(DEC_SEQ, rot_dim)
    gap = rot_dim // 2 if lead else 0
    tail = LANES - lead - gap - rot_dim
    n = DEC_SEQ

    def slab(lead_val, rot):
        return jnp.concatenate([jnp.full((rot.shape[0], lead), lead_val, F32),
                                jnp.zeros((rot.shape[0], gap), F32), rot,
                                jnp.zeros((rot.shape[0], tail), F32)], axis=-1)

    c = slab(1.0, jnp.concatenate([cos, cos], axis=-1))
    s = slab(0.0, jnp.concatenate([sin, sin], axis=-1))
    c_id = slab(1.0, jnp.ones((tm, rot_dim), F32))
    s_id = jnp.zeros((tm, LANES), F32)
    del n
    return jnp.concatenate([c, c_id], axis=0), jnp.concatenate([s, s_id], axis=0)


MLA_A_COLS = MLA_Q_RANK + MLA_KV_RANK + 2 * LANES
MLA_ROPE_LEAD = MLA_NOPE
MLA_ROPE_OFF = MLA_NOPE + MLA_ROPE // 2


def _mla_proj_kernel(x_ref, mod_ref, n1_ref, wa_ref, qn_ref, wuq_ref, kvn_ref,
                     cq_ref, sq_ref, ck_ref, sk_ref,
                     q_ref, ckv_ref, kp_ref, kst_ref):
    h = _modulated(x_ref[...], n1_ref[...], mod_ref[0], 0).astype(BF16)
    a = _dot(h, wa_ref[...])
    cq = _rms(a[:, :MLA_Q_RANK], qn_ref[...]).astype(BF16)
    ckv_ref[...] = _rms(a[:, MLA_Q_RANK:MLA_Q_RANK + MLA_KV_RANK], kvn_ref[...])
    k0 = MLA_Q_RANK + MLA_KV_RANK
    kp = a[:, k0:k0 + LANES]
    kp_ref[...] = _rope_slab(kp, ck_ref[...], sk_ref[...], MLA_ROPE // 2).astype(BF16)
    kst_ref[...] = a[:, k0 + LANES:k0 + LANES + MLA_ROPE]
    qa = _dot(cq, wuq_ref[...])
    cq_t = cq_ref[...]
    sq_t = sq_ref[...]
    for hd in range(MLA_HEADS):
        sl = slice(hd * LANES, (hd + 1) * LANES)
        q_ref[:, sl] = _rope_slab(qa[:, sl], cq_t, sq_t, MLA_ROPE // 2).astype(BF16)


def _mla_proj(x, mods, norm1, wa_ext, q_norm, wuq_ext, kv_norm, tabs):
    cq, sq, ck, sk = tabs
    n_tiles = T_ALL // TM
    row = lambda i: (i, 0)
    tab = pl.BlockSpec((TM, LANES), lambda i: (_pos_block(i, TM), 0))
    return pl.pallas_call(
        _mla_proj_kernel,
        grid=(n_tiles,),
        in_specs=[pl.BlockSpec((TM, D_MODEL), row), _mod_spec(TM), _full((1, D_MODEL)),
                  _full((D_MODEL, MLA_A_COLS)), _full((1, MLA_Q_RANK)),
                  _full((MLA_Q_RANK, MLA_HEADS * LANES)), _full((1, MLA_KV_RANK)),
                  tab, tab, tab, tab],
        out_specs=[pl.BlockSpec((TM, MLA_HEADS * LANES), row), pl.BlockSpec((TM, MLA_KV_RANK), row),
                   pl.BlockSpec((TM, LANES), row), pl.BlockSpec((TM, MLA_ROPE), row)],
        out_shape=[jax.ShapeDtypeStruct((T_ALL, MLA_HEADS * LANES), BF16),
                   jax.ShapeDtypeStruct((T_ALL, MLA_KV_RANK), F32),
                   jax.ShapeDtypeStruct((T_ALL, LANES), BF16),
                   jax.ShapeDtypeStruct((T_ALL, MLA_ROPE), F32)],
        compiler_params=_params("arbitrary"),
        name="mla_proj",
    )(x, mods, norm1.reshape(1, -1), wa_ext, q_norm.reshape(1, -1), wuq_ext, kv_norm.reshape(1, -1),
      cq, sq, ck, sk)


def _mla_attn_kernel(*refs, n_lat, n_cache):
    if n_cache:
        (q_ref, ckv_ref, kp_ref, cckv_ref, ckp_ref, wk_ref, wv_ref, wo_ref, x_ref, mod_ref,
         o_ref, k_s, v_s, o_s) = refs
    else:
        (q_ref, ckv_ref, kp_ref, wk_ref, wv_ref, wo_ref, x_ref, mod_ref, o_ref, k_s, v_s, o_s) = refs

    def expand(ckv_of, kp_of, off, rows):
        rc = min(rows, 256)
        for r0 in range(0, rows, rc):
            c = ckv_of(r0, rc).astype(BF16)
            kp = kp_of(r0, rc).astype(F32)
            kn = _dot(c, wk_ref[...])
            v_s[off + r0:off + r0 + rc, :] = _dot(c, wv_ref[...]).astype(BF16)
            for hd in range(MLA_HEADS):
                sl = slice(hd * LANES, (hd + 1) * LANES)
                k_s[off + r0:off + r0 + rc, sl] = (kn[:, sl] + kp).astype(BF16)

    @pl.when(pl.program_id(1) == 0)
    def _():
        if n_cache:
            expand(lambda r, n: cckv_ref[0, r:r + n, :], lambda r, n: ckp_ref[0, r:r + n, :], 0, n_cache)
        expand(lambda r, n: ckv_ref[r:r + n, :], lambda r, n: kp_ref[r:r + n, :], n_cache, n_lat)

    def head(hd, v_pair):
        sl = slice(hd * LANES, (hd + 1) * LANES)
        s = _dot_nt(q_ref[:, sl], k_s[:, sl])
        m = jnp.max(s, axis=-1, keepdims=True)
        p = jnp.exp(s - m)
        l = jnp.sum(p, axis=-1, keepdims=True)
        return _dot(p.astype(BF16), v_pair) / l

    low_half = lax.broadcasted_iota(jnp.int32, (q_ref.shape[0], LANES), 1) < MLA_V
    for hp in range(MLA_HEADS // 2):
        sl = slice(hp * LANES, (hp + 1) * LANES)
        v_pair = v_s[:, sl]
        o_s[:, sl] = jnp.where(low_half, head(2 * hp, v_pair), head(2 * hp + 1, v_pair)).astype(BF16)
    mix = _dot(o_s[...], wo_ref[...])
    o_ref[...] = x_ref[...] + mod_ref[0][2:3] * mix


def _mla_attn(x, mods, q, ckv, kp, wk_ext, wv_ext, wo_ext, cache=None):
    hw = MLA_HEADS * LANES
    if cache is None:
        n_lat, n_cache, nb, nq, base = SEQ, 0, BATCH, SEQ // TQ, 0
    else:
        n_lat, n_cache, nb, nq, base = DEC_SEQ, PAST_LEN, DEC_BATCH, DEC_SEQ // TQ, T_CTX // TQ
    seq_base = base * TQ // n_lat
    tile = lambda b, qi: (base + b * nq + qi, 0)
    seq = lambda b, qi: (seq_base + b, 0)
    in_specs = [pl.BlockSpec((TQ, hw), tile),
                pl.BlockSpec((n_lat, MLA_KV_RANK), seq, pipeline_mode=pl.Buffered(1)),
                pl.BlockSpec((n_lat, LANES), seq, pipeline_mode=pl.Buffered(1))]
    args = [q, ckv, kp]
    if cache is not None:
        in_specs += [pl.BlockSpec((1, n_cache, MLA_KV_RANK), lambda b, qi: (b, 0, 0)),
                     pl.BlockSpec((1, n_cache, LANES), lambda b, qi: (b, 0, 0))]
        args += list(cache)
    vw = MLA_HEADS * MLA_V
    in_specs += [_full((MLA_KV_RANK, hw)), _full((MLA_KV_RANK, vw)), _full((vw, D_MODEL)),
                 pl.BlockSpec((TQ, D_MODEL), tile),
                 pl.BlockSpec((1, 6, D_MODEL), lambda b, qi: (_cond_of_row((base + b * nq + qi) * TQ), 0, 0))]
    args += [wk_ext, wv_ext, wo_ext, x, mods]
    sk = n_cache + n_lat
    return pl.pallas_call(
        functools.partial(_mla_attn_kernel, n_lat=n_lat, n_cache=n_cache),
        grid=(nb, nq),
        in_specs=in_specs,
        out_specs=pl.BlockSpec((TQ, D_MODEL), tile),
        out_shape=jax.ShapeDtypeStruct((T_ALL, D_MODEL), F32),
        scratch_shapes=[pltpu.VMEM((sk, hw), BF16), pltpu.VMEM((sk, vw), BF16), pltpu.VMEM((TQ, vw), BF16)],
        input_output_aliases={len(args) - 2: 0},
        compiler_params=_params("arbitrary", "arbitrary"),
        name="mla_attn_lat" if cache is not None else "mla_attn_ctx",
    )(*args)


SWA_QW = SWA_HEADS * LANES
SWA_KW = SWA_KV_HEADS * LANES
SWA_CW = 2 * SWA_KV_HEADS * SWA_HEAD_DIM


def _swa_proj_kernel(*refs, rope):
    if rope:
        (x_ref, mod_ref, n1_ref, w_ref, b_ref, cq_ref, sq_ref, ck_ref, sk_ref, q_ref, k_ref, v_ref) = refs
    else:
        (x_ref, mod_ref, n1_ref, w_ref, b_ref, q_ref, k_ref, v_ref, kv_ref) = refs
    h = _modulated(x_ref[...], n1_ref[...], mod_ref[0], 0).astype(BF16)
    a = _dot(h, w_ref[...]) + b_ref[...]
    if rope:
        cq, sq, ck, sk = cq_ref[...], sq_ref[...], ck_ref[...], sk_ref[...]
        for hd in range(SWA_HEADS):
            sl = slice(hd * LANES, (hd + 1) * LANES)
            q_ref[:, sl] = _rope_slab(a[:, sl], cq, sq, SWA_HEAD_DIM // 2).astype(BF16)
        for hd in range(SWA_KV_HEADS):
            sl = slice(hd * LANES, (hd + 1) * LANES)
            k_ref[:, sl] = _rope_slab(a[:, SWA_QW + hd * LANES:SWA_QW + (hd + 1) * LANES], ck, sk,
                                      SWA_HEAD_DIM // 2).astype(BF16)
    else:
        q_ref[...] = (a[:, :SWA_QW] * SWA_SCALE).astype(BF16)
        k_ref[...] = a[:, SWA_QW:SWA_QW + SWA_KW].astype(BF16)
        kv_ref[...] = a[:, SWA_QW + 2 * SWA_KW:]
    v_ref[...] = a[:, SWA_QW + SWA_KW:SWA_QW + 2 * SWA_KW].astype(BF16)


def _swa_proj(x, mods, norm1, w_ext, b_ext, tabs, rope):
    base = T_CTX // TM if rope else 0
    n_tiles = (T_LAT if rope else T_CTX) // TM
    n_rows = n_tiles * TM
    ncols = w_ext.shape[1]
    row = lambda i: (i, 0)
    in_specs = [pl.BlockSpec((TM, D_MODEL), lambda i: (i + base, 0)), _mod_spec(TM, base), _full((1, D_MODEL)),
                _full((D_MODEL, ncols)), _full((1, ncols))]
    args = [x, mods, norm1.reshape(1, -1), w_ext, b_ext]
    out_specs = [pl.BlockSpec((TM, SWA_QW), row), pl.BlockSpec((TM, SWA_KW), row), pl.BlockSpec((TM, SWA_KW), row)]
    out_shape = [jax.ShapeDtypeStruct((n_rows, SWA_QW), BF16), jax.ShapeDtypeStruct((n_rows, SWA_KW), BF16),
                 jax.ShapeDtypeStruct((n_rows, SWA_KW), BF16)]
    if rope:
        tab = pl.BlockSpec((TM, LANES), lambda i: (_pos_block(i + base, TM), 0))
        in_specs += [tab] * 4
        args += list(tabs)
    else:
        out_specs.append(pl.BlockSpec((TM, SWA_CW), row))
        out_shape.append(jax.ShapeDtypeStruct((n_rows, SWA_CW), F32))
    return pl.pallas_call(
        functools.partial(_swa_proj_kernel, rope=rope),
        grid=(n_tiles,),
        in_specs=in_specs, out_specs=out_specs, out_shape=out_shape,
        compiler_params=_params("arbitrary"),
        name="swa_proj_lat" if rope else "swa_proj_ctx",
    )(*args)


def _swa_ctx_kernel(q_ref, k_ref, v_ref, sink_ref, wo_ref, bo_ref, x_ref, mod_ref, o_ref, o_s):
    for n in range(SWA_KV_HEADS):
        kn = k_ref[:, n * LANES:(n + 1) * LANES]
        vn = v_ref[:, n * LANES:(n + 1) * LANES]
        for g in range(SWA_GROUP):
            j = n * SWA_GROUP + g
            sl = slice(j * LANES, (j + 1) * LANES)
            s = _dot_nt(q_ref[:, sl], kn)
            sink = sink_ref[j]
            m = jnp.maximum(jnp.max(s, axis=-1, keepdims=True), sink)
            p = jnp.exp(s - m)
            l = jnp.sum(p, axis=-1, keepdims=True) + jnp.exp(sink - m)
            o_s[:, sl] = (_dot(p.astype(BF16), vn) / l).astype(BF16)
    mix = _dot(o_s[...], wo_ref[...]) + bo_ref[...]
    o_ref[...] = x_ref[...] + mod_ref[0][2:3] * mix


def _swa_lat_kernel(q_ref, k_ref, v_ref, ck_ref, cv_ref, sink_ref, wo_ref, bo_ref, x_ref, mod_ref, o_ref, o_s):
    span = TQ + 2 * SWA_WINDOW
    start = pl.program_id(1) * TQ
    ks = pl.multiple_of(jnp.clip(start - SWA_WINDOW, 0, DEC_SEQ - span), SWA_WINDOW)
    qpos = start + lax.broadcasted_iota(jnp.int32, (TQ, span), 0)
    kpos = ks + lax.broadcasted_iota(jnp.int32, (TQ, span), 1)
    valid = jnp.abs(kpos - qpos) <= SWA_WINDOW
    for n in range(SWA_KV_HEADS):
        nl = slice(n * LANES, (n + 1) * LANES)
        kc = ck_ref[0, :, nl]
        vc = cv_ref[0, :, nl]
        kl = k_ref[pl.ds(ks, span), nl]
        vl = v_ref[pl.ds(ks, span), nl]
        for g in range(SWA_GROUP):
            j = n * SWA_GROUP + g
            sl = slice(j * LANES, (j + 1) * LANES)
            qj = q_ref[:, sl]
            s_c = _dot_nt(qj, kc)
            s_l = jnp.where(valid, _dot_nt(qj, kl), -jnp.inf)
            sink = sink_ref[j]
            m = jnp.maximum(jnp.maximum(jnp.max(s_c, axis=-1, keepdims=True),
                                        jnp.max(s_l, axis=-1, keepdims=True)), sink)
            p_c = jnp.exp(s_c - m)
            p_l = jnp.exp(s_l - m)
            l = (jnp.sum(p_c, axis=-1, keepdims=True) + jnp.sum(p_l, axis=-1, keepdims=True)
                 + jnp.exp(sink - m))
            o = _dot(p_c.astype(BF16), vc) + _dot(p_l.astype(BF16), vl)
            o_s[:, sl] = (o / l).astype(BF16)
    mix = _dot(o_s[...], wo_ref[...]) + bo_ref[...]
    o_ref[...] = x_ref[...] + mod_ref[0][2:3] * mix


def _swa_attn(x, mods, q, k, v, sink, wo_ext, bo, cache=None):
    smem = pl.BlockSpec(memory_space=pltpu.SMEM)
    if cache is None:
        nq, base = 1, 0
        tile = lambda b, qi: (b, 0)
        in_specs = [pl.BlockSpec((TQ, SWA_QW), tile), pl.BlockSpec((SEQ, SWA_KW), tile),
                    pl.BlockSpec((SEQ, SWA_KW), tile)]
        args = [q, k, v]
        kern, nb, name = _swa_ctx_kernel, BATCH, "swa_attn_ctx"
    else:
        nq, base = DEC_SEQ // TQ, T_CTX // TQ
        tile = lambda b, qi: (b * nq + qi, 0)
        seq = lambda b, qi: (b, 0)
        cspec = pl.BlockSpec((1, PAST_LEN, SWA_KW), lambda b, qi: (b, 0, 0))
        in_specs = [pl.BlockSpec((TQ, SWA_QW), tile), pl.BlockSpec((DEC_SEQ, SWA_KW), seq),
                    pl.BlockSpec((DEC_SEQ, SWA_KW), seq), cspec, cspec]
        args = [q, k, v, cache[0], cache[1]]
        kern, nb, name = _swa_lat_kernel, DEC_BATCH, "swa_attn_lat"
    xtile = lambda b, qi: (base + b * nq + qi, 0)
    in_specs += [smem, _full((SWA_QW, D_MODEL)), _full((1, D_MODEL)), pl.BlockSpec((TQ, D_MODEL), xtile),
                 pl.BlockSpec((1, 6, D_MODEL), lambda b, qi: (_cond_of_row((base + b * nq + qi) * TQ), 0, 0))]
    args += [sink, wo_ext, bo.reshape(1, -1), x, mods]
    return pl.pallas_call(
        kern,
        grid=(nb, nq),
        in_specs=in_specs,
        out_specs=pl.BlockSpec((TQ, D_MODEL), xtile),
        out_shape=jax.ShapeDtypeStruct((T_ALL, D_MODEL), F32),
        scratch_shapes=[pltpu.VMEM((TQ, SWA_QW), BF16)],
        input_output_aliases={len(args) - 2: 0},
        compiler_params=_params("arbitrary", "arbitrary"),
        name=name,
    )(*args)


CONV_EXT = CONV_TM + 2 * CONV_HALO


def _conv_kernel(xp_ref, x_ref, xn_ref, mod_ref, n1_ref, w1_ref, b1_ref, dw_ref, dwb_ref, lng_ref, lnb_ref,
                 w2_ref, b2_ref, o_ref, ext_s, y_s):
    i = pl.program_id(0)
    tiles_per_seq = DEC_SEQ // CONV_TM
    j = (i - T_CTX // CONV_TM) % tiles_per_seq
    latent = i >= T_CTX // CONV_TM
    left_ok = jnp.logical_and(latent, j > 0)
    right_ok = jnp.logical_and(latent, j < tiles_per_seq - 1)
    mod = mod_ref[0]

    def glu_of(xv):
        h = _modulated(xv, n1_ref[...], mod, 0).astype(BF16)
        u = _dot(h, w1_ref[...]) + b1_ref[...]
        return u[:, :D_MODEL] * _sigmoid(u[:, D_MODEL:])

    ext_s[CONV_HALO:CONV_HALO + CONV_TM, :] = glu_of(x_ref[...])
    ext_s[0:CONV_HALO, :] = jnp.where(left_ok, glu_of(xp_ref[...]), 0.0)
    ext_s[CONV_HALO + CONV_TM:, :] = jnp.where(right_ok, glu_of(xn_ref[...]), 0.0)

    rows = 128
    first = CONV_HALO - CONV_PAD
    for c in range(D_MODEL // LANES):
        cl = slice(c * LANES, (c + 1) * LANES)
        for r in range(CONV_TM // rows):
            acc = jnp.zeros((rows, LANES), F32)
            for w in range(CONV_WIDTH):
                r0 = r * rows + first + w
                acc = acc + ext_s[r0:r0 + rows, cl] * dw_ref[w:w + 1, cl]
            y_s[r * rows:(r + 1) * rows, cl] = acc
    y = y_s[...] + dwb_ref[...]
    mu = jnp.mean(y, axis=-1, keepdims=True)
    yc = y - mu
    var = jnp.mean(yc * yc, axis=-1, keepdims=True)
    yn = yc * lax.rsqrt(var + NORM_EPS) * lng_ref[...] + lnb_ref[...]
    z = (yn * _sigmoid(yn)).astype(BF16)
    mix = _dot(z, w2_ref[...]) + b2_ref[...]
    o_ref[...] = x_ref[...] + mod[2:3] * mix


def _conv_layer(x, mods, norm1, w1, b1, dw, dwb, ln_g, ln_b, w2, b2):
    n_tiles = T_ALL // CONV_TM
    per = CONV_TM // CONV_HALO
    n_halo_blocks = T_ALL // CONV_HALO
    row = lambda i: (i, 0)
    vec = lambda v: v.reshape(1, -1)
    return pl.pallas_call(
        _conv_kernel,
        grid=(n_tiles,),
        in_specs=[pl.BlockSpec((CONV_HALO, D_MODEL), lambda i: (jnp.maximum(i * per - 1, 0), 0)),
                  pl.BlockSpec((CONV_TM, D_MODEL), row),
                  pl.BlockSpec((CONV_HALO, D_MODEL), lambda i: (jnp.minimum((i + 1) * per, n_halo_blocks - 1), 0)),
                  _mod_spec(CONV_TM), _full((1, D_MODEL)),
                  _full((D_MODEL, 2 * D_MODEL)), _full((1, 2 * D_MODEL)),
                  _full((CONV_WIDTH, D_MODEL)), _full((1, D_MODEL)), _full((1, D_MODEL)), _full((1, D_MODEL)),
                  _full((D_MODEL, D_MODEL)), _full((1, D_MODEL))],
        out_specs=pl.BlockSpec((CONV_TM, D_MODEL), row),
        out_shape=jax.ShapeDtypeStruct((T_ALL, D_MODEL), F32),
        scratch_shapes=[pltpu.VMEM((CONV_EXT, D_MODEL), F32), pltpu.VMEM((CONV_TM, D_MODEL), F32)],
        compiler_params=_params("arbitrary"),
        name="conv_module",
    )(x, x, x, mods, vec(norm1), w1.astype(BF16), vec(b1), dw, vec(dwb), vec(ln_g), vec(ln_b),
      w2.astype(BF16), vec(b2))


def _router_kernel(x_ref, mod_ref, n2_ref, rw_ref, rb_ref, h_ref, idx_ref, gate_ref, cnt_ref):
    h = _modulated(x_ref[...], n2_ref[...], mod_ref[0], 3)
    h_ref[...] = h
    cur = _dot_nt(rw_ref[...], h.astype(BF16)) + rb_ref[...]
    e_iota = lax.broadcasted_iota(jnp.int32, cur.shape, 0)
    vals, idxs = [], []
    for _ in range(TOP_K):
        m = jnp.max(cur, axis=0, keepdims=True)
        idx = jnp.min(jnp.where(cur == m, e_iota, N_EXPERTS), axis=0, keepdims=True)
        vals.append(m)
        idxs.append(idx)
        cur = jnp.where(e_iota == idx, -jnp.inf, cur)
    ex = [jnp.exp(v - vals[0]) for v in vals]
    tot = ex[0] + ex[1] + ex[2] + ex[3]
    for k in range(TOP_K):
        idx_ref[k:k + 1, :] = idxs[k]
        gate_ref[k:k + 1, :] = ex[k] / tot

    picked = jnp.sum(jnp.where(cur == -jnp.inf, 1.0, 0.0), axis=1, keepdims=True)

    @pl.when(pl.program_id(0) == 0)
    def _():
        cnt_ref[...] = jnp.zeros_like(cnt_ref)

    cnt_ref[...] += picked


def _router(x, mods, norm2, router_w, router_b):
    n_tiles = T_ALL // TM
    row = lambda i: (i, 0)
    col = lambda i: (0, i)
    return pl.pallas_call(
        _router_kernel,
        grid=(n_tiles,),
        in_specs=[pl.BlockSpec((TM, D_MODEL), row), _mod_spec(TM), _full((1, D_MODEL)),
                  _full((N_EXPERTS, D_MODEL)), _full((N_EXPERTS, 1))],
        out_specs=[pl.BlockSpec((TM, D_MODEL), row), pl.BlockSpec((TOP_K, TM), col),
                   pl.BlockSpec((TOP_K, TM), col), pl.BlockSpec((N_EXPERTS, 1), lambda i: (0, 0))],
        out_shape=[jax.ShapeDtypeStruct((T_ALL, D_MODEL), F32),
                   jax.ShapeDtypeStruct((TOP_K, T_ALL), jnp.int32),
                   jax.ShapeDtypeStruct((TOP_K, T_ALL), F32),
                   jax.ShapeDtypeStruct((N_EXPERTS, 1), F32)],
        compiler_params=_params("arbitrary"),
        name="router",
    )(x, mods, norm2.reshape(1, -1), router_w.T.astype(BF16), router_b.reshape(-1, 1))


N_ASSIGN = TOP_K * T_ALL
RANK_CHUNK = 256
RANK_BLOCK = 2048


def _block_layout(counts):
    counts = counts.reshape(N_EXPERTS).astype(jnp.int32)
    padded = (counts + MOE_BM - 1) // MOE_BM * MOE_BM
    pad_end = jnp.cumsum(padded)
    pad_start = pad_end - padded
    starts = jnp.arange(MOE_BLOCKS, dtype=jnp.int32) * MOE_BM
    block_e = jnp.minimum(jnp.sum((pad_end[None, :] <= starts[:, None]).astype(jnp.int32), axis=1), N_EXPERTS - 1)
    n_used = (pad_end[-1] // MOE_BM).reshape(1)
    return pad_start.astype(F32).reshape(N_EXPERTS, 1), block_e, n_used


def _rank_kernel(idx_ref, ps_ref, u_ref, dest_ref, carry):
    @pl.when(pl.program_id(0) == 0)
    def _():
        carry[...] = ps_ref[...]

    e_iota = lax.broadcasted_iota(jnp.int32, (N_EXPERTS, RANK_CHUNK), 0)
    for c in range(RANK_BLOCK // RANK_CHUNK):
        sl = slice(c * RANK_CHUNK, (c + 1) * RANK_CHUNK)
        hit = e_iota == idx_ref[:, sl]
        pref = _dot(jnp.where(hit, 1.0, 0.0).astype(BF16), u_ref[...])
        base = carry[...]
        dest = jnp.sum(jnp.where(hit, pref + base, 0.0), axis=0, keepdims=True) - 1.0
        dest_ref[:, sl] = dest.astype(jnp.int32)
        carry[...] = base + pref[:, RANK_CHUNK - 1:RANK_CHUNK]


def _assignment_slots(idx_t, pad_start):
    tri = (jnp.arange(RANK_CHUNK)[:, None] <= jnp.arange(RANK_CHUNK)[None, :]).astype(BF16)
    blk = pl.BlockSpec((1, RANK_BLOCK), lambda i: (0, i))
    return pl.pallas_call(
        _rank_kernel,
        grid=(N_ASSIGN // RANK_BLOCK,),
        in_specs=[blk, _full((N_EXPERTS, 1)), _full((RANK_CHUNK, RANK_CHUNK))],
        out_specs=blk,
        out_shape=jax.ShapeDtypeStruct((1, N_ASSIGN), jnp.int32),
        scratch_shapes=[pltpu.VMEM((N_EXPERTS, 1), F32)],
        compiler_params=_params("arbitrary"),
        name="moe_rank",
    )(idx_t.reshape(1, N_ASSIGN), pad_start, tri)


INV_UNROLL = 8
INV_CHUNK = 4096
INV_INIT_STEPS = MOE_CAP // INV_CHUNK
INV_PLACE_STEPS = N_ASSIGN // INV_CHUNK


def _inverse_kernel(dest_ref, slot_ref):
    i = pl.program_id(0)

    @pl.when(i < INV_INIT_STEPS)
    def _():
        base = i * INV_CHUNK

        def init(j, carry):
            for u in range(INV_UNROLL):
                s = base + j * INV_UNROLL + u
                slot_ref[s] = N_ASSIGN + (s & (MOE_BM - 1))
            return carry

        lax.fori_loop(0, INV_CHUNK // INV_UNROLL, init, 0)

    @pl.when(i >= INV_INIT_STEPS)
    def _():
        base = (i - INV_INIT_STEPS) * INV_CHUNK

        def place(j, carry):
            for u in range(INV_UNROLL):
                o = j * INV_UNROLL + u
                slot_ref[dest_ref[0, 0, o]] = base + o
            return carry

        lax.fori_loop(0, INV_CHUNK // INV_UNROLL, place, 0)


def _slot_assignments(dest):
    return pl.pallas_call(
        _inverse_kernel,
        grid=(INV_INIT_STEPS + INV_PLACE_STEPS,),
        in_specs=[pl.BlockSpec((1, 1, INV_CHUNK), lambda i: (jnp.maximum(i - INV_INIT_STEPS, 0), 0, 0),
                               memory_space=pltpu.SMEM)],
        out_specs=pl.BlockSpec(memory_space=pltpu.SMEM),
        out_shape=jax.ShapeDtypeStruct((MOE_CAP,), jnp.int32),
        compiler_params=_params("arbitrary"),
        name="moe_slots",
    )(dest.reshape(INV_PLACE_STEPS, 1, INV_CHUNK))


YG_ROWS = N_ASSIGN + MOE_BM


def _moe_kernel(be_ref, nu_ref, sa_ref, sa_next_ref, h_hbm, wg_ref, wu_ref, wd_ref, bg_ref, bu_ref, bd_ref,
                yg_hbm, xbuf, ybuf, wg_s, wu_s, wd_s, gsem, ssem):
    i = pl.program_id(0)
    n_used = nu_ref[0]
    cur = i % 2
    nxt = 1 - cur

    def start_gather(sa, buf):
        for r in range(MOE_BM):
            tok = sa[0, 0, r] & (T_ALL - 1)
            pltpu.make_async_copy(h_hbm.at[pl.ds(tok, 1)], xbuf.at[buf, pl.ds(r, 1)], gsem.at[buf]).start()

    def wait_gather(buf):
        pltpu.make_async_copy(h_hbm.at[pl.ds(0, MOE_BM)], xbuf.at[buf], gsem.at[buf]).wait()

    def wait_scatter():
        pltpu.make_async_copy(ybuf, yg_hbm.at[pl.ds(0, MOE_BM)], ssem.at[0]).wait()

    @pl.when(i == 0)
    def _():
        start_gather(sa_ref, 0)
        ybuf[...] = jnp.zeros_like(ybuf)
        dump = pltpu.make_async_copy(ybuf, yg_hbm.at[pl.ds(N_ASSIGN, MOE_BM)], ssem.at[0])
        dump.start()
        dump.wait()

    @pl.when(i < n_used)
    def _():
        start_gather(sa_next_ref, nxt)

        prev = be_ref[jnp.maximum(i - 1, 0)]

        @pl.when(jnp.logical_or(i == 0, be_ref[i] != prev))
        def _():
            wg_s[...] = wg_ref[0].astype(BF16)
            wu_s[...] = wu_ref[0].astype(BF16)
            wd_s[...] = wd_ref[0].astype(BF16)

        wait_gather(cur)
        x = xbuf[cur].astype(BF16)
        g = _dot(x, wg_s[...]) + bg_ref[0]
        u = _dot(x, wu_s[...]) + bu_ref[0]
        g = jnp.minimum(g, SWIGLU_LIMIT)
        u = jnp.clip(u, -SWIGLU_LIMIT, SWIGLU_LIMIT)
        act = (u + 1.0) * (g * _sigmoid(SWIGLU_ALPHA * g))
        y = _dot(act.astype(BF16), wd_s[...]) + bd_ref[0]

        @pl.when(i > 0)
        def _():
            wait_scatter()

        ybuf[...] = y
        for r in range(MOE_BM):
            pltpu.make_async_copy(ybuf.at[pl.ds(r, 1)], yg_hbm.at[pl.ds(sa_ref[0, 0, r], 1)], ssem.at[0]).start()

    last = jnp.minimum(n_used, MOE_BLOCKS) - 1

    @pl.when(i == last)
    def _():
        wait_gather(nxt)
        wait_scatter()


def _moe_experts(h, slots, block_e, n_used, wg, bg, wu, bu, wd, bd):
    wspec = pl.BlockSpec((1, D_MODEL, D_MODEL), lambda i, be, nu: (be[i], 0, 0))
    bspec = pl.BlockSpec((1, 1, D_MODEL), lambda i, be, nu: (be[i], 0, 0))
    sspec = lambda f: pl.BlockSpec((1, 1, MOE_BM), lambda i, be, nu: (f(i), 0, 0), memory_space=pltpu.SMEM)
    hbm = pl.BlockSpec(memory_space=pl.ANY)
    b3 = lambda b: b.reshape(N_EXPERTS, 1, D_MODEL)
    slots3 = slots.reshape(MOE_BLOCKS, 1, MOE_BM)
    return pl.pallas_call(
        _moe_kernel,
        grid_spec=pltpu.PrefetchScalarGridSpec(
            num_scalar_prefetch=2,
            grid=(MOE_BLOCKS,),
            in_specs=[sspec(lambda i: i), sspec(lambda i: jnp.minimum(i + 1, MOE_BLOCKS - 1)), hbm,
                      wspec, wspec, wspec, bspec, bspec, bspec],
            out_specs=hbm,
            scratch_shapes=[pltpu.VMEM((2, MOE_BM, D_MODEL), F32), pltpu.VMEM((MOE_BM, D_MODEL), F32),
                            pltpu.VMEM((D_MODEL, D_MODEL), BF16), pltpu.VMEM((D_MODEL, D_MODEL), BF16),
                            pltpu.VMEM((D_MODEL, D_MODEL), BF16),
                            pltpu.SemaphoreType.DMA((2,)), pltpu.SemaphoreType.DMA((1,))]),
        out_shape=jax.ShapeDtypeStruct((YG_ROWS, D_MODEL), F32),
        compiler_params=_params("arbitrary"),
        name="moe_experts",
    )(block_e, n_used, slots3, slots3, h, wg, wu, wd, b3(bg), b3(bu), b3(bd))


def _combine_kernel(x_ref, y0_ref, y1_ref, y2_ref, y3_ref, gate_ref, mod_ref, o_ref):
    gate = gate_ref[...]
    acc = y0_ref[...] * gate[:, 0:1]
    for k, y_ref in enumerate((y1_ref, y2_ref, y3_ref), start=1):
        acc = acc + y_ref[...] * gate[:, k:k + 1]
    o_ref[...] = x_ref[...] + mod_ref[0][5:6] * acc


def _combine(x, yg, gates, mods):
    row = lambda i: (i, 0)
    n_tiles = T_ALL // TM
    ysp = lambda k: pl.BlockSpec((TM, D_MODEL), lambda i: (k * n_tiles + i, 0))
    return pl.pallas_call(
        _combine_kernel,
        grid=(n_tiles,),
        in_specs=[pl.BlockSpec((TM, D_MODEL), row), ysp(0), ysp(1), ysp(2), ysp(3),
                  pl.BlockSpec((TM, TOP_K), row), _mod_spec(TM)],
        out_specs=pl.BlockSpec((TM, D_MODEL), row),
        out_shape=jax.ShapeDtypeStruct((T_ALL, D_MODEL), F32),
        input_output_aliases={0: 0},
        compiler_params=_params("arbitrary"),
        name="moe_combine",
    )(x, yg, yg, yg, yg, gates, mods)


def _moe_layer(x, mods, norm2, router_w, router_b, wg, bg, wu, bu, wd, bd):
    h, idx_t, gate_t, counts = _router(x, mods, norm2, router_w, router_b)
    pad_start, block_e, n_used = _block_layout(counts)
    slots = _slot_assignments(_assignment_slots(idx_t, pad_start))
    yg = _moe_experts(h, slots, block_e, n_used, wg, bg, wu, bu, wd, bd)
    return _combine(x, yg, gate_t.T, mods)


def _final_norm_kernel(x_ref, g_ref, o_ref):
    o_ref[...] = _rms(x_ref[...], g_ref[...])


def _final_norm(x, g, base_rows, n_rows):
    base = base_rows // TM
    return pl.pallas_call(
        _final_norm_kernel,
        grid=(n_rows // TM,),
        in_specs=[pl.BlockSpec((TM, D_MODEL), lambda i: (i + base, 0)), _full((1, D_MODEL))],
        out_specs=pl.BlockSpec((TM, D_MODEL), lambda i: (i, 0)),
        out_shape=jax.ShapeDtypeStruct((n_rows, D_MODEL), F32),
        compiler_params=_params("arbitrary"),
        name="final_norm",
    )(x, g.reshape(1, -1))


def _pad_lanes(w, lead_pad, width):
    tail = width - lead_pad - w.shape[-1]
    cfg = [(0, 0)] * (w.ndim - 1) + [(lead_pad, tail)]
    return jnp.pad(w, cfg)


def _mla_weights(wa, wuq, wukv, wo):
    krope_w = wa[:, MLA_Q_RANK + MLA_KV_RANK:]
    wa_ext = jnp.concatenate([wa[:, :MLA_Q_RANK + MLA_KV_RANK],
                              _pad_lanes(krope_w, MLA_ROPE_OFF, LANES),
                              _pad_lanes(krope_w, 0, LANES)], axis=1).astype(BF16)
    q3 = wuq.reshape(MLA_Q_RANK, MLA_HEADS, MLA_NOPE + MLA_ROPE)
    wuq_ext = jnp.concatenate([q3[..., :MLA_NOPE], _pad_lanes(q3[..., MLA_NOPE:], MLA_ROPE // 2, LANES - MLA_NOPE)],
                              axis=-1).reshape(MLA_Q_RANK, MLA_HEADS * LANES).astype(BF16)
    kv3 = wukv.reshape(MLA_KV_RANK, MLA_HEADS, MLA_NOPE + MLA_V)
    wk_ext = _pad_lanes(kv3[..., :MLA_NOPE], 0, LANES).reshape(MLA_KV_RANK, MLA_HEADS * LANES).astype(BF16)
    wv = kv3[..., MLA_NOPE:].reshape(MLA_KV_RANK, MLA_HEADS * MLA_V).astype(BF16)
    return wa_ext, wuq_ext, wk_ext, wv, wo.astype(BF16)


def _swa_weights(wqkv, bqkv, wo):
    nq = SWA_HEADS * SWA_HEAD_DIM
    nkv = SWA_KV_HEADS * SWA_HEAD_DIM

    def slabs(w, heads):
        lead = w.shape[:-1]
        return _pad_lanes(w.reshape(lead + (heads, SWA_HEAD_DIM)), 0, LANES).reshape(lead + (heads * LANES,))

    def ext(w, with_compact):
        parts = [slabs(w[..., :nq], SWA_HEADS), slabs(w[..., nq:nq + nkv], SWA_KV_HEADS),
                 slabs(w[..., nq + nkv:], SWA_KV_HEADS)]
        if with_compact:
            parts.append(w[..., nq:])
        return jnp.concatenate(parts, axis=-1)

    b2 = bqkv.reshape(1, -1)
    wo3 = wo.reshape(SWA_HEADS, SWA_HEAD_DIM, D_MODEL)
    wo_ext = jnp.pad(wo3, ((0, 0), (0, LANES - SWA_HEAD_DIM), (0, 0))).reshape(SWA_QW, D_MODEL).astype(BF16)
    return (ext(wqkv, True).astype(BF16), ext(b2, True), ext(wqkv, False).astype(BF16), ext(b2, False), wo_ext)


def kernel(x_prompt, x_sample, cache_l0_ckv, cache_l0_krope, cache_l1_k, cache_l1_v, cache_l3_ckv, cache_l3_krope, c, c_ctx, l0_mla_wa, l0_mla_q_norm, l0_mla_wuq, l0_mla_kv_norm, l0_mla_wukv, l0_mla_wo, l0_mod_w, l0_mod_b, l0_norm1, l0_norm2, l0_router_w, l0_router_b, l0_moe_wg, l0_moe_bg, l0_moe_wu, l0_moe_bu, l0_moe_wd, l0_moe_bd, l1_swa_wqkv, l1_swa_bqkv, l1_swa_sink, l1_swa_wo, l1_swa_bo, l1_mod_w, l1_mod_b, l1_norm1, l1_norm2, l1_router_w, l1_router_b, l1_moe_wg, l1_moe_bg, l1_moe_wu, l1_moe_bu, l1_moe_wd, l1_moe_bd, l2_conv_w1, l2_conv_b1, l2_conv_dw, l2_conv_dwb, l2_conv_ln_g, l2_conv_ln_b, l2_conv_w2, l2_conv_b2, l2_mod_w, l2_mod_b, l2_norm1, l2_norm2, l2_router_w, l2_router_b, l2_moe_wg, l2_moe_bg, l2_moe_wu, l2_moe_bu, l2_moe_wd, l2_moe_bd, l3_mla_wa, l3_mla_q_norm, l3_mla_wuq, l3_mla_kv_norm, l3_mla_wukv, l3_mla_wo, l3_mod_w, l3_mod_b, l3_norm1, l3_norm2, l3_router_w, l3_router_b, l3_moe_wg, l3_moe_bg, l3_moe_wu, l3_moe_bu, l3_moe_wd, l3_moe_bd, final_norm):
    x = jnp.concatenate([x_prompt.reshape(T_CTX, D_MODEL), x_sample.reshape(T_LAT, D_MODEL)], axis=0)
    cond = jnp.concatenate([c, c_ctx[None, :], jnp.zeros((N_COND - DEC_BATCH - 1, D_MODEL), F32)], axis=0)

    mla_c, mla_s = _rope_tables(MLA_ROPE, MLA_ROPE_LEAD, TM)
    mla_tabs = (mla_c * MLA_SCALE, mla_s * MLA_SCALE, mla_c, mla_s)
    swa_c, swa_s = _rope_tables(SWA_HEAD_DIM, 0, TM)
    swa_tabs = (swa_c * SWA_SCALE, swa_s * SWA_SCALE, swa_c, swa_s)

    def mla_layer(x, mods, norm1, wa, q_norm, wuq, kv_norm, wukv, wo, cache_ckv, cache_krope):
        wa_ext, wuq_ext, wk_ext, wv_ext, wo_ext = _mla_weights(wa, wuq, wukv, wo)
        q, ckv, kp, kst = _mla_proj(x, mods, norm1, wa_ext, q_norm, wuq_ext, kv_norm, mla_tabs)
        x = _mla_attn(x, mods, q, ckv, kp, wk_ext, wv_ext, wo_ext)
        cache = (cache_ckv, _pad_lanes(cache_krope, MLA_ROPE_OFF, LANES).astype(BF16))
        x = _mla_attn(x, mods, q, ckv, kp, wk_ext, wv_ext, wo_ext, cache)
        return x, (ckv[:T_CTX].reshape(BATCH, SEQ, MLA_KV_RANK), kst[:T_CTX].reshape(BATCH, SEQ, MLA_ROPE))

    def moe(x, mods, norm2, rw, rb, wg, bg, wu, bu, wd, bd):
        return _moe_layer(x, mods, norm2, rw, rb, wg, bg, wu, bu, wd, bd)

    mods = _modulation(cond, l0_mod_w, l0_mod_b)
    x, (st0_ckv, st0_krope) = mla_layer(x, mods, l0_norm1, l0_mla_wa, l0_mla_q_norm, l0_mla_wuq, l0_mla_kv_norm,
                                        l0_mla_wukv, l0_mla_wo, cache_l0_ckv, cache_l0_krope)
    x = moe(x, mods, l0_norm2, l0_router_w, l0_router_b, l0_moe_wg, l0_moe_bg, l0_moe_wu, l0_moe_bu,
            l0_moe_wd, l0_moe_bd)

    mods = _modulation(cond, l1_mod_w, l1_mod_b)
    w_ctx, b_ctx, w_lat, b_lat, swa_wo_ext = _swa_weights(l1_swa_wqkv, l1_swa_bqkv, l1_swa_wo)
    q_c, k_c, v_c, kv_c = _swa_proj(x, mods, l1_norm1, w_ctx, b_ctx, None, rope=False)
    q_l, k_l, v_l = _swa_proj(x, mods, l1_norm1, w_lat, b_lat, swa_tabs, rope=True)
    x = _swa_attn(x, mods, q_c, k_c, v_c, l1_swa_sink, swa_wo_ext, l1_swa_bo)
    nkv = SWA_KV_HEADS * SWA_HEAD_DIM
    cache_k = _pad_lanes(cache_l1_k, 0, LANES).reshape(DEC_BATCH, PAST_LEN, SWA_KW).astype(BF16)
    cache_v = _pad_lanes(cache_l1_v, 0, LANES).reshape(DEC_BATCH, PAST_LEN, SWA_KW).astype(BF16)
    x = _swa_attn(x, mods, q_l, k_l, v_l, l1_swa_sink, swa_wo_ext, l1_swa_bo, (cache_k, cache_v))
    st1_k = kv_c[:, :nkv].reshape(BATCH, SEQ, SWA_KV_HEADS, SWA_HEAD_DIM)
    st1_v = kv_c[:, nkv:].reshape(BATCH, SEQ, SWA_KV_HEADS, SWA_HEAD_DIM)
    x = moe(x, mods, l1_norm2, l1_router_w, l1_router_b, l1_moe_wg, l1_moe_bg, l1_moe_wu, l1_moe_bu,
            l1_moe_wd, l1_moe_bd)

    mods = _modulation(cond, l2_mod_w, l2_mod_b)
    x = _conv_layer(x, mods, l2_norm1, l2_conv_w1, l2_conv_b1, l2_conv_dw, l2_conv_dwb, l2_conv_ln_g,
                    l2_conv_ln_b, l2_conv_w2, l2_conv_b2)
    x = moe(x, mods, l2_norm2, l2_router_w, l2_router_b, l2_moe_wg, l2_moe_bg, l2_moe_wu, l2_moe_bu,
            l2_moe_wd, l2_moe_bd)

    mods = _modulation(cond, l3_mod_w, l3_mod_b)
    x, (st3_ckv, st3_krope) = mla_layer(x, mods, l3_norm1, l3_mla_wa, l3_mla_q_norm, l3_mla_wuq, l3_mla_kv_norm,
                                        l3_mla_wukv, l3_mla_wo, cache_l3_ckv, cache_l3_krope)
    x = moe(x, mods, l3_norm2, l3_router_w, l3_router_b, l3_moe_wg, l3_moe_bg, l3_moe_wu, l3_moe_bu,
            l3_moe_wd, l3_moe_bd)

    y_prompt = _final_norm(x, final_norm, 0, T_CTX).reshape(BATCH, SEQ, D_MODEL)
    y_sample = _final_norm(x, final_norm, T_CTX, T_LAT).reshape(DEC_BATCH, DEC_SEQ, D_MODEL)
    return (y_prompt, y_sample, st0_ckv, st0_krope, st1_k, st1_v, st3_ckv, st3_krope)
```

```python
import functools

import jax
import jax.numpy as jnp
from jax import lax
from jax.experimental import pallas as pl
from jax.experimental.pallas import tpu as pltpu

F32 = jnp.float32
BF16 = jnp.bfloat16

D_MODEL = 1024
BATCH = 32
SEQ = 256
DEPTH = 4
DEC_BATCH = 4
DEC_SEQ = 2048
PAST_LEN = 256
GRID_W = 64
NORM_EPS = 1e-6
ROPE_THETA = 10000.0

MLA_HEADS = 16
MLA_NOPE = 64
MLA_ROPE = 32
MLA_V = 64
MLA_Q_RANK = 384
MLA_KV_RANK = 256
MLA_SCALE = (MLA_NOPE + MLA_ROPE) ** -0.5

SWA_HEADS = 16
SWA_KV_HEADS = 4
SWA_GROUP = 4
SWA_HEAD_DIM = 64
SWA_WINDOW = 128
SWA_SCALE = SWA_HEAD_DIM ** -0.5

CONV_WIDTH = 31
CONV_PAD = CONV_WIDTH // 2

N_EXPERTS = 32
TOP_K = 4
SWIGLU_ALPHA = 1.702
SWIGLU_LIMIT = 7.0

LANES = 128
T_CTX = BATCH * SEQ
T_LAT = DEC_BATCH * DEC_SEQ
T_ALL = T_CTX + T_LAT
CTX_COND = DEC_BATCH
N_COND = 8

TM = 512
TQ = 256
CONV_TM = 256
CONV_HALO = 16
MOE_BM = 256
MOE_BLOCKS = (T_ALL * TOP_K + N_EXPERTS * (MOE_BM - 1) + MOE_BM - 1) // MOE_BM
MOE_CAP = MOE_BLOCKS * MOE_BM
VMEM_LIMIT = 56 * 1024 * 1024


def _params(*sem):
    return pltpu.CompilerParams(dimension_semantics=sem, vmem_limit_bytes=VMEM_LIMIT)


def _cond_of_row(row0):
    return jnp.where(row0 < T_CTX, CTX_COND, (row0 - T_CTX) // DEC_SEQ)


def _pos_block(i, tm):
    row0 = i * tm
    return jnp.where(row0 < T_CTX, DEC_SEQ // tm, ((row0 - T_CTX) % DEC_SEQ) // tm)


def _rms(x, g):
    ms = jnp.mean(x * x, axis=-1, keepdims=True)
    return x * lax.rsqrt(ms + NORM_EPS) * g


def _modulated(x, g, mod, j):
    return _rms(x, g) * (1.0 + mod[j + 1:j + 2]) + mod[j:j + 1]


def _sigmoid(x):
    return 1.0 / (1.0 + jnp.exp(-x))


def _dot(a, b):
    return jnp.dot(a, b, preferred_element_type=F32)


def _dot_nt(a, b):
    return lax.dot_general(a, b, (((1,), (1,)), ((), ())), preferred_element_type=F32)


TOK_ROWS = D_MODEL // LANES


def _store_token_tiles(ref, v):
    n = v.shape[0]
    for j in range(TOK_ROWS):
        ref[pl.ds(j, n, stride=TOK_ROWS), :] = v[:, j * LANES:(j + 1) * LANES]


def _load_token_tiles(ref, n):
    return jnp.concatenate([ref[pl.ds(j, n, stride=TOK_ROWS), :] for j in range(TOK_ROWS)], axis=-1)


def _rope_slab(a, c, s, shift):
    return a * c + (pltpu.roll(a, shift, 1) - pltpu.roll(a, LANES - shift, 1)) * s


def _mod_kernel(c_ref, w_ref, b_ref, o_ref):
    c = c_ref[...]
    s = (c * _sigmoid(c)).astype(BF16)
    o_ref[...] = _dot(s, w_ref[...].astype(BF16)) + b_ref[...]


def _modulation(cond, w, b):
    tn = 1536
    out = pl.pallas_call(
        _mod_kernel,
        grid=(6 * D_MODEL // tn,),
        in_specs=[pl.BlockSpec((N_COND, D_MODEL), lambda j: (0, 0)),
                  pl.BlockSpec((D_MODEL, tn), lambda j: (0, j)),
                  pl.BlockSpec((1, tn), lambda j: (0, j))],
        out_specs=pl.BlockSpec((N_COND, tn), lambda j: (0, j)),
        out_shape=jax.ShapeDtypeStruct((N_COND, 6 * D_MODEL), F32),
        compiler_params=_params("arbitrary"),
        name="modulation",
    )(cond, w, b.reshape(1, -1))
    return out.reshape(N_COND, 6, D_MODEL)


def _mod_spec(tm, base=0):
    return pl.BlockSpec((1, 6, D_MODEL), lambda i, *_: (_cond_of_row((i + base) * tm), 0, 0))


def _full(shape):
    nd = len(shape)
    return pl.BlockSpec(shape, lambda *_: (0,) * nd, pipeline_mode=pl.Buffered(1))


def _rope_cos_sin(n_tok, rot_dim):
    rows = n_tok // GRID_W
    row = jnp.repeat(jnp.arange(rows, dtype=F32), GRID_W)
    col = jnp.tile(jnp.arange(GRID_W, dtype=F32), rows)
    n_freq = rot_dim // 4
    inv_freq = ROPE_THETA ** (-jnp.arange(n_freq, dtype=F32) / n_freq)
    ang = jnp.concatenate([row[:, None] * inv_freq, col[:, None] * inv_freq], axis=-1)
    return jnp.cos(ang), jnp.sin(ang)


def _rope_tables(rot_dim, lead, tm):
    cos, sin = _rope_cos_sin(DEC_SEQ, rot_dim)
    gap = rot_dim // 2 if lead else 0
    tail = LANES - lead - gap - rot_dim
    n = DEC_SEQ

    def slab(lead_val, rot):
        return jnp.concatenate([jnp.full((rot.shape[0], lead), lead_val, F32),
                                jnp.zeros((rot.shape[0], gap), F32), rot,
                                jnp.zeros((rot.shape[0], tail), F32)], axis=-1)

    c = slab(1.0, jnp.concatenate([cos, cos], axis=-1))
    s = slab(0.0, jnp.concatenate([sin, sin], axis=-1))
    c_id = slab(1.0, jnp.ones((tm, rot_dim), F32))
    s_id = jnp.zeros((tm, LANES), F32)
    del n
    return jnp.concatenate([c, c_id], axis=0), jnp.concatenate([s, s_id], axis=0)


MLA_A_COLS = MLA_Q_RANK + MLA_KV_RANK + 2 * LANES
MLA_ROPE_LEAD = MLA_NOPE
MLA_ROPE_OFF = MLA_NOPE + MLA_ROPE // 2


def _mla_proj_kernel(x_ref, mod_ref, n1_ref, wa_ref, qn_ref, wuq_ref, kvn_ref,
                     cq_ref, sq_ref, ck_ref, sk_ref,
                     q_ref, ckv_ref, kp_ref, kst_ref):
    h = _modulated(x_ref[...], n1_ref[...], mod_ref[0], 0).astype(BF16)
    a = _dot(h, wa_ref[...])
    cq = _rms(a[:, :MLA_Q_RANK], qn_ref[...]).astype(BF16)
    ckv_ref[...] = _rms(a[:, MLA_Q_RANK:MLA_Q_RANK + MLA_KV_RANK], kvn_ref[...])
    k0 = MLA_Q_RANK + MLA_KV_RANK
    kp = a[:, k0:k0 + LANES]
    kp_ref[...] = _rope_slab(kp, ck_ref[...], sk_ref[...], MLA_ROPE // 2).astype(BF16)
    kst_ref[...] = a[:, k0 + LANES:k0 + LANES + MLA_ROPE]
    qa = _dot(cq, wuq_ref[...])
    cq_t = cq_ref[...]
    sq_t = sq_ref[...]
    for hd in range(MLA_HEADS):
        sl = slice(hd * LANES, (hd + 1) * LANES)
        q_ref[:, sl] = _rope_slab(qa[:, sl], cq_t, sq_t, MLA_ROPE // 2).astype(BF16)


def _mla_proj(x, mods, norm1, wa_ext, q_norm, wuq_ext, kv_norm, tabs):
    cq, sq, ck, sk = tabs
    n_tiles = T_ALL // TM
    row = lambda i: (i, 0)
    tab = pl.BlockSpec((TM, LANES), lambda i: (_pos_block(i, TM), 0))
    return pl.pallas_call(
        _mla_proj_kernel,
        grid=(n_tiles,),
        in_specs=[pl.BlockSpec((TM, D_MODEL), row), _mod_spec(TM), _full((1, D_MODEL)),
                  _full((D_MODEL, MLA_A_COLS)), _full((1, MLA_Q_RANK)),
                  _full((MLA_Q_RANK, MLA_HEADS * LANES)), _full((1, MLA_KV_RANK)),
                  tab, tab, tab, tab],
        out_specs=[pl.BlockSpec((TM, MLA_HEADS * LANES), row), pl.BlockSpec((TM, MLA_KV_RANK), row),
                   pl.BlockSpec((TM, LANES), row), pl.BlockSpec((TM, MLA_ROPE), row)],
        out_shape=[jax.ShapeDtypeStruct((T_ALL, MLA_HEADS * LANES), BF16),
                   jax.ShapeDtypeStruct((T_ALL, MLA_KV_RANK), F32),
                   jax.ShapeDtypeStruct((T_ALL, LANES), BF16),
                   jax.ShapeDtypeStruct((T_ALL, MLA_ROPE), F32)],
        compiler_params=_params("arbitrary"),
        name="mla_proj",
    )(x, mods, norm1.reshape(1, -1), wa_ext, q_norm.reshape(1, -1), wuq_ext, kv_norm.reshape(1, -1),
      cq, sq, ck, sk)


def _mla_attn_kernel(*refs, n_lat, n_cache):
    if n_cache:
        (q_ref, ckv_ref, kp_ref, cckv_ref, ckp_ref, wk_ref, wv_ref, wo_ref, x_ref, mod_ref,
         o_ref, k_s, v_s, o_s) = refs
    else:
        (q_ref, ckv_ref, kp_ref, wk_ref, wv_ref, wo_ref, x_ref, mod_ref, o_ref, k_s, v_s, o_s) = refs

    def expand(ckv_of, kp_of, off, rows):
        rc = min(rows, 256)
        for r0 in range(0, rows, rc):
            c = ckv_of(r0, rc).astype(BF16)
            kp = kp_of(r0, rc).astype(F32)
            kn = _dot(c, wk_ref[...])
            v_s[off + r0:off + r0 + rc, :] = _dot(c, wv_ref[...]).astype(BF16)
            for hd in range(MLA_HEADS):
                sl = slice(hd * LANES, (hd + 1) * LANES)
                k_s[off + r0:off + r0 + rc, sl] = (kn[:, sl] + kp).astype(BF16)

    @pl.when(pl.program_id(1) == 0)
    def _():
        if n_cache:
            expand(lambda r, n: cckv_ref[0, r:r + n, :], lambda r, n: ckp_ref[0, r:r + n, :], 0, n_cache)
        expand(lambda r, n: ckv_ref[r:r + n, :], lambda r, n: kp_ref[r:r + n, :], n_cache, n_lat)

    def head(hd, v_pair):
        sl = slice(hd * LANES, (hd + 1) * LANES)
        s = _dot_nt(q_ref[:, sl], k_s[:, sl])
        m = jnp.max(s, axis=-1, keepdims=True)
        p = jnp.exp(s - m)
        l = jnp.sum(p, axis=-1, keepdims=True)
        return _dot(p.astype(BF16), v_pair) / l

    low_half = lax.broadcasted_iota(jnp.int32, (q_ref.shape[0], LANES), 1) < MLA_V
    for hp in range(MLA_HEADS // 2):
        sl = slice(hp * LANES, (hp + 1) * LANES)
        v_pair = v_s[:, sl]
        o_s[:, sl] = jnp.where(low_half, head(2 * hp, v_pair), head(2 * hp + 1, v_pair)).astype(BF16)
    mix = _dot(o_s[...], wo_ref[...])
    o_ref[...] = x_ref[...] + mod_ref[0][2:3] * mix


def _mla_attn(x, mods, q, ckv, kp, wk_ext, wv_ext, wo_ext, cache=None):
    hw = MLA_HEADS * LANES
    if cache is None:
        n_lat, n_cache, nb, nq, base = SEQ, 0, BATCH, SEQ // TQ, 0
    else:
        n_lat, n_cache, nb, nq, base = DEC_SEQ, PAST_LEN, DEC_BATCH, DEC_SEQ // TQ, T_CTX // TQ
    seq_base = base * TQ // n_lat
    tile = lambda b, qi: (base + b * nq + qi, 0)
    seq = lambda b, qi: (seq_base + b, 0)
    in_specs = [pl.BlockSpec((TQ, hw), tile),
                pl.BlockSpec((n_lat, MLA_KV_RANK), seq, pipeline_mode=pl.Buffered(1)),
                pl.BlockSpec((n_lat, LANES), seq, pipeline_mode=pl.Buffered(1))]
    args = [q, ckv, kp]
    if cache is not None:
        in_specs += [pl.BlockSpec((1, n_cache, MLA_KV_RANK), lambda b, qi: (b, 0, 0)),
                     pl.BlockSpec((1, n_cache, LANES), lambda b, qi: (b, 0, 0))]
        args += list(cache)
    vw = MLA_HEADS * MLA_V
    in_specs += [_full((MLA_KV_RANK, hw)), _full((MLA_KV_RANK, vw)), _full((vw, D_MODEL)),
                 pl.BlockSpec((TQ, D_MODEL), tile),
                 pl.BlockSpec((1, 6, D_MODEL), lambda b, qi: (_cond_of_row((base + b * nq + qi) * TQ), 0, 0))]
    args += [wk_ext, wv_ext, wo_ext, x, mods]
    sk = n_cache + n_lat
    return pl.pallas_call(
        functools.partial(_mla_attn_kernel, n_lat=n_lat, n_cache=n_cache),
        grid=(nb, nq),
        in_specs=in_specs,
        out_specs=pl.BlockSpec((TQ, D_MODEL), tile),
        out_shape=jax.ShapeDtypeStruct((T_ALL, D_MODEL), F32),
        scratch_shapes=[pltpu.VMEM((sk, hw), BF16), pltpu.VMEM((sk, vw), BF16), pltpu.VMEM((TQ, vw), BF16)],
        input_output_aliases={len(args) - 2: 0},
        compiler_params=_params("arbitrary", "arbitrary"),
        name="mla_attn_lat" if cache is not None else "mla_attn_ctx",
    )(*args)


SWA_QW = SWA_HEADS * LANES
SWA_KW = SWA_KV_HEADS * LANES
SWA_CW = 2 * SWA_KV_HEADS * SWA_HEAD_DIM


def _swa_proj_kernel(*refs, rope):
    if rope:
        (x_ref, mod_ref, n1_ref, w_ref, b_ref, cq_ref, sq_ref, ck_ref, sk_ref, q_ref, k_ref, v_ref) = refs
    else:
        (x_ref, mod_ref, n1_ref, w_ref, b_ref, q_ref, k_ref, v_ref, kv_ref) = refs
    h = _modulated(x_ref[...], n1_ref[...], mod_ref[0], 0).astype(BF16)
    a = _dot(h, w_ref[...]) + b_ref[...]
    if rope:
        cq, sq, ck, sk = cq_ref[...], sq_ref[...], ck_ref[...], sk_ref[...]
        for hd in range(SWA_HEADS):
            sl = slice(hd * LANES, (hd + 1) * LANES)
            q_ref[:, sl] = _rope_slab(a[:, sl], cq, sq, SWA_HEAD_DIM // 2).astype(BF16)
        for hd in range(SWA_KV_HEADS):
            sl = slice(hd * LANES, (hd + 1) * LANES)
            k_ref[:, sl] = _rope_slab(a[:, SWA_QW + hd * LANES:SWA_QW + (hd + 1) * LANES], ck, sk,
                                      SWA_HEAD_DIM // 2).astype(BF16)
    else:
        q_ref[...] = (a[:, :SWA_QW] * SWA_SCALE).astype(BF16)
        k_ref[...] = a[:, SWA_QW:SWA_QW + SWA_KW].astype(BF16)
        kv_ref[...] = a[:, SWA_QW + 2 * SWA_KW:]
    v_ref[...] = a[:, SWA_QW + SWA_KW:SWA_QW + 2 * SWA_KW].astype(BF16)


def _swa_proj(x, mods, norm1, w_ext, b_ext, tabs, rope):
    base = T_CTX // TM if rope else 0
    n_tiles = (T_LAT if rope else T_CTX) // TM
    n_rows = n_tiles * TM
    ncols = w_ext.shape[1]
    row = lambda i: (i, 0)
    in_specs = [pl.BlockSpec((TM, D_MODEL), lambda i: (i + base, 0)), _mod_spec(TM, base), _full((1, D_MODEL)),
                _full((D_MODEL, ncols)), _full((1, ncols))]
    args = [x, mods, norm1.reshape(1, -1), w_ext, b_ext]
    out_specs = [pl.BlockSpec((TM, SWA_QW), row), pl.BlockSpec((TM, SWA_KW), row), pl.BlockSpec((TM, SWA_KW), row)]
    out_shape = [jax.ShapeDtypeStruct((n_rows, SWA_QW), BF16), jax.ShapeDtypeStruct((n_rows, SWA_KW), BF16),
                 jax.ShapeDtypeStruct((n_rows, SWA_KW), BF16)]
    if rope:
        tab = pl.BlockSpec((TM, LANES), lambda i: (_pos_block(i + base, TM), 0))
        in_specs += [tab] * 4
        args += list(tabs)
    else:
        out_specs.append(pl.BlockSpec((TM, SWA_CW), row))
        out_shape.append(jax.ShapeDtypeStruct((n_rows, SWA_CW), F32))
    return pl.pallas_call(
        functools.partial(_swa_proj_kernel, rope=rope),
        grid=(n_tiles,),
        in_specs=in_specs, out_specs=out_specs, out_shape=out_shape,
        compiler_params=_params("arbitrary"),
        name="swa_proj_lat" if rope else "swa_proj_ctx",
    )(*args)


def _swa_ctx_kernel(q_ref, k_ref, v_ref, sink_ref, wo_ref, bo_ref, x_ref, mod_ref, o_ref, o_s):
    for n in range(SWA_KV_HEADS):
        kn = k_ref[:, n * LANES:(n + 1) * LANES]
        vn = v_ref[:, n * LANES:(n + 1) * LANES]
        for g in range(SWA_GROUP):
            j = n * SWA_GROUP + g
            sl = slice(j * LANES, (j + 1) * LANES)
            s = _dot_nt(q_ref[:, sl], kn)
            sink = sink_ref[j]
            m = jnp.maximum(jnp.max(s, axis=-1, keepdims=True), sink)
            p = jnp.exp(s - m)
            l = jnp.sum(p, axis=-1, keepdims=True) + jnp.exp(sink - m)
            o_s[:, sl] = (_dot(p.astype(BF16), vn) / l).astype(BF16)
    mix = _dot(o_s[...], wo_ref[...]) + bo_ref[...]
    o_ref[...] = x_ref[...] + mod_ref[0][2:3] * mix


def _swa_lat_kernel(q_ref, k_ref, v_ref, ck_ref, cv_ref, sink_ref, wo_ref, bo_ref, x_ref, mod_ref, o_ref, o_s):
    span = TQ + 2 * SWA_WINDOW
    start = pl.program_id(1) * TQ
    ks = pl.multiple_of(jnp.clip(start - SWA_WINDOW, 0, DEC_SEQ - span), SWA_WINDOW)
    qpos = start + lax.broadcasted_iota(jnp.int32, (TQ, span), 0)
    kpos = ks + lax.broadcasted_iota(jnp.int32, (TQ, span), 1)
    valid = jnp.abs(kpos - qpos) <= SWA_WINDOW
    for n in range(SWA_KV_HEADS):
        nl = slice(n * LANES, (n + 1) * LANES)
        kc = ck_ref[0, :, nl]
        vc = cv_ref[0, :, nl]
        kl = k_ref[pl.ds(ks, span), nl]
        vl = v_ref[pl.ds(ks, span), nl]
        for g in range(SWA_GROUP):
            j = n * SWA_GROUP + g
            sl = slice(j * LANES, (j + 1) * LANES)
            qj = q_ref[:, sl]
            s_c = _dot_nt(qj, kc)
            s_l = jnp.where(valid, _dot_nt(qj, kl), -jnp.inf)
            sink = sink_ref[j]
            m = jnp.maximum(jnp.maximum(jnp.max(s_c, axis=-1, keepdims=True),
                                        jnp.max(s_l, axis=-1, keepdims=True)), sink)
            p_c = jnp.exp(s_c - m)
            p_l = jnp.exp(s_l - m)
            l = (jnp.sum(p_c, axis=-1, keepdims=True) + jnp.sum(p_l, axis=-1, keepdims=True)
                 + jnp.exp(sink - m))
            o = _dot(p_c.astype(BF16), vc) + _dot(p_l.astype(BF16), vl)
            o_s[:, sl] = (o / l).astype(BF16)
    mix = _dot(o_s[...], wo_ref[...]) + bo_ref[...]
    o_ref[...] = x_ref[...] + mod_ref[0][2:3] * mix


def _swa_attn(x, mods, q, k, v, sink, wo_ext, bo, cache=None):
    smem = pl.BlockSpec(memory_space=pltpu.SMEM)
    if cache is None:
        nq, base = 1, 0
        tile = lambda b, qi: (b, 0)
        in_specs = [pl.BlockSpec((TQ, SWA_QW), tile), pl.BlockSpec((SEQ, SWA_KW), tile),
                    pl.BlockSpec((SEQ, SWA_KW), tile)]
        args = [q, k, v]
        kern, nb, name = _swa_ctx_kernel, BATCH, "swa_attn_ctx"
    else:
        nq, base = DEC_SEQ // TQ, T_CTX // TQ
        tile = lambda b, qi: (b * nq + qi, 0)
        seq = lambda b, qi: (b, 0)
        cspec = pl.BlockSpec((1, PAST_LEN, SWA_KW), lambda b, qi: (b, 0, 0))
        in_specs = [pl.BlockSpec((TQ, SWA_QW), tile), pl.BlockSpec((DEC_SEQ, SWA_KW), seq),
                    pl.BlockSpec((DEC_SEQ, SWA_KW), seq), cspec, cspec]
        args = [q, k, v, cache[0], cache[1]]
        kern, nb, name = _swa_lat_kernel, DEC_BATCH, "swa_attn_lat"
    xtile = lambda b, qi: (base + b * nq + qi, 0)
    in_specs += [smem, _full((SWA_QW, D_MODEL)), _full((1, D_MODEL)), pl.BlockSpec((TQ, D_MODEL), xtile),
                 pl.BlockSpec((1, 6, D_MODEL), lambda b, qi: (_cond_of_row((base + b * nq + qi) * TQ), 0, 0))]
    args += [sink, wo_ext, bo.reshape(1, -1), x, mods]
    return pl.pallas_call(
        kern,
        grid=(nb, nq),
        in_specs=in_specs,
        out_specs=pl.BlockSpec((TQ, D_MODEL), xtile),
        out_shape=jax.ShapeDtypeStruct((T_ALL, D_MODEL), F32),
        scratch_shapes=[pltpu.VMEM((TQ, SWA_QW), BF16)],
        input_output_aliases={len(args) - 2: 0},
        compiler_params=_params("arbitrary", "arbitrary"),
        name=name,
    )(*args)


CONV_EXT = CONV_TM + 2 * CONV_HALO


def _conv_kernel(xp_ref, x_ref, xn_ref, mod_ref, n1_ref, w1_ref, b1_ref, dw_ref, dwb_ref, lng_ref, lnb_ref,
                 w2_ref, b2_ref, o_ref, ext_s, y_s):
    i = pl.program_id(0)
    tiles_per_seq = DEC_SEQ // CONV_TM
    j = (i - T_CTX // CONV_TM) % tiles_per_seq
    latent = i >= T_CTX // CONV_TM
    left_ok = jnp.logical_and(latent, j > 0)
    right_ok = jnp.logical_and(latent, j < tiles_per_seq - 1)
    mod = mod_ref[0]

    def glu_of(xv):
        h = _modulated(xv, n1_ref[...], mod, 0).astype(BF16)
        u = _dot(h, w1_ref[...]) + b1_ref[...]
        return u[:, :D_MODEL] * _sigmoid(u[:, D_MODEL:])

    ext_s[CONV_HALO:CONV_HALO + CONV_TM, :] = glu_of(x_ref[...])
    ext_s[0:CONV_HALO, :] = jnp.where(left_ok, glu_of(xp_ref[...]), 0.0)
    ext_s[CONV_HALO + CONV_TM:, :] = jnp.where(right_ok, glu_of(xn_ref[...]), 0.0)

    rows = 128
    first = CONV_HALO - CONV_PAD
    for c in range(D_MODEL // LANES):
        cl = slice(c * LANES, (c + 1) * LANES)
        for r in range(CONV_TM // rows):
            acc = jnp.zeros((rows, LANES), F32)
            for w in range(CONV_WIDTH):
                r0 = r * rows + first + w
                acc = acc + ext_s[r0:r0 + rows, cl] * dw_ref[w:w + 1, cl]
            y_s[r * rows:(r + 1) * rows, cl] = acc
    y = y_s[...] + dwb_ref[...]
    mu = jnp.mean(y, axis=-1, keepdims=True)
    yc = y - mu
    var = jnp.mean(yc * yc, axis=-1, keepdims=True)
    yn = yc * lax.rsqrt(var + NORM_EPS) * lng_ref[...] + lnb_ref[...]
    z = (yn * _sigmoid(yn)).astype(BF16)
    mix = _dot(z, w2_ref[...]) + b2_ref[...]
    o_ref[...] = x_ref[...] + mod[2:3] * mix


def _conv_layer(x, mods, norm1, w1, b1, dw, dwb, ln_g, ln_b, w2, b2):
    n_tiles = T_ALL // CONV_TM
    per = CONV_TM // CONV_HALO
    n_halo_blocks = T_ALL // CONV_HALO
    row = lambda i: (i, 0)
    vec = lambda v: v.reshape(1, -1)
    return pl.pallas_call(
        _conv_kernel,
        grid=(n_tiles,),
        in_specs=[pl.BlockSpec((CONV_HALO, D_MODEL), lambda i: (jnp.maximum(i * per - 1, 0), 0)),
                  pl.BlockSpec((CONV_TM, D_MODEL), row),
                  pl.BlockSpec((CONV_HALO, D_MODEL), lambda i: (jnp.minimum((i + 1) * per, n_halo_blocks - 1), 0)),
                  _mod_spec(CONV_TM), _full((1, D_MODEL)),
                  _full((D_MODEL, 2 * D_MODEL)), _full((1, 2 * D_MODEL)),
                  _full((CONV_WIDTH, D_MODEL)), _full((1, D_MODEL)), _full((1, D_MODEL)), _full((1, D_MODEL)),
                  _full((D_MODEL, D_MODEL)), _full((1, D_MODEL))],
        out_specs=pl.BlockSpec((CONV_TM, D_MODEL), row),
        out_shape=jax.ShapeDtypeStruct((T_ALL, D_MODEL), F32),
        scratch_shapes=[pltpu.VMEM((CONV_EXT, D_MODEL), F32), pltpu.VMEM((CONV_TM, D_MODEL), F32)],
        compiler_params=_params("arbitrary"),
        name="conv_module",
    )(x, x, x, mods, vec(norm1), w1.astype(BF16), vec(b1), dw, vec(dwb), vec(ln_g), vec(ln_b),
      w2.astype(BF16), vec(b2))


def _router_kernel(x_ref, mod_ref, n2_ref, rw_ref, rb_ref, h_ref, idx_ref, gate_ref, cnt_ref):
    h = _modulated(x_ref[...], n2_ref[...], mod_ref[0], 3)
    _store_token_tiles(h_ref, h)
    cur = _dot_nt(rw_ref[...], h.astype(BF16)) + rb_ref[...]
    e_iota = lax.broadcasted_iota(jnp.int32, cur.shape, 0)
    vals, idxs = [], []
    for _ in range(TOP_K):
        m = jnp.max(cur, axis=0, keepdims=True)
        idx = jnp.min(jnp.where(cur == m, e_iota, N_EXPERTS), axis=0, keepdims=True)
        vals.append(m)
        idxs.append(idx)
        cur = jnp.where(e_iota == idx, -jnp.inf, cur)
    ex = [jnp.exp(v - vals[0]) for v in vals]
    tot = ex[0] + ex[1] + ex[2] + ex[3]
    for k in range(TOP_K):
        idx_ref[k:k + 1, :] = idxs[k]
        gate_ref[k:k + 1, :] = ex[k] / tot

    picked = jnp.sum(jnp.where(cur == -jnp.inf, 1.0, 0.0), axis=1, keepdims=True)

    @pl.when(pl.program_id(0) == 0)
    def _():
        cnt_ref[...] = jnp.zeros_like(cnt_ref)

    cnt_ref[...] += picked


def _router(x, mods, norm2, router_w, router_b):
    n_tiles = T_ALL // TM
    row = lambda i: (i, 0)
    col = lambda i: (0, i)
    return pl.pallas_call(
        _router_kernel,
        grid=(n_tiles,),
        in_specs=[pl.BlockSpec((TM, D_MODEL), row), _mod_spec(TM), _full((1, D_MODEL)),
                  _full((N_EXPERTS, D_MODEL)), _full((N_EXPERTS, 1))],
        out_specs=[pl.BlockSpec((TM * TOK_ROWS, LANES), row), pl.BlockSpec((TOP_K, TM), col),
                   pl.BlockSpec((TOP_K, TM), col), pl.BlockSpec((N_EXPERTS, 1), lambda i: (0, 0))],
        out_shape=[jax.ShapeDtypeStruct((T_ALL * TOK_ROWS, LANES), F32),
                   jax.ShapeDtypeStruct((TOP_K, T_ALL), jnp.int32),
                   jax.ShapeDtypeStruct((TOP_K, T_ALL), F32),
                   jax.ShapeDtypeStruct((N_EXPERTS, 1), F32)],
        compiler_params=_params("arbitrary"),
        name="router",
    )(x, mods, norm2.reshape(1, -1), router_w.T.astype(BF16), router_b.reshape(-1, 1))


N_ASSIGN = TOP_K * T_ALL
RANK_CHUNK = 256
RANK_BLOCK = 2048


def _block_layout(counts):
    counts = counts.reshape(N_EXPERTS).astype(jnp.int32)
    padded = (counts + MOE_BM - 1) // MOE_BM * MOE_BM
    pad_end = jnp.cumsum(padded)
    pad_start = pad_end - padded
    starts = jnp.arange(MOE_BLOCKS + 1, dtype=jnp.int32) * MOE_BM
    block_e = jnp.minimum(jnp.sum((pad_end[None, :] <= starts[:, None]).astype(jnp.int32), axis=1), N_EXPERTS - 1)
    n_used = (pad_end[-1] // MOE_BM).reshape(1)
    return pad_start.astype(F32).reshape(N_EXPERTS, 1), block_e, n_used


def _rank_kernel(idx_ref, ps_ref, u_ref, dest_ref, carry):
    @pl.when(pl.program_id(0) == 0)
    def _():
        carry[...] = ps_ref[...]

    e_iota = lax.broadcasted_iota(jnp.int32, (N_EXPERTS, RANK_CHUNK), 0)
    for c in range(RANK_BLOCK // RANK_CHUNK):
        sl = slice(c * RANK_CHUNK, (c + 1) * RANK_CHUNK)
        hit = e_iota == idx_ref[:, sl]
        pref = _dot(jnp.where(hit, 1.0, 0.0).astype(BF16), u_ref[...])
        base = carry[...]
        dest = jnp.sum(jnp.where(hit, pref + base, 0.0), axis=0, keepdims=True) - 1.0
        dest_ref[:, sl] = dest.astype(jnp.int32)
        carry[...] = base + pref[:, RANK_CHUNK - 1:RANK_CHUNK]


def _assignment_slots(idx_t, pad_start):
    tri = (jnp.arange(RANK_CHUNK)[:, None] <= jnp.arange(RANK_CHUNK)[None, :]).astype(BF16)
    blk = pl.BlockSpec((1, RANK_BLOCK), lambda i: (0, i))
    return pl.pallas_call(
        _rank_kernel,
        grid=(N_ASSIGN // RANK_BLOCK,),
        in_specs=[blk, _full((N_EXPERTS, 1)), _full((RANK_CHUNK, RANK_CHUNK))],
        out_specs=blk,
        out_shape=jax.ShapeDtypeStruct((1, N_ASSIGN), jnp.int32),
        scratch_shapes=[pltpu.VMEM((N_EXPERTS, 1), F32)],
        compiler_params=_params("arbitrary"),
        name="moe_rank",
    )(idx_t.reshape(1, N_ASSIGN), pad_start, tri)


INV_UNROLL = 32
INV_CHUNK = 4096
INV_STEPS = N_ASSIGN // INV_CHUNK
N_SLOTS = MOE_CAP + MOE_BM


def _inverse_kernel(dest_ref, pad_hbm, slot_ref, sem):
    i = pl.program_id(0)

    @pl.when(i == 0)
    def _():
        fill = pltpu.make_async_copy(pad_hbm, slot_ref, sem)
        fill.start()
        fill.wait()

    base = i * INV_CHUNK

    def place(j, carry):
        for u in range(INV_UNROLL):
            o = j * INV_UNROLL + u
            slot_ref[dest_ref[0, 0, o]] = base + o
        return carry

    lax.fori_loop(0, INV_CHUNK // INV_UNROLL, place, 0)


def _slot_assignments(dest):
    s = jnp.arange(N_SLOTS, dtype=jnp.int32)
    pad_ids = N_ASSIGN + (s & (MOE_BM - 1)) + jnp.where(s >= MOE_CAP, MOE_BM, 0)
    return pl.pallas_call(
        _inverse_kernel,
        grid=(INV_STEPS,),
        in_specs=[pl.BlockSpec((1, 1, INV_CHUNK), lambda i: (i, 0, 0), memory_space=pltpu.SMEM),
                  pl.BlockSpec(memory_space=pl.ANY)],
        out_specs=pl.BlockSpec(memory_space=pltpu.SMEM),
        out_shape=jax.ShapeDtypeStruct((N_SLOTS,), jnp.int32),
        scratch_shapes=[pltpu.SemaphoreType.DMA(())],
        compiler_params=_params("arbitrary"),
        name="moe_slots",
    )(dest.reshape(INV_STEPS, 1, INV_CHUNK), pad_ids)


YG_TOKENS = N_ASSIGN + 2 * MOE_BM
BLOCK_TILE_ROWS = MOE_BM * TOK_ROWS
MOE_CHUNK = 256


def _moe_kernel(be_ref, nu_ref, sa_prev_ref, sa_ref, sa_next_ref, h_hbm, wg_ref, wu_ref, wd_ref,
                bg_ref, bu_ref, bd_ref, yg_hbm, xbuf, ybuf, x_s, act_s, wgu_s, wd_s, gsem, ssem):
    i = pl.program_id(0)
    n_used = nu_ref[0]
    cur = i % 2
    nxt = 1 - cur

    def start_gather(sa, buf):
        for r in range(MOE_BM):
            tok = sa[0, 0, r] & (T_ALL - 1)
            pltpu.make_async_copy(h_hbm.at[pl.ds(tok * TOK_ROWS, TOK_ROWS)],
                                  xbuf.at[buf, pl.ds(r * TOK_ROWS, TOK_ROWS)], gsem.at[buf]).start()

    def start_scatter(sa, buf):
        for r in range(MOE_BM):
            pltpu.make_async_copy(ybuf.at[buf, pl.ds(r * TOK_ROWS, TOK_ROWS)],
                                  yg_hbm.at[pl.ds(sa[0, 0, r] * TOK_ROWS, TOK_ROWS)], ssem.at[buf]).start()

    def wait_gather(buf):
        pltpu.make_async_copy(h_hbm.at[pl.ds(0, BLOCK_TILE_ROWS)], xbuf.at[buf], gsem.at[buf]).wait()

    def wait_scatter(buf):
        pltpu.make_async_copy(ybuf.at[buf], yg_hbm.at[pl.ds(0, BLOCK_TILE_ROWS)], ssem.at[buf]).wait()

    @pl.when(i == 0)
    def _():
        start_gather(sa_ref, 0)
        ybuf[...] = jnp.zeros_like(ybuf)
        pltpu.make_async_copy(ybuf.at[0], yg_hbm.at[pl.ds(N_ASSIGN * TOK_ROWS, BLOCK_TILE_ROWS)], ssem.at[0]).start()

    @pl.when(i <= n_used)
    def _():
        prev = be_ref[jnp.maximum(i - 1, 0)]

        @pl.when(jnp.logical_or(i == 0, be_ref[i] != prev))
        def _():
            for c in range(D_MODEL // MOE_CHUNK):
                cs = slice(c * MOE_CHUNK, (c + 1) * MOE_CHUNK)
                wgu_s[:, 2 * c * MOE_CHUNK:(2 * c + 1) * MOE_CHUNK] = wg_ref[0, :, cs].astype(BF16)
                wgu_s[:, (2 * c + 1) * MOE_CHUNK:(2 * c + 2) * MOE_CHUNK] = wu_ref[0, :, cs].astype(BF16)
            wd_s[...] = wd_ref[0].astype(BF16)

        wait_gather(cur)
        x_s[...] = _load_token_tiles(xbuf.at[cur], MOE_BM).astype(BF16)
        start_gather(sa_next_ref, nxt)
        for c in range(D_MODEL // MOE_CHUNK):
            cs = slice(c * MOE_CHUNK, (c + 1) * MOE_CHUNK)
            gu = _dot(x_s[...], wgu_s[:, 2 * c * MOE_CHUNK:(2 * c + 2) * MOE_CHUNK])
            g = jnp.minimum(gu[:, :MOE_CHUNK] + bg_ref[0, :, cs], SWIGLU_LIMIT)
            u = jnp.clip(gu[:, MOE_CHUNK:] + bu_ref[0, :, cs], -SWIGLU_LIMIT, SWIGLU_LIMIT)
            act_s[:, cs] = ((u + 1.0) * (g * _sigmoid(SWIGLU_ALPHA * g))).astype(BF16)
        wait_scatter(cur)
        start_scatter(sa_prev_ref, nxt)
        half = D_MODEL // 2
        for c in range(2):
            y = _dot(act_s[...], wd_s[:, c * half:(c + 1) * half]) + bd_ref[0, :, c * half:(c + 1) * half]
            for j in range(TOK_ROWS // 2):
                jj = c * (TOK_ROWS // 2) + j
                ybuf.at[cur][pl.ds(jj, MOE_BM, stride=TOK_ROWS), :] = y[:, j * LANES:(j + 1) * LANES]

    @pl.when(i == n_used)
    def _():
        wait_gather(nxt)
        wait_scatter(nxt)


def _moe_experts(h, slots, block_e, n_used, wg, bg, wu, bu, wd, bd):
    wspec = pl.BlockSpec((1, D_MODEL, D_MODEL), lambda i, be, nu: (be[i], 0, 0))
    bspec = pl.BlockSpec((1, 1, D_MODEL), lambda i, be, nu: (be[i], 0, 0))
    sspec = lambda f: pl.BlockSpec((1, 1, MOE_BM), lambda i, be, nu: (f(i), 0, 0), memory_space=pltpu.SMEM)
    hbm = pl.BlockSpec(memory_space=pl.ANY)
    b3 = lambda b: b.reshape(N_EXPERTS, 1, D_MODEL)
    slots3 = slots.reshape(MOE_BLOCKS + 1, 1, MOE_BM)
    return pl.pallas_call(
        _moe_kernel,
        grid_spec=pltpu.PrefetchScalarGridSpec(
            num_scalar_prefetch=2,
            grid=(MOE_BLOCKS + 1,),
            in_specs=[sspec(lambda i: jnp.where(i == 0, MOE_BLOCKS, i - 1)), sspec(lambda i: i),
                      sspec(lambda i: jnp.minimum(i + 1, MOE_BLOCKS)), hbm,
                      wspec, wspec, wspec, bspec, bspec, bspec],
            out_specs=hbm,
            scratch_shapes=[pltpu.VMEM((2, BLOCK_TILE_ROWS, LANES), F32), pltpu.VMEM((2, BLOCK_TILE_ROWS, LANES), F32),
                            pltpu.VMEM((MOE_BM, D_MODEL), BF16), pltpu.VMEM((MOE_BM, D_MODEL), BF16),
                            pltpu.VMEM((D_MODEL, 2 * D_MODEL), BF16), pltpu.VMEM((D_MODEL, D_MODEL), BF16),
                            pltpu.SemaphoreType.DMA((2,)), pltpu.SemaphoreType.DMA((2,))]),
        out_shape=jax.ShapeDtypeStruct((YG_TOKENS * TOK_ROWS, LANES), F32),
        compiler_params=_params("arbitrary"),
        name="moe_experts",
    )(block_e, n_used, slots3, slots3, slots3, h, wg, wu, wd, b3(bg), b3(bu), b3(bd))


def _combine_kernel(x_ref, y0_ref, y1_ref, y2_ref, y3_ref, gate_ref, mod_ref, o_ref):
    gate = gate_ref[...]
    n = x_ref.shape[0]
    g2 = mod_ref[0][5:6]
    for j in range(TOK_ROWS):
        rows = pl.ds(j, n, stride=TOK_ROWS)
        sl = slice(j * LANES, (j + 1) * LANES)
        acc = y0_ref[rows, :] * gate[:, 0:1]
        for k, y_ref in enumerate((y1_ref, y2_ref, y3_ref), start=1):
            acc = acc + y_ref[rows, :] * gate[:, k:k + 1]
        o_ref[:, sl] = x_ref[:, sl] + g2[:, sl] * acc


def _combine(x, yg, gates, mods):
    row = lambda i: (i, 0)
    n_tiles = T_ALL // TM
    ysp = lambda k: pl.BlockSpec((TM * TOK_ROWS, LANES), lambda i: (k * n_tiles + i, 0))
    return pl.pallas_call(
        _combine_kernel,
        grid=(n_tiles,),
        in_specs=[pl.BlockSpec((TM, D_MODEL), row), ysp(0), ysp(1), ysp(2), ysp(3),
                  pl.BlockSpec((TM, TOP_K), row), _mod_spec(TM)],
        out_specs=pl.BlockSpec((TM, D_MODEL), row),
        out_shape=jax.ShapeDtypeStruct((T_ALL, D_MODEL), F32),
        input_output_aliases={0: 0},
        compiler_params=_params("arbitrary"),
        name="moe_combine",
    )(x, yg, yg, yg, yg, gates, mods)


def _moe_layer(x, mods, norm2, router_w, router_b, wg, bg, wu, bu, wd, bd):
    h, idx_t, gate_t, counts = _router(x, mods, norm2, router_w, router_b)
    pad_start, block_e, n_used = _block_layout(counts)
    slots = _slot_assignments(_assignment_slots(idx_t, pad_start))
    yg = _moe_experts(h, slots, block_e, n_used, wg, bg, wu, bu, wd, bd)
    return _combine(x, yg, gate_t.T, mods)


def _final_norm_kernel(x_ref, g_ref, o_ref):
    o_ref[...] = _rms(x_ref[...], g_ref[...])


def _final_norm(x, g, base_rows, n_rows):
    base = base_rows // TM
    return pl.pallas_call(
        _final_norm_kernel,
        grid=(n_rows // TM,),
        in_specs=[pl.BlockSpec((TM, D_MODEL), lambda i: (i + base, 0)), _full((1, D_MODEL))],
        out_specs=pl.BlockSpec((TM, D_MODEL), lambda i: (i, 0)),
        out_shape=jax.ShapeDtypeStruct((n_rows, D_MODEL), F32),
        compiler_params=_params("arbitrary"),
        name="final_norm",
    )(x, g.reshape(1, -1))


def _pad_lanes(w, lead_pad, width):
    tail = width - lead_pad - w.shape[-1]
    cfg = [(0, 0)] * (w.ndim - 1) + [(lead_pad, tail)]
    return jnp.pad(w, cfg)


def _mla_weights(wa, wuq, wukv, wo):
    krope_w = wa[:, MLA_Q_RANK + MLA_KV_RANK:]
    wa_ext = jnp.concatenate([wa[:, :MLA_Q_RANK + MLA_KV_RANK],
                              _pad_lanes(krope_w, MLA_ROPE_OFF, LANES),
                              _pad_lanes(krope_w, 0, LANES)], axis=1).astype(BF16)
    q3 = wuq.reshape(MLA_Q_RANK, MLA_HEADS, MLA_NOPE + MLA_ROPE)
    wuq_ext = jnp.concatenate([q3[..., :MLA_NOPE], _pad_lanes(q3[..., MLA_NOPE:], MLA_ROPE // 2, LANES - MLA_NOPE)],
                              axis=-1).reshape(MLA_Q_RANK, MLA_HEADS * LANES).astype(BF16)
    kv3 = wukv.reshape(MLA_KV_RANK, MLA_HEADS, MLA_NOPE + MLA_V)
    wk_ext = _pad_lanes(kv3[..., :MLA_NOPE], 0, LANES).reshape(MLA_KV_RANK, MLA_HEADS * LANES).astype(BF16)
    wv = kv3[..., MLA_NOPE:].reshape(MLA_KV_RANK, MLA_HEADS * MLA_V).astype(BF16)
    return wa_ext, wuq_ext, wk_ext, wv, wo.astype(BF16)


def _swa_weights(wqkv, bqkv, wo):
    nq = SWA_HEADS * SWA_HEAD_DIM
    nkv = SWA_KV_HEADS * SWA_HEAD_DIM

    def slabs(w, heads):
        lead = w.shape[:-1]
        return _pad_lanes(w.reshape(lead + (heads, SWA_HEAD_DIM)), 0, LANES).reshape(lead + (heads * LANES,))

    def ext(w, with_compact):
        parts = [slabs(w[..., :nq], SWA_HEADS), slabs(w[..., nq:nq + nkv], SWA_KV_HEADS),
                 slabs(w[..., nq + nkv:], SWA_KV_HEADS)]
        if with_compact:
            parts.append(w[..., nq:])
        return jnp.concatenate(parts, axis=-1)

    b2 = bqkv.reshape(1, -1)
    wo3 = wo.reshape(SWA_HEADS, SWA_HEAD_DIM, D_MODEL)
    wo_ext = jnp.pad(wo3, ((0, 0), (0, LANES - SWA_HEAD_DIM), (0, 0))).reshape(SWA_QW, D_MODEL).astype(BF16)
    return (ext(wqkv, True).astype(BF16), ext(b2, True), ext(wqkv, False).astype(BF16), ext(b2, False), wo_ext)


def kernel(x_prompt, x_sample, cache_l0_ckv, cache_l0_krope, cache_l1_k, cache_l1_v, cache_l3_ckv, cache_l3_krope, c, c_ctx, l0_mla_wa, l0_mla_q_norm, l0_mla_wuq, l0_mla_kv_norm, l0_mla_wukv, l0_mla_wo, l0_mod_w, l0_mod_b, l0_norm1, l0_norm2, l0_router_w, l0_router_b, l0_moe_wg, l0_moe_bg, l0_moe_wu, l0_moe_bu, l0_moe_wd, l0_moe_bd, l1_swa_wqkv, l1_swa_bqkv, l1_swa_sink, l1_swa_wo, l1_swa_bo, l1_mod_w, l1_mod_b, l1_norm1, l1_norm2, l1_router_w, l1_router_b, l1_moe_wg, l1_moe_bg, l1_moe_wu, l1_moe_bu, l1_moe_wd, l1_moe_bd, l2_conv_w1, l2_conv_b1, l2_conv_dw, l2_conv_dwb, l2_conv_ln_g, l2_conv_ln_b, l2_conv_w2, l2_conv_b2, l2_mod_w, l2_mod_b, l2_norm1, l2_norm2, l2_router_w, l2_router_b, l2_moe_wg, l2_moe_bg, l2_moe_wu, l2_moe_bu, l2_moe_wd, l2_moe_bd, l3_mla_wa, l3_mla_q_norm, l3_mla_wuq, l3_mla_kv_norm, l3_mla_wukv, l3_mla_wo, l3_mod_w, l3_mod_b, l3_norm1, l3_norm2, l3_router_w, l3_router_b, l3_moe_wg, l3_moe_bg, l3_moe_wu, l3_moe_bu, l3_moe_wd, l3_moe_bd, final_norm):
    x = jnp.concatenate([x_prompt.reshape(T_CTX, D_MODEL), x_sample.reshape(T_LAT, D_MODEL)], axis=0)
    cond = jnp.concatenate([c, c_ctx[None, :], jnp.zeros((N_COND - DEC_BATCH - 1, D_MODEL), F32)], axis=0)

    mla_c, mla_s = _rope_tables(MLA_ROPE, MLA_ROPE_LEAD, TM)
    mla_tabs = (mla_c * MLA_SCALE, mla_s * MLA_SCALE, mla_c, mla_s)
    swa_c, swa_s = _rope_tables(SWA_HEAD_DIM, 0, TM)
    swa_tabs = (swa_c * SWA_SCALE, swa_s * SWA_SCALE, swa_c, swa_s)

    def mla_layer(x, mods, norm1, wa, q_norm, wuq, kv_norm, wukv, wo, cache_ckv, cache_krope):
        wa_ext, wuq_ext, wk_ext, wv_ext, wo_ext = _mla_weights(wa, wuq, wukv, wo)
        q, ckv, kp, kst = _mla_proj(x, mods, norm1, wa_ext, q_norm, wuq_ext, kv_norm, mla_tabs)
        x = _mla_attn(x, mods, q, ckv, kp, wk_ext, wv_ext, wo_ext)
        cache = (cache_ckv, _pad_lanes(cache_krope, MLA_ROPE_OFF, LANES).astype(BF16))
        x = _mla_attn(x, mods, q, ckv, kp, wk_ext, wv_ext, wo_ext, cache)
        return x, (ckv[:T_CTX].reshape(BATCH, SEQ, MLA_KV_RANK), kst[:T_CTX].reshape(BATCH, SEQ, MLA_ROPE))

    def moe(x, mods, norm2, rw, rb, wg, bg, wu, bu, wd, bd):
        return _moe_layer(x, mods, norm2, rw, rb, wg, bg, wu, bu, wd, bd)

    mods = _modulation(cond, l0_mod_w, l0_mod_b)
    x, (st0_ckv, st0_krope) = mla_layer(x, mods, l0_norm1, l0_mla_wa, l0_mla_q_norm, l0_mla_wuq, l0_mla_kv_norm,
                                        l0_mla_wukv, l0_mla_wo, cache_l0_ckv, cache_l0_krope)
    x = moe(x, mods, l0_norm2, l0_router_w, l0_router_b, l0_moe_wg, l0_moe_bg, l0_moe_wu, l0_moe_bu,
            l0_moe_wd, l0_moe_bd)

    mods = _modulation(cond, l1_mod_w, l1_mod_b)
    w_ctx, b_ctx, w_lat, b_lat, swa_wo_ext = _swa_weights(l1_swa_wqkv, l1_swa_bqkv, l1_swa_wo)
    q_c, k_c, v_c, kv_c = _swa_proj(x, mods, l1_norm1, w_ctx, b_ctx, None, rope=False)
    q_l, k_l, v_l = _swa_proj(x, mods, l1_norm1, w_lat, b_lat, swa_tabs, rope=True)
    x = _swa_attn(x, mods, q_c, k_c, v_c, l1_swa_sink, swa_wo_ext, l1_swa_bo)
    nkv = SWA_KV_HEADS * SWA_HEAD_DIM
    cache_k = _pad_lanes(cache_l1_k, 0, LANES).reshape(DEC_BATCH, PAST_LEN, SWA_KW).astype(BF16)
    cache_v = _pad_lanes(cache_l1_v, 0, LANES).reshape(DEC_BATCH, PAST_LEN, SWA_KW).astype(BF16)
    x = _swa_attn(x, mods, q_l, k_l, v_l, l1_swa_sink, swa_wo_ext, l1_swa_bo, (cache_k, cache_v))
    st1_k = kv_c[:, :nkv].reshape(BATCH, SEQ, SWA_KV_HEADS, SWA_HEAD_DIM)
    st1_v = kv_c[:, nkv:].reshape(BATCH, SEQ, SWA_KV_HEADS, SWA_HEAD_DIM)
    x = moe(x, mods, l1_norm2, l1_router_w, l1_router_b, l1_moe_wg, l1_moe_bg, l1_moe_wu, l1_moe_bu,
            l1_moe_wd, l1_moe_bd)

    mods = _modulation(cond, l2_mod_w, l2_mod_b)
    x = _conv_layer(x, mods, l2_norm1, l2_conv_w1, l2_conv_b1, l2_conv_dw, l2_conv_dwb, l2_conv_ln_g,
                    l2_conv_ln_b, l2_conv_w2, l2_conv_b2)
    x = moe(x, mods, l2_norm2, l2_router_w, l2_router_b, l2_moe_wg, l2_moe_bg, l2_moe_wu, l2_moe_bu,
            l2_moe_wd, l2_moe_bd)

    mods = _modulation(cond, l3_mod_w, l3_mod_b)
    x, (st3_ckv, st3_krope) = mla_layer(x, mods, l3_norm1, l3_mla_wa, l3_mla_q_norm, l3_mla_wuq, l3_mla_kv_norm,
                                        l3_mla_wukv, l3_mla_wo, cache_l3_ckv, cache_l3_krope)
    x = moe(x, mods, l3_norm2, l3_router_w, l3_router_b, l3_moe_wg, l3_moe_bg, l3_moe_wu, l3_moe_bu,
            l3_moe_wd, l3_moe_bd)

    y_prompt = _final_norm(x, final_norm, 0, T_CTX).reshape(BATCH, SEQ, D_MODEL)
    y_sample = _final_norm(x, final_norm, T_CTX, T_LAT).reshape(DEC_BATCH, DEC_SEQ, D_MODEL)
    return (y_prompt, y_sample, st0_ckv, st0_krope, st1_k, st1_v, st3_ckv, st3_krope)
```

```python
import functools

import jax
import jax.numpy as jnp
from jax import lax
from jax.experimental import pallas as pl
from jax.experimental.pallas import tpu as pltpu

F32 = jnp.float32
BF16 = jnp.bfloat16

D_MODEL = 1024
BATCH = 32
SEQ = 256
DEPTH = 4
DEC_BATCH = 4
DEC_SEQ = 2048
PAST_LEN = 256
GRID_W = 64
NORM_EPS = 1e-6
ROPE_THETA = 10000.0

MLA_HEADS = 16
MLA_NOPE = 64
MLA_ROPE = 32
MLA_V = 64
MLA_Q_RANK = 384
MLA_KV_RANK = 256
MLA_SCALE = (MLA_NOPE + MLA_ROPE) ** -0.5

SWA_HEADS = 16
SWA_KV_HEADS = 4
SWA_GROUP = 4
SWA_HEAD_DIM = 64
SWA_WINDOW = 128
SWA_SCALE = SWA_HEAD_DIM ** -0.5

CONV_WIDTH = 31
CONV_PAD = CONV_WIDTH // 2

N_EXPERTS = 32
TOP_K = 4
SWIGLU_ALPHA = 1.702
SWIGLU_LIMIT = 7.0

LANES = 128
T_CTX = BATCH * SEQ
T_LAT = DEC_BATCH * DEC_SEQ
T_ALL = T_CTX + T_LAT
CTX_COND = DEC_BATCH
N_COND = 8

TM = 512
TQ = 256
CONV_TM = 256
CONV_HALO = 16
MOE_BM = 256
MOE_BLOCKS = (T_ALL * TOP_K + N_EXPERTS * (MOE_BM - 1) + MOE_BM - 1) // MOE_BM
MOE_CAP = MOE_BLOCKS * MOE_BM
VMEM_LIMIT = 56 * 1024 * 1024


def _params(*sem):
    return pltpu.CompilerParams(dimension_semantics=sem, vmem_limit_bytes=VMEM_LIMIT)


def _cond_of_row(row0):
    return jnp.where(row0 < T_CTX, CTX_COND, (row0 - T_CTX) // DEC_SEQ)


def _pos_block(i, tm):
    row0 = i * tm
    return jnp.where(row0 < T_CTX, DEC_SEQ // tm, ((row0 - T_CTX) % DEC_SEQ) // tm)


def _rms(x, g):
    ms = jnp.mean(x * x, axis=-1, keepdims=True)
    return x * lax.rsqrt(ms + NORM_EPS) * g


def _modulated(x, g, mod, j):
    return _rms(x, g) * (1.0 + mod[j + 1:j + 2]) + mod[j:j + 1]


def _sigmoid(x):
    return 1.0 / (1.0 + jnp.exp(-x))


def _dot(a, b):
    return jnp.dot(a, b, preferred_element_type=F32)


def _dot_nt(a, b):
    return lax.dot_general(a, b, (((1,), (1,)), ((), ())), preferred_element_type=F32)


TOK_ROWS = D_MODEL // LANES


def _store_token_tiles(ref, v):
    n = v.shape[0]
    for j in range(TOK_ROWS):
        ref[pl.ds(j, n, stride=TOK_ROWS), :] = v[:, j * LANES:(j + 1) * LANES]


def _load_token_tiles(ref, n):
    return jnp.concatenate([ref[pl.ds(j, n, stride=TOK_ROWS), :] for j in range(TOK_ROWS)], axis=-1)


def _rope_slab(a, c, s, shift):
    return a * c + (pltpu.roll(a, shift, 1) - pltpu.roll(a, LANES - shift, 1)) * s


def _mod_kernel(c_ref, w_ref, b_ref, o_ref):
    c = c_ref[...]
    s = (c * _sigmoid(c)).astype(BF16)
    o_ref[...] = _dot(s, w_ref[...].astype(BF16)) + b_ref[...]


def _modulation(cond, w, b):
    tn = 1536
    out = pl.pallas_call(
        _mod_kernel,
        grid=(6 * D_MODEL // tn,),
        in_specs=[pl.BlockSpec((N_COND, D_MODEL), lambda j: (0, 0)),
                  pl.BlockSpec((D_MODEL, tn), lambda j: (0, j)),
                  pl.BlockSpec((1, tn), lambda j: (0, j))],
        out_specs=pl.BlockSpec((N_COND, tn), lambda j: (0, j)),
        out_shape=jax.ShapeDtypeStruct((N_COND, 6 * D_MODEL), F32),
        compiler_params=_params("arbitrary"),
        name="modulation",
    )(cond, w, b.reshape(1, -1))
    return out.reshape(N_COND, 6, D_MODEL)


def _mod_spec(tm, base=0):
    return pl.BlockSpec((1, 6, D_MODEL), lambda i, *_: (_cond_of_row((i + base) * tm), 0, 0))


def _full(shape):
    nd = len(shape)
    return pl.BlockSpec(shape, lambda *_: (0,) * nd, pipeline_mode=pl.Buffered(1))


def _rope_cos_sin(n_tok, rot_dim):
    rows = n_tok // GRID_W
    row = jnp.repeat(jnp.arange(rows, dtype=F32), GRID_W)
    col = jnp.tile(jnp.arange(GRID_W, dtype=F32), rows)
    n_freq = rot_dim // 4
    inv_freq = ROPE_THETA ** (-jnp.arange(n_freq, dtype=F32) / n_freq)
    ang = jnp.concatenate([row[:, None] * inv_freq, col[:, None] * inv_freq], axis=-1)
    return jnp.cos(ang), jnp.sin(ang)


def _rope_tables(rot_dim, lead, tm):
    cos, sin = _rope_cos_sin(DEC_SEQ, rot_dim)
    gap = rot_dim // 2 if lead else 0
    tail = LANES - lead - gap - rot_dim
    n = DEC_SEQ

    def slab(lead_val, rot):
        return jnp.concatenate([jnp.full((rot.shape[0], lead), lead_val, F32),
                                jnp.zeros((rot.shape[0], gap), F32), rot,
                                jnp.zeros((rot.shape[0], tail), F32)], axis=-1)

    c = slab(1.0, jnp.concatenate([cos, cos], axis=-1))
    s = slab(0.0, jnp.concatenate([sin, sin], axis=-1))
    c_id = slab(1.0, jnp.ones((tm, rot_dim), F32))
    s_id = jnp.zeros((tm, LANES), F32)
    del n
    return jnp.concatenate([c, c_id], axis=0), jnp.concatenate([s, s_id], axis=0)


MLA_A_COLS = MLA_Q_RANK + MLA_KV_RANK + 2 * LANES
MLA_ROPE_LEAD = MLA_NOPE
MLA_ROPE_OFF = MLA_NOPE + MLA_ROPE // 2


def _mla_proj_kernel(x_ref, mod_ref, n1_ref, wa_ref, qn_ref, wuq_ref, kvn_ref,
                     cq_ref, sq_ref, ck_ref, sk_ref,
                     q_ref, ckv_ref, kp_ref, kst_ref):
    h = _modulated(x_ref[...], n1_ref[...], mod_ref[0], 0).astype(BF16)
    a = _dot(h, wa_ref[...])
    cq = _rms(a[:, :MLA_Q_RANK], qn_ref[...]).astype(BF16)
    ckv_ref[...] = _rms(a[:, MLA_Q_RANK:MLA_Q_RANK + MLA_KV_RANK], kvn_ref[...])
    k0 = MLA_Q_RANK + MLA_KV_RANK
    kp = a[:, k0:k0 + LANES]
    kp_ref[...] = _rope_slab(kp, ck_ref[...], sk_ref[...], MLA_ROPE // 2).astype(BF16)
    kst_ref[...] = a[:, k0 + LANES:k0 + LANES + MLA_ROPE]
    qa = _dot(cq, wuq_ref[...])
    cq_t = cq_ref[...]
    sq_t = sq_ref[...]
    for hd in range(MLA_HEADS):
        sl = slice(hd * LANES, (hd + 1) * LANES)
        q_ref[:, sl] = _rope_slab(qa[:, sl], cq_t, sq_t, MLA_ROPE // 2).astype(BF16)


def _mla_proj(x, mods, norm1, wa_ext, q_norm, wuq_ext, kv_norm, tabs):
    cq, sq, ck, sk = tabs
    n_tiles = T_ALL // TM
    row = lambda i: (i, 0)
    tab = pl.BlockSpec((TM, LANES), lambda i: (_pos_block(i, TM), 0))
    return pl.pallas_call(
        _mla_proj_kernel,
        grid=(n_tiles,),
        in_specs=[pl.BlockSpec((TM, D_MODEL), row), _mod_spec(TM), _full((1, D_MODEL)),
                  _full((D_MODEL, MLA_A_COLS)), _full((1, MLA_Q_RANK)),
                  _full((MLA_Q_RANK, MLA_HEADS * LANES)), _full((1, MLA_KV_RANK)),
                  tab, tab, tab, tab],
        out_specs=[pl.BlockSpec((TM, MLA_HEADS * LANES), row), pl.BlockSpec((TM, MLA_KV_RANK), row),
                   pl.BlockSpec((TM, LANES), row), pl.BlockSpec((TM, MLA_ROPE), row)],
        out_shape=[jax.ShapeDtypeStruct((T_ALL, MLA_HEADS * LANES), BF16),
                   jax.ShapeDtypeStruct((T_ALL, MLA_KV_RANK), F32),
                   jax.ShapeDtypeStruct((T_ALL, LANES), BF16),
                   jax.ShapeDtypeStruct((T_ALL, MLA_ROPE), F32)],
        compiler_params=_params("arbitrary"),
        name="mla_proj",
    )(x, mods, norm1.reshape(1, -1), wa_ext, q_norm.reshape(1, -1), wuq_ext, kv_norm.reshape(1, -1),
      cq, sq, ck, sk)


def _mla_attn_kernel(*refs, n_lat, n_cache):
    if n_cache:
        (q_ref, ckv_ref, kp_ref, cckv_ref, ckp_ref, wk_ref, wv_ref, wo_ref, x_ref, mod_ref,
         o_ref, k_s, v_s, o_s) = refs
    else:
        (q_ref, ckv_ref, kp_ref, wk_ref, wv_ref, wo_ref, x_ref, mod_ref, o_ref, k_s, v_s, o_s) = refs

    def expand(ckv_of, kp_of, off, rows):
        rc = min(rows, 256)
        for r0 in range(0, rows, rc):
            c = ckv_of(r0, rc).astype(BF16)
            kp = kp_of(r0, rc).astype(F32)
            kn = _dot(c, wk_ref[...])
            v_s[off + r0:off + r0 + rc, :] = _dot(c, wv_ref[...]).astype(BF16)
            for hd in range(MLA_HEADS):
                sl = slice(hd * LANES, (hd + 1) * LANES)
                k_s[off + r0:off + r0 + rc, sl] = (kn[:, sl] + kp).astype(BF16)

    @pl.when(pl.program_id(1) == 0)
    def _():
        if n_cache:
            expand(lambda r, n: cckv_ref[0, r:r + n, :], lambda r, n: ckp_ref[0, r:r + n, :], 0, n_cache)
        expand(lambda r, n: ckv_ref[r:r + n, :], lambda r, n: kp_ref[r:r + n, :], n_cache, n_lat)

    def head(hd, v_pair):
        sl = slice(hd * LANES, (hd + 1) * LANES)
        s = _dot_nt(q_ref[:, sl], k_s[:, sl])
        m = jnp.max(s, axis=-1, keepdims=True)
        p = jnp.exp(s - m)
        l = jnp.sum(p, axis=-1, keepdims=True)
        return _dot(p.astype(BF16), v_pair) / l

    low_half = lax.broadcasted_iota(jnp.int32, (q_ref.shape[0], LANES), 1) < MLA_V
    for hp in range(MLA_HEADS // 2):
        sl = slice(hp * LANES, (hp + 1) * LANES)
        v_pair = v_s[:, sl]
        o_s[:, sl] = jnp.where(low_half, head(2 * hp, v_pair), head(2 * hp + 1, v_pair)).astype(BF16)
    mix = _dot(o_s[...], wo_ref[...])
    o_ref[...] = x_ref[...] + mod_ref[0][2:3] * mix


def _mla_attn(x, mods, q, ckv, kp, wk_ext, wv_ext, wo_ext, cache=None):
    hw = MLA_HEADS * LANES
    if cache is None:
        n_lat, n_cache, nb, nq, base = SEQ, 0, BATCH, SEQ // TQ, 0
    else:
        n_lat, n_cache, nb, nq, base = DEC_SEQ, PAST_LEN, DEC_BATCH, DEC_SEQ // TQ, T_CTX // TQ
    seq_base = base * TQ // n_lat
    tile = lambda b, qi: (base + b * nq + qi, 0)
    seq = lambda b, qi: (seq_base + b, 0)
    in_specs = [pl.BlockSpec((TQ, hw), tile),
                pl.BlockSpec((n_lat, MLA_KV_RANK), seq, pipeline_mode=pl.Buffered(1)),
                pl.BlockSpec((n_lat, LANES), seq, pipeline_mode=pl.Buffered(1))]
    args = [q, ckv, kp]
    if cache is not None:
        in_specs += [pl.BlockSpec((1, n_cache, MLA_KV_RANK), lambda b, qi: (b, 0, 0)),
                     pl.BlockSpec((1, n_cache, LANES), lambda b, qi: (b, 0, 0))]
        args += list(cache)
    vw = MLA_HEADS * MLA_V
    in_specs += [_full((MLA_KV_RANK, hw)), _full((MLA_KV_RANK, vw)), _full((vw, D_MODEL)),
                 pl.BlockSpec((TQ, D_MODEL), tile),
                 pl.BlockSpec((1, 6, D_MODEL), lambda b, qi: (_cond_of_row((base + b * nq + qi) * TQ), 0, 0))]
    args += [wk_ext, wv_ext, wo_ext, x, mods]
    sk = n_cache + n_lat
    return pl.pallas_call(
        functools.partial(_mla_attn_kernel, n_lat=n_lat, n_cache=n_cache),
        grid=(nb, nq),
        in_specs=in_specs,
        out_specs=pl.BlockSpec((TQ, D_MODEL), tile),
        out_shape=jax.ShapeDtypeStruct((T_ALL, D_MODEL), F32),
        scratch_shapes=[pltpu.VMEM((sk, hw), BF16), pltpu.VMEM((sk, vw), BF16), pltpu.VMEM((TQ, vw), BF16)],
        input_output_aliases={len(args) - 2: 0},
        compiler_params=_params("arbitrary", "arbitrary"),
        name="mla_attn_lat" if cache is not None else "mla_attn_ctx",
    )(*args)


SWA_QW = SWA_HEADS * LANES
SWA_KW = SWA_KV_HEADS * LANES
SWA_CW = 2 * SWA_KV_HEADS * SWA_HEAD_DIM


def _swa_proj_kernel(*refs, rope):
    if rope:
        (x_ref, mod_ref, n1_ref, w_ref, b_ref, cq_ref, sq_ref, ck_ref, sk_ref, q_ref, k_ref, v_ref) = refs
    else:
        (x_ref, mod_ref, n1_ref, w_ref, b_ref, q_ref, k_ref, v_ref, kv_ref) = refs
    h = _modulated(x_ref[...], n1_ref[...], mod_ref[0], 0).astype(BF16)
    a = _dot(h, w_ref[...]) + b_ref[...]
    if rope:
        cq, sq, ck, sk = cq_ref[...], sq_ref[...], ck_ref[...], sk_ref[...]
        for hd in range(SWA_HEADS):
            sl = slice(hd * LANES, (hd + 1) * LANES)
            q_ref[:, sl] = _rope_slab(a[:, sl], cq, sq, SWA_HEAD_DIM // 2).astype(BF16)
        for hd in range(SWA_KV_HEADS):
            sl = slice(hd * LANES, (hd + 1) * LANES)
            k_ref[:, sl] = _rope_slab(a[:, SWA_QW + hd * LANES:SWA_QW + (hd + 1) * LANES], ck, sk,
                                      SWA_HEAD_DIM // 2).astype(BF16)
    else:
        q_ref[...] = (a[:, :SWA_QW] * SWA_SCALE).astype(BF16)
        k_ref[...] = a[:, SWA_QW:SWA_QW + SWA_KW].astype(BF16)
        kv_ref[...] = a[:, SWA_QW + 2 * SWA_KW:]
    v_ref[...] = a[:, SWA_QW + SWA_KW:SWA_QW + 2 * SWA_KW].astype(BF16)


def _swa_proj(x, mods, norm1, w_ext, b_ext, tabs, rope):
    base = T_CTX // TM if rope else 0
    n_tiles = (T_LAT if rope else T_CTX) // TM
    n_rows = n_tiles * TM
    ncols = w_ext.shape[1]
    row = lambda i: (i, 0)
    in_specs = [pl.BlockSpec((TM, D_MODEL), lambda i: (i + base, 0)), _mod_spec(TM, base), _full((1, D_MODEL)),
                _full((D_MODEL, ncols)), _full((1, ncols))]
    args = [x, mods, norm1.reshape(1, -1), w_ext, b_ext]
    out_specs = [pl.BlockSpec((TM, SWA_QW), row), pl.BlockSpec((TM, SWA_KW), row), pl.BlockSpec((TM, SWA_KW), row)]
    out_shape = [jax.ShapeDtypeStruct((n_rows, SWA_QW), BF16), jax.ShapeDtypeStruct((n_rows, SWA_KW), BF16),
                 jax.ShapeDtypeStruct((n_rows, SWA_KW), BF16)]
    if rope:
        tab = pl.BlockSpec((TM, LANES), lambda i: (_pos_block(i + base, TM), 0))
        in_specs += [tab] * 4
        args += list(tabs)
    else:
        out_specs.append(pl.BlockSpec((TM, SWA_CW), row))
        out_shape.append(jax.ShapeDtypeStruct((n_rows, SWA_CW), F32))
    return pl.pallas_call(
        functools.partial(_swa_proj_kernel, rope=rope),
        grid=(n_tiles,),
        in_specs=in_specs, out_specs=out_specs, out_shape=out_shape,
        compiler_params=_params("arbitrary"),
        name="swa_proj_lat" if rope else "swa_proj_ctx",
    )(*args)


def _swa_ctx_kernel(q_ref, k_ref, v_ref, sink_ref, wo_ref, bo_ref, x_ref, mod_ref, o_ref, o_s):
    for n in range(SWA_KV_HEADS):
        kn = k_ref[:, n * LANES:(n + 1) * LANES]
        vn = v_ref[:, n * LANES:(n + 1) * LANES]
        for g in range(SWA_GROUP):
            j = n * SWA_GROUP + g
            sl = slice(j * LANES, (j + 1) * LANES)
            s = _dot_nt(q_ref[:, sl], kn)
            sink = sink_ref[j]
            m = jnp.maximum(jnp.max(s, axis=-1, keepdims=True), sink)
            p = jnp.exp(s - m)
            l = jnp.sum(p, axis=-1, keepdims=True) + jnp.exp(sink - m)
            o_s[:, sl] = (_dot(p.astype(BF16), vn) / l).astype(BF16)
    mix = _dot(o_s[...], wo_ref[...]) + bo_ref[...]
    o_ref[...] = x_ref[...] + mod_ref[0][2:3] * mix


def _swa_lat_kernel(q_ref, k_ref, v_ref, ck_ref, cv_ref, sink_ref, wo_ref, bo_ref, x_ref, mod_ref, o_ref, o_s):
    span = TQ + 2 * SWA_WINDOW
    start = pl.program_id(1) * TQ
    ks = pl.multiple_of(jnp.clip(start - SWA_WINDOW, 0, DEC_SEQ - span), SWA_WINDOW)
    qpos = start + lax.broadcasted_iota(jnp.int32, (TQ, span), 0)
    kpos = ks + lax.broadcasted_iota(jnp.int32, (TQ, span), 1)
    valid = jnp.abs(kpos - qpos) <= SWA_WINDOW
    for n in range(SWA_KV_HEADS):
        nl = slice(n * LANES, (n + 1) * LANES)
        kc = ck_ref[0, :, nl]
        vc = cv_ref[0, :, nl]
        kl = k_ref[pl.ds(ks, span), nl]
        vl = v_ref[pl.ds(ks, span), nl]
        for g in range(SWA_GROUP):
            j = n * SWA_GROUP + g
            sl = slice(j * LANES, (j + 1) * LANES)
            qj = q_ref[:, sl]
            s_c = _dot_nt(qj, kc)
            s_l = jnp.where(valid, _dot_nt(qj, kl), -jnp.inf)
            sink = sink_ref[j]
            m = jnp.maximum(jnp.maximum(jnp.max(s_c, axis=-1, keepdims=True),
                                        jnp.max(s_l, axis=-1, keepdims=True)), sink)
            p_c = jnp.exp(s_c - m)
            p_l = jnp.exp(s_l - m)
            l = (jnp.sum(p_c, axis=-1, keepdims=True) + jnp.sum(p_l, axis=-1, keepdims=True)
                 + jnp.exp(sink - m))
            o = _dot(p_c.astype(BF16), vc) + _dot(p_l.astype(BF16), vl)
            o_s[:, sl] = (o / l).astype(BF16)
    mix = _dot(o_s[...], wo_ref[...]) + bo_ref[...]
    o_ref[...] = x_ref[...] + mod_ref[0][2:3] * mix


def _swa_attn(x, mods, q, k, v, sink, wo_ext, bo, cache=None):
    smem = pl.BlockSpec(memory_space=pltpu.SMEM)
    if cache is None:
        nq, base = 1, 0
        tile = lambda b, qi: (b, 0)
        in_specs = [pl.BlockSpec((TQ, SWA_QW), tile), pl.BlockSpec((SEQ, SWA_KW), tile),
                    pl.BlockSpec((SEQ, SWA_KW), tile)]
        args = [q, k, v]
        kern, nb, name = _swa_ctx_kernel, BATCH, "swa_attn_ctx"
    else:
        nq, base = DEC_SEQ // TQ, T_CTX // TQ
        tile = lambda b, qi: (b * nq + qi, 0)
        seq = lambda b, qi: (b, 0)
        cspec = pl.BlockSpec((1, PAST_LEN, SWA_KW), lambda b, qi: (b, 0, 0))
        in_specs = [pl.BlockSpec((TQ, SWA_QW), tile), pl.BlockSpec((DEC_SEQ, SWA_KW), seq),
                    pl.BlockSpec((DEC_SEQ, SWA_KW), seq), cspec, cspec]
        args = [q, k, v, cache[0], cache[1]]
        kern, nb, name = _swa_lat_kernel, DEC_BATCH, "swa_attn_lat"
    xtile = lambda b, qi: (base + b * nq + qi, 0)
    in_specs += [smem, _full((SWA_QW, D_MODEL)), _full((1, D_MODEL)), pl.BlockSpec((TQ, D_MODEL), xtile),
                 pl.BlockSpec((1, 6, D_MODEL), lambda b, qi: (_cond_of_row((base + b * nq + qi) * TQ), 0, 0))]
    args += [sink, wo_ext, bo.reshape(1, -1), x, mods]
    return pl.pallas_call(
        kern,
        grid=(nb, nq),
        in_specs=in_specs,
        out_specs=pl.BlockSpec((TQ, D_MODEL), xtile),
        out_shape=jax.ShapeDtypeStruct((T_ALL, D_MODEL), F32),
        scratch_shapes=[pltpu.VMEM((TQ, SWA_QW), BF16)],
        input_output_aliases={len(args) - 2: 0},
        compiler_params=_params("arbitrary", "arbitrary"),
        name=name,
    )(*args)


CONV_EXT = CONV_TM + 2 * CONV_HALO


def _conv_kernel(xp_ref, x_ref, xn_ref, mod_ref, n1_ref, w1_ref, b1_ref, dw_ref, dwb_ref, lng_ref, lnb_ref,
                 w2_ref, b2_ref, o_ref, ext_s, y_s):
    i = pl.program_id(0)
    tiles_per_seq = DEC_SEQ // CONV_TM
    j = (i - T_CTX // CONV_TM) % tiles_per_seq
    latent = i >= T_CTX // CONV_TM
    left_ok = jnp.logical_and(latent, j > 0)
    right_ok = jnp.logical_and(latent, j < tiles_per_seq - 1)
    mod = mod_ref[0]

    def glu_of(xv):
        h = _modulated(xv, n1_ref[...], mod, 0).astype(BF16)
        u = _dot(h, w1_ref[...]) + b1_ref[...]
        return u[:, :D_MODEL] * _sigmoid(u[:, D_MODEL:])

    ext_s[CONV_HALO:CONV_HALO + CONV_TM, :] = glu_of(x_ref[...])
    ext_s[0:CONV_HALO, :] = jnp.where(left_ok, glu_of(xp_ref[...]), 0.0)
    ext_s[CONV_HALO + CONV_TM:, :] = jnp.where(right_ok, glu_of(xn_ref[...]), 0.0)

    rows = 128
    sub = 8
    first = CONV_HALO - CONV_PAD
    span = rows + (first + CONV_WIDTH - 1) // sub * sub
    for c in range(D_MODEL // LANES):
        cl = slice(c * LANES, (c + 1) * LANES)
        dwc = dw_ref[:, cl]
        for r in range(CONV_TM // rows):
            window = ext_s[r * rows:r * rows + span + sub, cl]
            acc = jnp.zeros((rows, LANES), F32)
            for res in range(sub):
                shifted = window if res == 0 else pltpu.roll(window, span + sub - res, 0)
                for q in range(span // sub - rows // sub + 1):
                    w = q * sub + res - first
                    if 0 <= w < CONV_WIDTH:
                        acc = acc + shifted[q * sub:q * sub + rows] * dwc[w:w + 1]
            y_s[r * rows:(r + 1) * rows, cl] = acc
    y = y_s[...] + dwb_ref[...]
    mu = jnp.mean(y, axis=-1, keepdims=True)
    yc = y - mu
    var = jnp.mean(yc * yc, axis=-1, keepdims=True)
    yn = yc * lax.rsqrt(var + NORM_EPS) * lng_ref[...] + lnb_ref[...]
    z = (yn * _sigmoid(yn)).astype(BF16)
    mix = _dot(z, w2_ref[...]) + b2_ref[...]
    o_ref[...] = x_ref[...] + mod[2:3] * mix


def _conv_layer(x, mods, norm1, w1, b1, dw, dwb, ln_g, ln_b, w2, b2):
    n_tiles = T_ALL // CONV_TM
    per = CONV_TM // CONV_HALO
    n_halo_blocks = T_ALL // CONV_HALO
    row = lambda i: (i, 0)
    vec = lambda v: v.reshape(1, -1)
    return pl.pallas_call(
        _conv_kernel,
        grid=(n_tiles,),
        in_specs=[pl.BlockSpec((CONV_HALO, D_MODEL), lambda i: (jnp.maximum(i * per - 1, 0), 0)),
                  pl.BlockSpec((CONV_TM, D_MODEL), row),
                  pl.BlockSpec((CONV_HALO, D_MODEL), lambda i: (jnp.minimum((i + 1) * per, n_halo_blocks - 1), 0)),
                  _mod_spec(CONV_TM), _full((1, D_MODEL)),
                  _full((D_MODEL, 2 * D_MODEL)), _full((1, 2 * D_MODEL)),
                  _full((CONV_WIDTH, D_MODEL)), _full((1, D_MODEL)), _full((1, D_MODEL)), _full((1, D_MODEL)),
                  _full((D_MODEL, D_MODEL)), _full((1, D_MODEL))],
        out_specs=pl.BlockSpec((CONV_TM, D_MODEL), row),
        out_shape=jax.ShapeDtypeStruct((T_ALL, D_MODEL), F32),
        scratch_shapes=[pltpu.VMEM((CONV_EXT, D_MODEL), F32), pltpu.VMEM((CONV_TM, D_MODEL), F32)],
        compiler_params=_params("arbitrary"),
        name="conv_module",
    )(x, x, x, mods, vec(norm1), w1.astype(BF16), vec(b1), dw, vec(dwb), vec(ln_g), vec(ln_b),
      w2.astype(BF16), vec(b2))


def _router_kernel(x_ref, mod_ref, n2_ref, rw_ref, rb_ref, h_ref, idx_ref, gate_ref, cnt_ref):
    h = _modulated(x_ref[...], n2_ref[...], mod_ref[0], 3)
    _store_token_tiles(h_ref, h)
    cur = _dot_nt(rw_ref[...], h.astype(BF16)) + rb_ref[...]
    e_iota = lax.broadcasted_iota(jnp.int32, cur.shape, 0)
    vals, idxs = [], []
    for _ in range(TOP_K):
        m = jnp.max(cur, axis=0, keepdims=True)
        idx = jnp.min(jnp.where(cur == m, e_iota, N_EXPERTS), axis=0, keepdims=True)
        vals.append(m)
        idxs.append(idx)
        cur = jnp.where(e_iota == idx, -jnp.inf, cur)
    ex = [jnp.exp(v - vals[0]) for v in vals]
    tot = ex[0] + ex[1] + ex[2] + ex[3]
    for k in range(TOP_K):
        idx_ref[k:k + 1, :] = idxs[k]
        gate_ref[k:k + 1, :] = ex[k] / tot

    picked = jnp.sum(jnp.where(cur == -jnp.inf, 1.0, 0.0), axis=1, keepdims=True)

    @pl.when(pl.program_id(0) == 0)
    def _():
        cnt_ref[...] = jnp.zeros_like(cnt_ref)

    cnt_ref[...] += picked


def _router(x, mods, norm2, router_w, router_b):
    n_tiles = T_ALL // TM
    row = lambda i: (i, 0)
    col = lambda i: (0, i)
    return pl.pallas_call(
        _router_kernel,
        grid=(n_tiles,),
        in_specs=[pl.BlockSpec((TM, D_MODEL), row), _mod_spec(TM), _full((1, D_MODEL)),
                  _full((N_EXPERTS, D_MODEL)), _full((N_EXPERTS, 1))],
        out_specs=[pl.BlockSpec((TM * TOK_ROWS, LANES), row), pl.BlockSpec((TOP_K, TM), col),
                   pl.BlockSpec((TOP_K, TM), col), pl.BlockSpec((N_EXPERTS, 1), lambda i: (0, 0))],
        out_shape=[jax.ShapeDtypeStruct((T_ALL * TOK_ROWS, LANES), F32),
                   jax.ShapeDtypeStruct((TOP_K, T_ALL), jnp.int32),
                   jax.ShapeDtypeStruct((TOP_K, T_ALL), F32),
                   jax.ShapeDtypeStruct((N_EXPERTS, 1), F32)],
        compiler_params=_params("arbitrary"),
        name="router",
    )(x, mods, norm2.reshape(1, -1), router_w.T.astype(BF16), router_b.reshape(-1, 1))


N_ASSIGN = TOP_K * T_ALL
RANK_CHUNK = 256
RANK_BLOCK = 2048


def _block_layout(counts):
    counts = counts.reshape(N_EXPERTS).astype(jnp.int32)
    padded = (counts + MOE_BM - 1) // MOE_BM * MOE_BM
    pad_end = jnp.cumsum(padded)
    pad_start = pad_end - padded
    starts = jnp.arange(MOE_BLOCKS + 1, dtype=jnp.int32) * MOE_BM
    e_ids = jnp.arange(N_EXPERTS, dtype=jnp.int32)
    used = padded > 0
    last_used = jnp.max(jnp.where(used, e_ids, 0))
    block_e = jnp.minimum(jnp.sum((pad_end[None, :] <= starts[:, None]).astype(jnp.int32), axis=1), last_used)
    n_used = (pad_end[-1] // MOE_BM).reshape(1)
    run = jnp.cumsum(used.astype(jnp.int32)) - 1
    later_used = jnp.where(jnp.logical_and(used[None, :], e_ids[None, :] > e_ids[:, None]), e_ids[None, :], N_EXPERTS)
    next_used = jnp.min(later_used, axis=1)
    next_used = jnp.where(next_used == N_EXPERTS, e_ids, next_used)
    block_plan = (block_e, next_used[block_e], run[block_e] & 1, n_used)
    return pad_start.astype(F32).reshape(N_EXPERTS, 1), block_plan


def _rank_kernel(idx_ref, ps_ref, u_ref, dest_ref, carry):
    @pl.when(pl.program_id(0) == 0)
    def _():
        carry[...] = ps_ref[...]

    e_iota = lax.broadcasted_iota(jnp.int32, (N_EXPERTS, RANK_CHUNK), 0)
    for c in range(RANK_BLOCK // RANK_CHUNK):
        sl = slice(c * RANK_CHUNK, (c + 1) * RANK_CHUNK)
        hit = e_iota == idx_ref[:, sl]
        pref = _dot(jnp.where(hit, 1.0, 0.0).astype(BF16), u_ref[...])
        base = carry[...]
        dest = jnp.sum(jnp.where(hit, pref + base, 0.0), axis=0, keepdims=True) - 1.0
        dest_ref[:, sl] = dest.astype(jnp.int32)
        carry[...] = base + pref[:, RANK_CHUNK - 1:RANK_CHUNK]


def _assignment_slots(idx_t, pad_start):
    tri = (jnp.arange(RANK_CHUNK)[:, None] <= jnp.arange(RANK_CHUNK)[None, :]).astype(BF16)
    blk = pl.BlockSpec((1, RANK_BLOCK), lambda i: (0, i))
    return pl.pallas_call(
        _rank_kernel,
        grid=(N_ASSIGN // RANK_BLOCK,),
        in_specs=[blk, _full((N_EXPERTS, 1)), _full((RANK_CHUNK, RANK_CHUNK))],
        out_specs=blk,
        out_shape=jax.ShapeDtypeStruct((1, N_ASSIGN), jnp.int32),
        scratch_shapes=[pltpu.VMEM((N_EXPERTS, 1), F32)],
        compiler_params=_params("arbitrary"),
        name="moe_rank",
    )(idx_t.reshape(1, N_ASSIGN), pad_start, tri)


INV_UNROLL = 32
INV_CHUNK = 4096
INV_STEPS = N_ASSIGN // INV_CHUNK
N_SLOTS = MOE_CAP + MOE_BM


def _inverse_kernel(dest_ref, pad_hbm, slot_ref, sem):
    i = pl.program_id(0)

    @pl.when(i == 0)
    def _():
        fill = pltpu.make_async_copy(pad_hbm, slot_ref, sem)
        fill.start()
        fill.wait()

    base = i * INV_CHUNK

    def place(j, carry):
        for u in range(INV_UNROLL):
            o = j * INV_UNROLL + u
            slot_ref[dest_ref[0, 0, o]] = base + o
        return carry

    lax.fori_loop(0, INV_CHUNK // INV_UNROLL, place, 0)


def _slot_assignments(dest):
    s = jnp.arange(N_SLOTS, dtype=jnp.int32)
    pad_ids = N_ASSIGN + (s & (MOE_BM - 1)) + jnp.where(s >= MOE_CAP, MOE_BM, 0)
    return pl.pallas_call(
        _inverse_kernel,
        grid=(INV_STEPS,),
        in_specs=[pl.BlockSpec((1, 1, INV_CHUNK), lambda i: (i, 0, 0), memory_space=pltpu.SMEM),
                  pl.BlockSpec(memory_space=pl.ANY)],
        out_specs=pl.BlockSpec(memory_space=pltpu.SMEM),
        out_shape=jax.ShapeDtypeStruct((N_SLOTS,), jnp.int32),
        scratch_shapes=[pltpu.SemaphoreType.DMA(())],
        compiler_params=_params("arbitrary"),
        name="moe_slots",
    )(dest.reshape(INV_STEPS, 1, INV_CHUNK), pad_ids)


YG_TOKENS = N_ASSIGN + 2 * MOE_BM
BLOCK_TILE_ROWS = MOE_BM * TOK_ROWS
MOE_CHUNK = 256


def _moe_kernel(be_ref, ne_ref, par_ref, nu_ref, sa_prev_ref, sa_ref, sa_next_ref, sa_next2_ref, h_hbm,
                wg_hbm, wu_hbm, wd_hbm, bg_ref, bu_ref, bd_ref, yg_hbm,
                xbuf, ybuf, x_s, act_s, wg_f, wu_f, wd_f, wgu_s, wd_s, gsem, ssem, wsem):
    i = pl.program_id(0)
    n_used = nu_ref[0]
    cur = i % 2
    nxt = 1 - cur
    gcur = lax.rem(i, 3)
    gnext = lax.rem(i + 1, 3)
    gnext2 = lax.rem(i + 2, 3)

    def weight_copies(e, slot):
        return [pltpu.make_async_copy(src.at[e], dst.at[slot], wsem.at[slot])
                for src, dst in ((wg_hbm, wg_f), (wu_hbm, wu_f), (wd_hbm, wd_f))]

    def start_gather(sa, buf):
        for r in range(MOE_BM):
            tok = sa[0, 0, r] & (T_ALL - 1)
            pltpu.make_async_copy(h_hbm.at[pl.ds(tok * TOK_ROWS, TOK_ROWS)],
                                  xbuf.at[buf, pl.ds(r * TOK_ROWS, TOK_ROWS)], gsem.at[buf]).start()

    def start_scatter(sa, buf):
        for r in range(MOE_BM):
            pltpu.make_async_copy(ybuf.at[buf, pl.ds(r * TOK_ROWS, TOK_ROWS)],
                                  yg_hbm.at[pl.ds(sa[0, 0, r] * TOK_ROWS, TOK_ROWS)], ssem.at[buf]).start()

    def wait_gather(buf):
        pltpu.make_async_copy(h_hbm.at[pl.ds(0, BLOCK_TILE_ROWS)], xbuf.at[buf], gsem.at[buf]).wait()

    def wait_scatter(buf):
        pltpu.make_async_copy(ybuf.at[buf], yg_hbm.at[pl.ds(0, BLOCK_TILE_ROWS)], ssem.at[buf]).wait()

    @pl.when(i == 0)
    def _():
        for cp in weight_copies(be_ref[0], 0):
            cp.start()
        start_gather(sa_ref, 0)
        start_gather(sa_next_ref, 1)
        ybuf[...] = jnp.zeros_like(ybuf)
        pltpu.make_async_copy(ybuf.at[0], yg_hbm.at[pl.ds(N_ASSIGN * TOK_ROWS, BLOCK_TILE_ROWS)], ssem.at[0]).start()

    @pl.when(i <= n_used)
    def _():
        e = be_ref[i]

        @pl.when(jnp.logical_or(i == 0, e != be_ref[jnp.maximum(i - 1, 0)]))
        def _():
            slot = par_ref[i]
            for cp in weight_copies(e, slot):
                cp.wait()
            for c in range(D_MODEL // MOE_CHUNK):
                cs = slice(c * MOE_CHUNK, (c + 1) * MOE_CHUNK)
                wgu_s[:, 2 * c * MOE_CHUNK:(2 * c + 1) * MOE_CHUNK] = wg_f[slot, :, cs].astype(BF16)
                wgu_s[:, (2 * c + 1) * MOE_CHUNK:(2 * c + 2) * MOE_CHUNK] = wu_f[slot, :, cs].astype(BF16)
            wd_s[...] = wd_f[slot].astype(BF16)

            @pl.when(ne_ref[i] != e)
            def _():
                for cp in weight_copies(ne_ref[i], 1 - slot):
                    cp.start()

        wait_gather(gcur)
        x_s[...] = _load_token_tiles(xbuf.at[gcur], MOE_BM).astype(BF16)
        start_gather(sa_next2_ref, gnext2)
        for c in range(D_MODEL // MOE_CHUNK):
            cs = slice(c * MOE_CHUNK, (c + 1) * MOE_CHUNK)
            gu = _dot(x_s[...], wgu_s[:, 2 * c * MOE_CHUNK:(2 * c + 2) * MOE_CHUNK])
            g = jnp.minimum(gu[:, :MOE_CHUNK] + bg_ref[0, :, cs], SWIGLU_LIMIT)
            u = jnp.clip(gu[:, MOE_CHUNK:] + bu_ref[0, :, cs], -SWIGLU_LIMIT, SWIGLU_LIMIT)
            act_s[:, cs] = ((u + 1.0) * (g * _sigmoid(SWIGLU_ALPHA * g))).astype(BF16)
        wait_scatter(cur)
        start_scatter(sa_prev_ref, nxt)
        half = D_MODEL // 2
        for c in range(2):
            y = _dot(act_s[...], wd_s[:, c * half:(c + 1) * half]) + bd_ref[0, :, c * half:(c + 1) * half]
            for j in range(TOK_ROWS // 2):
                jj = c * (TOK_ROWS // 2) + j
                ybuf.at[cur][pl.ds(jj, MOE_BM, stride=TOK_ROWS), :] = y[:, j * LANES:(j + 1) * LANES]

    @pl.when(i == n_used)
    def _():
        wait_gather(gnext)
        wait_gather(gnext2)
        wait_scatter(nxt)


def _moe_experts(h, slots, block_plan, wg, bg, wu, bu, wd, bd):
    bspec = pl.BlockSpec((1, 1, D_MODEL), lambda i, be, *_: (be[i], 0, 0))
    sspec = lambda f: pl.BlockSpec((1, 1, MOE_BM), lambda i, *_: (f(i), 0, 0), memory_space=pltpu.SMEM)
    hbm = pl.BlockSpec(memory_space=pl.ANY)
    b3 = lambda b: b.reshape(N_EXPERTS, 1, D_MODEL)
    slots3 = slots.reshape(MOE_BLOCKS + 1, 1, MOE_BM)
    wbuf = pltpu.VMEM((2, D_MODEL, D_MODEL), F32)
    return pl.pallas_call(
        _moe_kernel,
        grid_spec=pltpu.PrefetchScalarGridSpec(
            num_scalar_prefetch=4,
            grid=(MOE_BLOCKS + 1,),
            in_specs=[sspec(lambda i: jnp.where(i == 0, MOE_BLOCKS, i - 1)), sspec(lambda i: i),
                      sspec(lambda i: jnp.minimum(i + 1, MOE_BLOCKS)), sspec(lambda i: jnp.minimum(i + 2, MOE_BLOCKS)),
                      hbm, hbm, hbm, hbm, bspec, bspec, bspec],
            out_specs=hbm,
            scratch_shapes=[pltpu.VMEM((3, BLOCK_TILE_ROWS, LANES), F32), pltpu.VMEM((2, BLOCK_TILE_ROWS, LANES), F32),
                            pltpu.VMEM((MOE_BM, D_MODEL), BF16), pltpu.VMEM((MOE_BM, D_MODEL), BF16),
                            wbuf, wbuf, wbuf,
                            pltpu.VMEM((D_MODEL, 2 * D_MODEL), BF16), pltpu.VMEM((D_MODEL, D_MODEL), BF16),
                            pltpu.SemaphoreType.DMA((3,)), pltpu.SemaphoreType.DMA((2,)),
                            pltpu.SemaphoreType.DMA((2,))]),
        out_shape=jax.ShapeDtypeStruct((YG_TOKENS * TOK_ROWS, LANES), F32),
        compiler_params=_params("arbitrary"),
        name="moe_experts",
    )(*block_plan, slots3, slots3, slots3, slots3, h, wg, wu, wd, b3(bg), b3(bu), b3(bd))


def _combine_kernel(x_ref, y0_ref, y1_ref, y2_ref, y3_ref, gate_ref, mod_ref, o_ref):
    gate = gate_ref[...]
    n = x_ref.shape[0]
    g2 = mod_ref[0][5:6]
    for j in range(TOK_ROWS):
        rows = pl.ds(j, n, stride=TOK_ROWS)
        sl = slice(j * LANES, (j + 1) * LANES)
        acc = y0_ref[rows, :] * gate[:, 0:1]
        for k, y_ref in enumerate((y1_ref, y2_ref, y3_ref), start=1):
            acc = acc + y_ref[rows, :] * gate[:, k:k + 1]
        o_ref[:, sl] = x_ref[:, sl] + g2[:, sl] * acc


def _combine(x, yg, gates, mods):
    row = lambda i: (i, 0)
    n_tiles = T_ALL // TM
    ysp = lambda k: pl.BlockSpec((TM * TOK_ROWS, LANES), lambda i: (k * n_tiles + i, 0))
    return pl.pallas_call(
        _combine_kernel,
        grid=(n_tiles,),
        in_specs=[pl.BlockSpec((TM, D_MODEL), row), ysp(0), ysp(1), ysp(2), ysp(3),
                  pl.BlockSpec((TM, TOP_K), row), _mod_spec(TM)],
        out_specs=pl.BlockSpec((TM, D_MODEL), row),
        out_shape=jax.ShapeDtypeStruct((T_ALL, D_MODEL), F32),
        input_output_aliases={0: 0},
        compiler_params=_params("arbitrary"),
        name="moe_combine",
    )(x, yg, yg, yg, yg, gates, mods)


def _moe_layer(x, mods, norm2, router_w, router_b, wg, bg, wu, bu, wd, bd):
    h, idx_t, gate_t, counts = _router(x, mods, norm2, router_w, router_b)
    pad_start, block_plan = _block_layout(counts)
    slots = _slot_assignments(_assignment_slots(idx_t, pad_start))
    yg = _moe_experts(h, slots, block_plan, wg, bg, wu, bu, wd, bd)
    return _combine(x, yg, gate_t.T, mods)


def _final_norm_kernel(x_ref, g_ref, o_ref):
    o_ref[...] = _rms(x_ref[...], g_ref[...])


def _final_norm(x, g, base_rows, n_rows):
    base = base_rows // TM
    return pl.pallas_call(
        _final_norm_kernel,
        grid=(n_rows // TM,),
        in_specs=[pl.BlockSpec((TM, D_MODEL), lambda i: (i + base, 0)), _full((1, D_MODEL))],
        out_specs=pl.BlockSpec((TM, D_MODEL), lambda i: (i, 0)),
        out_shape=jax.ShapeDtypeStruct((n_rows, D_MODEL), F32),
        compiler_params=_params("arbitrary"),
        name="final_norm",
    )(x, g.reshape(1, -1))


def _pad_lanes(w, lead_pad, width):
    tail = width - lead_pad - w.shape[-1]
    cfg = [(0, 0)] * (w.ndim - 1) + [(lead_pad, tail)]
    return jnp.pad(w, cfg)


def _mla_weights(wa, wuq, wukv, wo):
    krope_w = wa[:, MLA_Q_RANK + MLA_KV_RANK:]
    wa_ext = jnp.concatenate([wa[:, :MLA_Q_RANK + MLA_KV_RANK],
                              _pad_lanes(krope_w, MLA_ROPE_OFF, LANES),
                              _pad_lanes(krope_w, 0, LANES)], axis=1).astype(BF16)
    q3 = wuq.reshape(MLA_Q_RANK, MLA_HEADS, MLA_NOPE + MLA_ROPE)
    wuq_ext = jnp.concatenate([q3[..., :MLA_NOPE], _pad_lanes(q3[..., MLA_NOPE:], MLA_ROPE // 2, LANES - MLA_NOPE)],
                              axis=-1).reshape(MLA_Q_RANK, MLA_HEADS * LANES).astype(BF16)
    kv3 = wukv.reshape(MLA_KV_RANK, MLA_HEADS, MLA_NOPE + MLA_V)
    wk_ext = _pad_lanes(kv3[..., :MLA_NOPE], 0, LANES).reshape(MLA_KV_RANK, MLA_HEADS * LANES).astype(BF16)
    wv = kv3[..., MLA_NOPE:].reshape(MLA_KV_RANK, MLA_HEADS * MLA_V).astype(BF16)
    return wa_ext, wuq_ext, wk_ext, wv, wo.astype(BF16)


def _swa_weights(wqkv, bqkv, wo):
    nq = SWA_HEADS * SWA_HEAD_DIM
    nkv = SWA_KV_HEADS * SWA_HEAD_DIM

    def slabs(w, heads):
        lead = w.shape[:-1]
        return _pad_lanes(w.reshape(lead + (heads, SWA_HEAD_DIM)), 0, LANES).reshape(lead + (heads * LANES,))

    def ext(w, with_compact):
        parts = [slabs(w[..., :nq], SWA_HEADS), slabs(w[..., nq:nq + nkv], SWA_KV_HEADS),
                 slabs(w[..., nq + nkv:], SWA_KV_HEADS)]
        if with_compact:
            parts.append(w[..., nq:])
        return jnp.concatenate(parts, axis=-1)

    b2 = bqkv.reshape(1, -1)
    wo3 = wo.reshape(SWA_HEADS, SWA_HEAD_DIM, D_MODEL)
    wo_ext = jnp.pad(wo3, ((0, 0), (0, LANES - SWA_HEAD_DIM), (0, 0))).reshape(SWA_QW, D_MODEL).astype(BF16)
    return (ext(wqkv, True).astype(BF16), ext(b2, True), ext(wqkv, False).astype(BF16), ext(b2, False), wo_ext)


def kernel(x_prompt, x_sample, cache_l0_ckv, cache_l0_krope, cache_l1_k, cache_l1_v, cache_l3_ckv, cache_l3_krope, c, c_ctx, l0_mla_wa, l0_mla_q_norm, l0_mla_wuq, l0_mla_kv_norm, l0_mla_wukv, l0_mla_wo, l0_mod_w, l0_mod_b, l0_norm1, l0_norm2, l0_router_w, l0_router_b, l0_moe_wg, l0_moe_bg, l0_moe_wu, l0_moe_bu, l0_moe_wd, l0_moe_bd, l1_swa_wqkv, l1_swa_bqkv, l1_swa_sink, l1_swa_wo, l1_swa_bo, l1_mod_w, l1_mod_b, l1_norm1, l1_norm2, l1_router_w, l1_router_b, l1_moe_wg, l1_moe_bg, l1_moe_wu, l1_moe_bu, l1_moe_wd, l1_moe_bd, l2_conv_w1, l2_conv_b1, l2_conv_dw, l2_conv_dwb, l2_conv_ln_g, l2_conv_ln_b, l2_conv_w2, l2_conv_b2, l2_mod_w, l2_mod_b, l2_norm1, l2_norm2, l2_router_w, l2_router_b, l2_moe_wg, l2_moe_bg, l2_moe_wu, l2_moe_bu, l2_moe_wd, l2_moe_bd, l3_mla_wa, l3_mla_q_norm, l3_mla_wuq, l3_mla_kv_norm, l3_mla_wukv, l3_mla_wo, l3_mod_w, l3_mod_b, l3_norm1, l3_norm2, l3_router_w, l3_router_b, l3_moe_wg, l3_moe_bg, l3_moe_wu, l3_moe_bu, l3_moe_wd, l3_moe_bd, final_norm):
    x = jnp.concatenate([x_prompt.reshape(T_CTX, D_MODEL), x_sample.reshape(T_LAT, D_MODEL)], axis=0)
    cond = jnp.concatenate([c, c_ctx[None, :], jnp.zeros((N_COND - DEC_BATCH - 1, D_MODEL), F32)], axis=0)

    mla_c, mla_s = _rope_tables(MLA_ROPE, MLA_ROPE_LEAD, TM)
    mla_tabs = (mla_c * MLA_SCALE, mla_s * MLA_SCALE, mla_c, mla_s)
    swa_c, swa_s = _rope_tables(SWA_HEAD_DIM, 0, TM)
    swa_tabs = (swa_c * SWA_SCALE, swa_s * SWA_SCALE, swa_c, swa_s)

    def mla_layer(x, mods, norm1, wa, q_norm, wuq, kv_norm, wukv, wo, cache_ckv, cache_krope):
        wa_ext, wuq_ext, wk_ext, wv_ext, wo_ext = _mla_weights(wa, wuq, wukv, wo)
        q, ckv, kp, kst = _mla_proj(x, mods, norm1, wa_ext, q_norm, wuq_ext, kv_norm, mla_tabs)
        x = _mla_attn(x, mods, q, ckv, kp, wk_ext, wv_ext, wo_ext)
        cache = (cache_ckv, _pad_lanes(cache_krope, MLA_ROPE_OFF, LANES).astype(BF16))
        x = _mla_attn(x, mods, q, ckv, kp, wk_ext, wv_ext, wo_ext, cache)
        return x, (ckv[:T_CTX].reshape(BATCH, SEQ, MLA_KV_RANK), kst[:T_CTX].reshape(BATCH, SEQ, MLA_ROPE))

    def moe(x, mods, norm2, rw, rb, wg, bg, wu, bu, wd, bd):
        return _moe_layer(x, mods, norm2, rw, rb, wg, bg, wu, bu, wd, bd)

    mods = _modulation(cond, l0_mod_w, l0_mod_b)
    x, (st0_ckv, st0_krope) = mla_layer(x, mods, l0_norm1, l0_mla_wa, l0_mla_q_norm, l0_mla_wuq, l0_mla_kv_norm,
                                        l0_mla_wukv, l0_mla_wo, cache_l0_ckv, cache_l0_krope)
    x = moe(x, mods, l0_norm2, l0_router_w, l0_router_b, l0_moe_wg, l0_moe_bg, l0_moe_wu, l0_moe_bu,
            l0_moe_wd, l0_moe_bd)

    mods = _modulation(cond, l1_mod_w, l1_mod_b)
    w_ctx, b_ctx, w_lat, b_lat, swa_wo_ext = _swa_weights(l1_swa_wqkv, l1_swa_bqkv, l1_swa_wo)
    q_c, k_c, v_c, kv_c = _swa_proj(x, mods, l1_norm1, w_ctx, b_ctx, None, rope=False)
    q_l, k_l, v_l = _swa_proj(x, mods, l1_norm1, w_lat, b_lat, swa_tabs, rope=True)
    x = _swa_attn(x, mods, q_c, k_c, v_c, l1_swa_sink, swa_wo_ext, l1_swa_bo)
    nkv = SWA_KV_HEADS * SWA_HEAD_DIM
    cache_k = _pad_lanes(cache_l1_k, 0, LANES).reshape(DEC_BATCH, PAST_LEN, SWA_KW).astype(BF16)
    cache_v = _pad_lanes(cache_l1_v, 0, LANES).reshape(DEC_BATCH, PAST_LEN, SWA_KW).astype(BF16)
    x = _swa_attn(x, mods, q_l, k_l, v_l, l1_swa_sink, swa_wo_ext, l1_swa_bo, (cache_k, cache_v))
    st1_k = kv_c[:, :nkv].reshape(BATCH, SEQ, SWA_KV_HEADS, SWA_HEAD_DIM)
    st1_v = kv_c[:, nkv:].reshape(BATCH, SEQ, SWA_KV_HEADS, SWA_HEAD_DIM)
    x = moe(x, mods, l1_norm2, l1_router_w, l1_router_b, l1_moe_wg, l1_moe_bg, l1_moe_wu, l1_moe_bu,
            l1_moe_wd, l1_moe_bd)

    mods = _modulation(cond, l2_mod_w, l2_mod_b)
    x = _conv_layer(x, mods, l2_norm1, l2_conv_w1, l2_conv_b1, l2_conv_dw, l2_conv_dwb, l2_conv_ln_g,
                    l2_conv_ln_b, l2_conv_w2, l2_conv_b2)
    x = moe(x, mods, l2_norm2, l2_router_w, l2_router_b, l2_moe_wg, l2_moe_bg, l2_moe_wu, l2_moe_bu,
            l2_moe_wd, l2_moe_bd)

    mods = _modulation(cond, l3_mod_w, l3_mod_b)
    x, (st3_ckv, st3_krope) = mla_layer(x, mods, l3_norm1, l3_mla_wa, l3_mla_q_norm, l3_mla_wuq, l3_mla_kv_norm,
                                        l3_mla_wukv, l3_mla_wo, cache_l3_ckv, cache_l3_krope)
    x = moe(x, mods, l3_norm2, l3_router_w, l3_router_b, l3_moe_wg, l3_moe_bg, l3_moe_wu, l3_moe_bu,
            l3_moe_wd, l3_moe_bd)

    y_prompt = _final_norm(x, final_norm, 0, T_CTX).reshape(BATCH, SEQ, D_MODEL)
    y_sample = _final_norm(x, final_norm, T_CTX, T_LAT).reshape(DEC_BATCH, DEC_SEQ, D_MODEL)
    return (y_prompt, y_sample, st0_ckv, st0_krope, st1_k, st1_v, st3_ckv, st3_krope)
```

```python
import functools

import jax
import jax.numpy as jnp
import numpy as np
from jax import lax
from jax.experimental import pallas as pl
from jax.experimental.pallas import tpu as pltpu

F32 = jnp.float32
BF16 = jnp.bfloat16

D_MODEL = 1024
BATCH = 32
SEQ = 256
DEPTH = 4
DEC_BATCH = 4
DEC_SEQ = 2048
PAST_LEN = 256
GRID_W = 64
NORM_EPS = 1e-6
ROPE_THETA = 10000.0

MLA_HEADS = 16
MLA_NOPE = 64
MLA_ROPE = 32
MLA_V = 64
MLA_Q_RANK = 384
MLA_KV_RANK = 256
MLA_SCALE = (MLA_NOPE + MLA_ROPE) ** -0.5

SWA_HEADS = 16
SWA_KV_HEADS = 4
SWA_GROUP = 4
SWA_HEAD_DIM = 64
SWA_WINDOW = 128
SWA_SCALE = SWA_HEAD_DIM ** -0.5

CONV_WIDTH = 31
CONV_PAD = CONV_WIDTH // 2

N_EXPERTS = 32
TOP_K = 4
SWIGLU_ALPHA = 1.702
SWIGLU_LIMIT = 7.0

LANES = 128
T_CTX = BATCH * SEQ
T_LAT = DEC_BATCH * DEC_SEQ
T_ALL = T_CTX + T_LAT
CTX_COND = DEC_BATCH
N_COND = 8

TM = 512
TQ = 256
CONV_TM = 256
CONV_HALO = 16
MOE_BM = 256
MOE_BLOCKS = (T_ALL * TOP_K + N_EXPERTS * (MOE_BM - 1) + MOE_BM - 1) // MOE_BM
MOE_CAP = MOE_BLOCKS * MOE_BM
VMEM_LIMIT = 56 * 1024 * 1024


def _params(*sem):
    return pltpu.CompilerParams(dimension_semantics=sem, vmem_limit_bytes=VMEM_LIMIT)


def _cond_of_row(row0):
    return jnp.where(row0 < T_CTX, CTX_COND, (row0 - T_CTX) // DEC_SEQ)


def _pos_block(i, tm):
    row0 = i * tm
    return jnp.where(row0 < T_CTX, DEC_SEQ // tm, ((row0 - T_CTX) % DEC_SEQ) // tm)


def _rms(x, g):
    ms = jnp.mean(x * x, axis=-1, keepdims=True)
    return x * lax.rsqrt(ms + NORM_EPS) * g


def _modulated(x, g, mod, j):
    return _rms(x, g) * (1.0 + mod[j + 1:j + 2]) + mod[j:j + 1]


def _sigmoid(x):
    return 1.0 / (1.0 + jnp.exp(-x))


def _dot(a, b):
    return jnp.dot(a, b, preferred_element_type=F32)


def _dot_nt(a, b):
    return lax.dot_general(a, b, (((1,), (1,)), ((), ())), preferred_element_type=F32)


TOK_ROWS = D_MODEL // LANES


def _store_token_tiles(ref, v):
    n = v.shape[0]
    for j in range(TOK_ROWS):
        ref[pl.ds(j, n, stride=TOK_ROWS), :] = v[:, j * LANES:(j + 1) * LANES]


def _load_token_tiles(ref, n):
    return jnp.concatenate([ref[pl.ds(j, n, stride=TOK_ROWS), :] for j in range(TOK_ROWS)], axis=-1)


def _rope_slab(a, c, s, shift):
    return a * c + (pltpu.roll(a, shift, 1) - pltpu.roll(a, LANES - shift, 1)) * s


def _mod_kernel(c_ref, w_ref, b_ref, o_ref):
    c = c_ref[...]
    s = (c * _sigmoid(c)).astype(BF16)
    o_ref[...] = _dot(s, w_ref[...].astype(BF16)) + b_ref[...]


def _modulation(cond, w, b):
    tn = 1536
    out = pl.pallas_call(
        _mod_kernel,
        grid=(6 * D_MODEL // tn,),
        in_specs=[pl.BlockSpec((N_COND, D_MODEL), lambda j: (0, 0)),
                  pl.BlockSpec((D_MODEL, tn), lambda j: (0, j)),
                  pl.BlockSpec((1, tn), lambda j: (0, j))],
        out_specs=pl.BlockSpec((N_COND, tn), lambda j: (0, j)),
        out_shape=jax.ShapeDtypeStruct((N_COND, 6 * D_MODEL), F32),
        compiler_params=_params("arbitrary"),
        name="modulation",
    )(cond, w, b.reshape(1, -1))
    return out.reshape(N_COND, 6, D_MODEL)


def _mod_spec(tm, base=0):
    return pl.BlockSpec((1, 6, D_MODEL), lambda i, *_: (_cond_of_row((i + base) * tm), 0, 0))


def _full(shape):
    nd = len(shape)
    return pl.BlockSpec(shape, lambda *_: (0,) * nd, pipeline_mode=pl.Buffered(1))


def _rope_cos_sin(n_tok, rot_dim):
    rows = n_tok // GRID_W
    row = jnp.repeat(jnp.arange(rows, dtype=F32), GRID_W)
    col = jnp.tile(jnp.arange(GRID_W, dtype=F32), rows)
    n_freq = rot_dim // 4
    inv_freq = ROPE_THETA ** (-jnp.arange(n_freq, dtype=F32) / n_freq)
    ang = jnp.concatenate([row[:, None] * inv_freq, col[:, None] * inv_freq], axis=-1)
    return jnp.cos(ang), jnp.sin(ang)


def _rope_tables(rot_dim, lead, tm):
    cos, sin = _rope_cos_sin(DEC_SEQ, rot_dim)
    gap = rot_dim // 2 if lead else 0
    tail = LANES - lead - gap - rot_dim
    n = DEC_SEQ

    def slab(lead_val, rot):
        return jnp.concatenate([jnp.full((rot.shape[0], lead), lead_val, F32),
                                jnp.zeros((rot.shape[0], gap), F32), rot,
                                jnp.zeros((rot.shape[0], tail), F32)], axis=-1)

    c = slab(1.0, jnp.concatenate([cos, cos], axis=-1))
    s = slab(0.0, jnp.concatenate([sin, sin], axis=-1))
    c_id = slab(1.0, jnp.ones((tm, rot_dim), F32))
    s_id = jnp.zeros((tm, LANES), F32)
    del n
    return jnp.concatenate([c, c_id], axis=0), jnp.concatenate([s, s_id], axis=0)


MLA_A_COLS = MLA_Q_RANK + MLA_KV_RANK + 2 * LANES
MLA_ROPE_LEAD = MLA_NOPE
MLA_ROPE_OFF = MLA_NOPE + MLA_ROPE // 2


def _mla_proj_kernel(x_ref, mod_ref, n1_ref, wa_ref, qn_ref, wuq_ref, kvn_ref,
                     cq_ref, sq_ref, ck_ref, sk_ref,
                     q_ref, ckv_ref, kp_ref, kst_ref):
    h = _modulated(x_ref[...], n1_ref[...], mod_ref[0], 0).astype(BF16)
    a = _dot(h, wa_ref[...])
    cq = _rms(a[:, :MLA_Q_RANK], qn_ref[...]).astype(BF16)
    ckv_ref[...] = _rms(a[:, MLA_Q_RANK:MLA_Q_RANK + MLA_KV_RANK], kvn_ref[...])
    k0 = MLA_Q_RANK + MLA_KV_RANK
    kp = a[:, k0:k0 + LANES]
    kst_ref[...] = a[:, k0 + LANES:k0 + LANES + MLA_ROPE]
    qa = _dot(cq, wuq_ref[...])
    cq_t = cq_ref[...]
    is_ctx = pl.program_id(0) < T_CTX // TM

    @pl.when(is_ctx)
    def _():
        kp_ref[...] = (kp * ck_ref[...]).astype(BF16)
        for hd in range(MLA_HEADS):
            sl = slice(hd * LANES, (hd + 1) * LANES)
            q_ref[:, sl] = (qa[:, sl] * cq_t).astype(BF16)

    @pl.when(jnp.logical_not(is_ctx))
    def _():
        sq_t = sq_ref[...]
        kp_ref[...] = _rope_slab(kp, ck_ref[...], sk_ref[...], MLA_ROPE // 2).astype(BF16)
        for hd in range(MLA_HEADS):
            sl = slice(hd * LANES, (hd + 1) * LANES)
            q_ref[:, sl] = _rope_slab(qa[:, sl], cq_t, sq_t, MLA_ROPE // 2).astype(BF16)


def _mla_proj(x, mods, norm1, wa_ext, q_norm, wuq_ext, kv_norm, tabs):
    cq, sq, ck, sk = tabs
    n_tiles = T_ALL // TM
    row = lambda i: (i, 0)
    tab = pl.BlockSpec((TM, LANES), lambda i: (_pos_block(i, TM), 0))
    return pl.pallas_call(
        _mla_proj_kernel,
        grid=(n_tiles,),
        in_specs=[pl.BlockSpec((TM, D_MODEL), row), _mod_spec(TM), _full((1, D_MODEL)),
                  _full((D_MODEL, MLA_A_COLS)), _full((1, MLA_Q_RANK)),
                  _full((MLA_Q_RANK, MLA_HEADS * LANES)), _full((1, MLA_KV_RANK)),
                  tab, tab, tab, tab],
        out_specs=[pl.BlockSpec((TM, MLA_HEADS * LANES), row), pl.BlockSpec((TM, MLA_KV_RANK), row),
                   pl.BlockSpec((TM, LANES), row), pl.BlockSpec((TM, MLA_ROPE), row)],
        out_shape=[jax.ShapeDtypeStruct((T_ALL, MLA_HEADS * LANES), BF16),
                   jax.ShapeDtypeStruct((T_ALL, MLA_KV_RANK), F32),
                   jax.ShapeDtypeStruct((T_ALL, LANES), BF16),
                   jax.ShapeDtypeStruct((T_ALL, MLA_ROPE), F32)],
        compiler_params=_params("arbitrary"),
        name="mla_proj",
    )(x, mods, norm1.reshape(1, -1), wa_ext, q_norm.reshape(1, -1), wuq_ext, kv_norm.reshape(1, -1),
      cq, sq, ck, sk)


def _mla_attn_kernel(*refs, n_lat, n_cache):
    if n_cache:
        (q_ref, ckv_ref, kp_ref, cckv_ref, ckp_ref, wk_ref, wv_ref, wo_ref, x_ref, mod_ref,
         o_ref, k_s, v_s, o_s) = refs
    else:
        (q_ref, ckv_ref, kp_ref, wk_ref, wv_ref, wo_ref, x_ref, mod_ref, o_ref, k_s, v_s, o_s) = refs

    def expand(ckv_of, kp_of, off, rows):
        rc = min(rows, 256)
        for r0 in range(0, rows, rc):
            c = ckv_of(r0, rc).astype(BF16)
            kp = kp_of(r0, rc).astype(F32)
            kn = _dot(c, wk_ref[...])
            v_s[off + r0:off + r0 + rc, :] = _dot(c, wv_ref[...]).astype(BF16)
            for hd in range(MLA_HEADS):
                sl = slice(hd * LANES, (hd + 1) * LANES)
                k_s[off + r0:off + r0 + rc, sl] = (kn[:, sl] + kp).astype(BF16)

    @pl.when(pl.program_id(1) == 0)
    def _():
        if n_cache:
            expand(lambda r, n: cckv_ref[0, r:r + n, :], lambda r, n: ckp_ref[0, r:r + n, :], 0, n_cache)
        expand(lambda r, n: ckv_ref[r:r + n, :], lambda r, n: kp_ref[r:r + n, :], n_cache, n_lat)

    def head(hd, v_pair):
        sl = slice(hd * LANES, (hd + 1) * LANES)
        s = _dot_nt(q_ref[:, sl], k_s[:, sl])
        m = jnp.max(s, axis=-1, keepdims=True)
        p = jnp.exp(s - m)
        l = jnp.sum(p, axis=-1, keepdims=True)
        return _dot(p.astype(BF16), v_pair) / l

    low_half = lax.broadcasted_iota(jnp.int32, (q_ref.shape[0], LANES), 1) < MLA_V
    for hp in range(MLA_HEADS // 2):
        sl = slice(hp * LANES, (hp + 1) * LANES)
        v_pair = v_s[:, sl]
        o_s[:, sl] = jnp.where(low_half, head(2 * hp, v_pair), head(2 * hp + 1, v_pair)).astype(BF16)
    mix = _dot(o_s[...], wo_ref[...])
    o_ref[...] = x_ref[...] + mod_ref[0][2:3] * mix


def _mla_attn(x, mods, q, ckv, kp, wk_ext, wv_ext, wo_ext, cache=None):
    hw = MLA_HEADS * LANES
    if cache is None:
        n_lat, n_cache, nb, nq, base = SEQ, 0, BATCH, SEQ // TQ, 0
    else:
        n_lat, n_cache, nb, nq, base = DEC_SEQ, PAST_LEN, DEC_BATCH, DEC_SEQ // TQ, T_CTX // TQ
    seq_base = base * TQ // n_lat
    tile = lambda b, qi: (base + b * nq + qi, 0)
    seq = lambda b, qi: (seq_base + b, 0)
    seq_mode = dict(pipeline_mode=pl.Buffered(1)) if cache is not None else {}
    in_specs = [pl.BlockSpec((TQ, hw), tile),
                pl.BlockSpec((n_lat, MLA_KV_RANK), seq, **seq_mode),
                pl.BlockSpec((n_lat, LANES), seq, **seq_mode)]
    args = [q, ckv, kp]
    if cache is not None:
        in_specs += [pl.BlockSpec((1, n_cache, MLA_KV_RANK), lambda b, qi: (b, 0, 0)),
                     pl.BlockSpec((1, n_cache, LANES), lambda b, qi: (b, 0, 0))]
        args += list(cache)
    vw = MLA_HEADS * MLA_V
    in_specs += [_full((MLA_KV_RANK, hw)), _full((MLA_KV_RANK, vw)), _full((vw, D_MODEL)),
                 pl.BlockSpec((TQ, D_MODEL), tile),
                 pl.BlockSpec((1, 6, D_MODEL), lambda b, qi: (_cond_of_row((base + b * nq + qi) * TQ), 0, 0))]
    args += [wk_ext, wv_ext, wo_ext, x, mods]
    sk = n_cache + n_lat
    return pl.pallas_call(
        functools.partial(_mla_attn_kernel, n_lat=n_lat, n_cache=n_cache),
        grid=(nb, nq),
        in_specs=in_specs,
        out_specs=pl.BlockSpec((TQ, D_MODEL), tile),
        out_shape=jax.ShapeDtypeStruct((T_ALL, D_MODEL), F32),
        scratch_shapes=[pltpu.VMEM((sk, hw), BF16), pltpu.VMEM((sk, vw), BF16), pltpu.VMEM((TQ, vw), BF16)],
        input_output_aliases={len(args) - 2: 0},
        compiler_params=_params("arbitrary", "arbitrary"),
        name="mla_attn_lat" if cache is not None else "mla_attn_ctx",
    )(*args)


SWA_QW = SWA_HEADS * LANES
SWA_KW = SWA_KV_HEADS * LANES
SWA_CW = 2 * SWA_KV_HEADS * SWA_HEAD_DIM


def _swa_proj_kernel(*refs, rope):
    if rope:
        (x_ref, mod_ref, n1_ref, w_ref, b_ref, cq_ref, sq_ref, ck_ref, sk_ref, q_ref, k_ref, v_ref) = refs
    else:
        (x_ref, mod_ref, n1_ref, w_ref, b_ref, q_ref, k_ref, v_ref, kv_ref) = refs
    h = _modulated(x_ref[...], n1_ref[...], mod_ref[0], 0).astype(BF16)
    a = _dot(h, w_ref[...]) + b_ref[...]
    if rope:
        cq, sq, ck, sk = cq_ref[...], sq_ref[...], ck_ref[...], sk_ref[...]
        for hd in range(SWA_HEADS):
            sl = slice(hd * LANES, (hd + 1) * LANES)
            q_ref[:, sl] = _rope_slab(a[:, sl], cq, sq, SWA_HEAD_DIM // 2).astype(BF16)
        for hd in range(SWA_KV_HEADS):
            sl = slice(hd * LANES, (hd + 1) * LANES)
            k_ref[:, sl] = _rope_slab(a[:, SWA_QW + hd * LANES:SWA_QW + (hd + 1) * LANES], ck, sk,
                                      SWA_HEAD_DIM // 2).astype(BF16)
    else:
        q_ref[...] = (a[:, :SWA_QW] * SWA_SCALE).astype(BF16)
        k_ref[...] = a[:, SWA_QW:SWA_QW + SWA_KW].astype(BF16)
        kv_ref[...] = a[:, SWA_QW + 2 * SWA_KW:]
    v_ref[...] = a[:, SWA_QW + SWA_KW:SWA_QW + 2 * SWA_KW].astype(BF16)


def _swa_proj(x, mods, norm1, w_ext, b_ext, tabs, rope):
    base = T_CTX // TM if rope else 0
    n_tiles = (T_LAT if rope else T_CTX) // TM
    n_rows = n_tiles * TM
    ncols = w_ext.shape[1]
    row = lambda i: (i, 0)
    in_specs = [pl.BlockSpec((TM, D_MODEL), lambda i: (i + base, 0)), _mod_spec(TM, base), _full((1, D_MODEL)),
                _full((D_MODEL, ncols)), _full((1, ncols))]
    args = [x, mods, norm1.reshape(1, -1), w_ext, b_ext]
    out_specs = [pl.BlockSpec((TM, SWA_QW), row), pl.BlockSpec((TM, SWA_KW), row), pl.BlockSpec((TM, SWA_KW), row)]
    out_shape = [jax.ShapeDtypeStruct((n_rows, SWA_QW), BF16), jax.ShapeDtypeStruct((n_rows, SWA_KW), BF16),
                 jax.ShapeDtypeStruct((n_rows, SWA_KW), BF16)]
    if rope:
        tab = pl.BlockSpec((TM, LANES), lambda i: (_pos_block(i + base, TM), 0))
        in_specs += [tab] * 4
        args += list(tabs)
    else:
        out_specs.append(pl.BlockSpec((TM, SWA_CW), row))
        out_shape.append(jax.ShapeDtypeStruct((n_rows, SWA_CW), F32))
    return pl.pallas_call(
        functools.partial(_swa_proj_kernel, rope=rope),
        grid=(n_tiles,),
        in_specs=in_specs, out_specs=out_specs, out_shape=out_shape,
        compiler_params=_params("arbitrary"),
        name="swa_proj_lat" if rope else "swa_proj_ctx",
    )(*args)


def _swa_ctx_kernel(q_ref, k_ref, v_ref, sink_ref, wo_ref, bo_ref, x_ref, mod_ref, o_ref, o_s):
    for n in range(SWA_KV_HEADS):
        kn = k_ref[:, n * LANES:(n + 1) * LANES]
        vn = v_ref[:, n * LANES:(n + 1) * LANES]
        for g in range(SWA_GROUP):
            j = n * SWA_GROUP + g
            sl = slice(j * LANES, (j + 1) * LANES)
            s = _dot_nt(q_ref[:, sl], kn)
            sink = sink_ref[j]
            m = jnp.maximum(jnp.max(s, axis=-1, keepdims=True), sink)
            p = jnp.exp(s - m)
            l = jnp.sum(p, axis=-1, keepdims=True) + jnp.exp(sink - m)
            o_s[:, sl] = (_dot(p.astype(BF16), vn) / l).astype(BF16)
    mix = _dot(o_s[...], wo_ref[...]) + bo_ref[...]
    o_ref[...] = x_ref[...] + mod_ref[0][2:3] * mix


def _swa_lat_kernel(q_ref, k_ref, v_ref, ck_ref, cv_ref, sink_ref, wo_ref, bo_ref, x_ref, mod_ref, o_ref, o_s):
    span = TQ + 2 * SWA_WINDOW
    start = pl.program_id(1) * TQ
    ks = pl.multiple_of(jnp.clip(start - SWA_WINDOW, 0, DEC_SEQ - span), SWA_WINDOW)
    qpos = start + lax.broadcasted_iota(jnp.int32, (TQ, span), 0)
    kpos = ks + lax.broadcasted_iota(jnp.int32, (TQ, span), 1)
    valid = jnp.abs(kpos - qpos) <= SWA_WINDOW
    for n in range(SWA_KV_HEADS):
        nl = slice(n * LANES, (n + 1) * LANES)
        kc = ck_ref[0, :, nl]
        vc = cv_ref[0, :, nl]
        kl = k_ref[pl.ds(ks, span), nl]
        vl = v_ref[pl.ds(ks, span), nl]
        for g in range(SWA_GROUP):
            j = n * SWA_GROUP + g
            sl = slice(j * LANES, (j + 1) * LANES)
            qj = q_ref[:, sl]
            s_c = _dot_nt(qj, kc)
            s_l = jnp.where(valid, _dot_nt(qj, kl), -jnp.inf)
            sink = sink_ref[j]
            m = jnp.maximum(jnp.maximum(jnp.max(s_c, axis=-1, keepdims=True),
                                        jnp.max(s_l, axis=-1, keepdims=True)), sink)
            p_c = jnp.exp(s_c - m)
            p_l = jnp.exp(s_l - m)
            l = (jnp.sum(p_c, axis=-1, keepdims=True) + jnp.sum(p_l, axis=-1, keepdims=True)
                 + jnp.exp(sink - m))
            o = _dot(p_c.astype(BF16), vc) + _dot(p_l.astype(BF16), vl)
            o_s[:, sl] = (o / l).astype(BF16)
    mix = _dot(o_s[...], wo_ref[...]) + bo_ref[...]
    o_ref[...] = x_ref[...] + mod_ref[0][2:3] * mix


def _swa_attn(x, mods, q, k, v, sink, wo_ext, bo, cache=None):
    smem = pl.BlockSpec(memory_space=pltpu.SMEM)
    if cache is None:
        nq, base = 1, 0
        tile = lambda b, qi: (b, 0)
        in_specs = [pl.BlockSpec((TQ, SWA_QW), tile), pl.BlockSpec((SEQ, SWA_KW), tile),
                    pl.BlockSpec((SEQ, SWA_KW), tile)]
        args = [q, k, v]
        kern, nb, name = _swa_ctx_kernel, BATCH, "swa_attn_ctx"
    else:
        nq, base = DEC_SEQ // TQ, T_CTX // TQ
        tile = lambda b, qi: (b * nq + qi, 0)
        seq = lambda b, qi: (b, 0)
        cspec = pl.BlockSpec((1, PAST_LEN, SWA_KW), lambda b, qi: (b, 0, 0))
        in_specs = [pl.BlockSpec((TQ, SWA_QW), tile), pl.BlockSpec((DEC_SEQ, SWA_KW), seq),
                    pl.BlockSpec((DEC_SEQ, SWA_KW), seq), cspec, cspec]
        args = [q, k, v, cache[0], cache[1]]
        kern, nb, name = _swa_lat_kernel, DEC_BATCH, "swa_attn_lat"
    xtile = lambda b, qi: (base + b * nq + qi, 0)
    in_specs += [smem, _full((SWA_QW, D_MODEL)), _full((1, D_MODEL)), pl.BlockSpec((TQ, D_MODEL), xtile),
                 pl.BlockSpec((1, 6, D_MODEL), lambda b, qi: (_cond_of_row((base + b * nq + qi) * TQ), 0, 0))]
    args += [sink, wo_ext, bo.reshape(1, -1), x, mods]
    return pl.pallas_call(
        kern,
        grid=(nb, nq),
        in_specs=in_specs,
        out_specs=pl.BlockSpec((TQ, D_MODEL), xtile),
        out_shape=jax.ShapeDtypeStruct((T_ALL, D_MODEL), F32),
        scratch_shapes=[pltpu.VMEM((TQ, SWA_QW), BF16)],
        input_output_aliases={len(args) - 2: 0},
        compiler_params=_params("arbitrary", "arbitrary"),
        name=name,
    )(*args)


CONV_EXT = CONV_TM + 2 * CONV_HALO


def _conv_kernel(xp_ref, x_ref, xn_ref, mod_ref, n1_ref, w1_ref, b1_ref, dw_ref, dwb_ref, lng_ref, lnb_ref,
                 w2_ref, b2_ref, o_ref, ext_s, y_s):
    i = pl.program_id(0)
    tiles_per_seq = DEC_SEQ // CONV_TM
    j = (i - T_CTX // CONV_TM) % tiles_per_seq
    latent = i >= T_CTX // CONV_TM
    left_ok = jnp.logical_and(latent, j > 0)
    right_ok = jnp.logical_and(latent, j < tiles_per_seq - 1)
    mod = mod_ref[0]

    def glu_of(xv):
        h = _modulated(xv, n1_ref[...], mod, 0).astype(BF16)
        u = _dot(h, w1_ref[...]) + b1_ref[...]
        return u[:, :D_MODEL] * _sigmoid(u[:, D_MODEL:])

    ext_s[CONV_HALO:CONV_HALO + CONV_TM, :] = glu_of(x_ref[...])
    ext_s[0:CONV_HALO, :] = jnp.where(left_ok, glu_of(xp_ref[...]), 0.0)
    ext_s[CONV_HALO + CONV_TM:, :] = jnp.where(right_ok, glu_of(xn_ref[...]), 0.0)

    rows = 128
    sub = 8
    first = CONV_HALO - CONV_PAD
    span = rows + (first + CONV_WIDTH - 1) // sub * sub
    for c in range(D_MODEL // LANES):
        cl = slice(c * LANES, (c + 1) * LANES)
        dwc = dw_ref[:, cl]
        for r in range(CONV_TM // rows):
            window = ext_s[r * rows:r * rows + span + sub, cl]
            acc = jnp.zeros((rows, LANES), F32)
            for res in range(sub):
                shifted = window if res == 0 else pltpu.roll(window, span + sub - res, 0)
                for q in range(span // sub - rows // sub + 1):
                    w = q * sub + res - first
                    if 0 <= w < CONV_WIDTH:
                        acc = acc + shifted[q * sub:q * sub + rows] * dwc[w:w + 1]
            y_s[r * rows:(r + 1) * rows, cl] = acc
    y = y_s[...] + dwb_ref[...]
    mu = jnp.mean(y, axis=-1, keepdims=True)
    yc = y - mu
    var = jnp.mean(yc * yc, axis=-1, keepdims=True)
    yn = yc * lax.rsqrt(var + NORM_EPS) * lng_ref[...] + lnb_ref[...]
    z = (yn * _sigmoid(yn)).astype(BF16)
    mix = _dot(z, w2_ref[...]) + b2_ref[...]
    o_ref[...] = x_ref[...] + mod[2:3] * mix


def _conv_layer(x, mods, norm1, w1, b1, dw, dwb, ln_g, ln_b, w2, b2):
    n_tiles = T_ALL // CONV_TM
    per = CONV_TM // CONV_HALO
    n_halo_blocks = T_ALL // CONV_HALO
    row = lambda i: (i, 0)
    vec = lambda v: v.reshape(1, -1)
    return pl.pallas_call(
        _conv_kernel,
        grid=(n_tiles,),
        in_specs=[pl.BlockSpec((CONV_HALO, D_MODEL), lambda i: (jnp.maximum(i * per - 1, 0), 0)),
                  pl.BlockSpec((CONV_TM, D_MODEL), row),
                  pl.BlockSpec((CONV_HALO, D_MODEL), lambda i: (jnp.minimum((i + 1) * per, n_halo_blocks - 1), 0)),
                  _mod_spec(CONV_TM), _full((1, D_MODEL)),
                  _full((D_MODEL, 2 * D_MODEL)), _full((1, 2 * D_MODEL)),
                  _full((CONV_WIDTH, D_MODEL)), _full((1, D_MODEL)), _full((1, D_MODEL)), _full((1, D_MODEL)),
                  _full((D_MODEL, D_MODEL)), _full((1, D_MODEL))],
        out_specs=pl.BlockSpec((CONV_TM, D_MODEL), row),
        out_shape=jax.ShapeDtypeStruct((T_ALL, D_MODEL), F32),
        scratch_shapes=[pltpu.VMEM((CONV_EXT, D_MODEL), F32), pltpu.VMEM((CONV_TM, D_MODEL), F32)],
        compiler_params=_params("arbitrary"),
        name="conv_module",
    )(x, x, x, mods, vec(norm1), w1.astype(BF16), vec(b1), dw, vec(dwb), vec(ln_g), vec(ln_b),
      w2.astype(BF16), vec(b2))


def _router_kernel(x_ref, mod_ref, n2_ref, rw_ref, rb_ref, h_ref, idx_ref, gate_ref, cnt_ref):
    h = _modulated(x_ref[...], n2_ref[...], mod_ref[0], 3)
    _store_token_tiles(h_ref, h)
    cur = _dot_nt(rw_ref[...], h.astype(BF16)) + rb_ref[...]
    e_iota = lax.broadcasted_iota(jnp.int32, cur.shape, 0)
    vals, idxs = [], []
    for _ in range(TOP_K):
        m = jnp.max(cur, axis=0, keepdims=True)
        idx = jnp.min(jnp.where(cur == m, e_iota, N_EXPERTS), axis=0, keepdims=True)
        vals.append(m)
        idxs.append(idx)
        cur = jnp.where(e_iota == idx, -jnp.inf, cur)
    ex = [jnp.exp(v - vals[0]) for v in vals]
    tot = ex[0] + ex[1] + ex[2] + ex[3]
    for k in range(TOP_K):
        idx_ref[k:k + 1, :] = idxs[k]
        gate_ref[k:k + 1, :] = ex[k] / tot

    picked = jnp.sum(jnp.where(cur == -jnp.inf, 1.0, 0.0), axis=1, keepdims=True)

    @pl.when(pl.program_id(0) == 0)
    def _():
        cnt_ref[...] = jnp.zeros_like(cnt_ref)

    cnt_ref[...] += picked


def _router(x, mods, norm2, router_w, router_b):
    n_tiles = T_ALL // TM
    row = lambda i: (i, 0)
    col = lambda i: (0, i)
    return pl.pallas_call(
        _router_kernel,
        grid=(n_tiles,),
        in_specs=[pl.BlockSpec((TM, D_MODEL), row), _mod_spec(TM), _full((1, D_MODEL)),
                  _full((N_EXPERTS, D_MODEL)), _full((N_EXPERTS, 1))],
        out_specs=[pl.BlockSpec((TM * TOK_ROWS, LANES), row), pl.BlockSpec((TOP_K, TM), col),
                   pl.BlockSpec((TOP_K, TM), col), pl.BlockSpec((N_EXPERTS, 1), lambda i: (0, 0))],
        out_shape=[jax.ShapeDtypeStruct((T_ALL * TOK_ROWS, LANES), F32),
                   jax.ShapeDtypeStruct((TOP_K, T_ALL), jnp.int32),
                   jax.ShapeDtypeStruct((TOP_K, T_ALL), F32),
                   jax.ShapeDtypeStruct((N_EXPERTS, 1), F32)],
        compiler_params=_params("arbitrary"),
        name="router",
    )(x, mods, norm2.reshape(1, -1), router_w.T.astype(BF16), router_b.reshape(-1, 1))


N_ASSIGN = TOP_K * T_ALL
RANK_CHUNK = 256
RANK_BLOCK = 2048


def _block_layout(counts):
    counts = counts.reshape(N_EXPERTS).astype(jnp.int32)
    padded = (counts + MOE_BM - 1) // MOE_BM * MOE_BM
    pad_end = jnp.cumsum(padded)
    pad_start = pad_end - padded
    starts = jnp.arange(MOE_BLOCKS + 1, dtype=jnp.int32) * MOE_BM
    e_ids = jnp.arange(N_EXPERTS, dtype=jnp.int32)
    used = padded > 0
    last_used = jnp.max(jnp.where(used, e_ids, 0))
    block_e = jnp.minimum(jnp.sum((pad_end[None, :] <= starts[:, None]).astype(jnp.int32), axis=1), last_used)
    n_used = (pad_end[-1] // MOE_BM).reshape(1)
    run = jnp.cumsum(used.astype(jnp.int32)) - 1
    later_used = jnp.where(jnp.logical_and(used[None, :], e_ids[None, :] > e_ids[:, None]), e_ids[None, :], N_EXPERTS)
    next_used = jnp.min(later_used, axis=1)
    next_used = jnp.where(next_used == N_EXPERTS, e_ids, next_used)
    block_plan = (block_e, next_used[block_e], run[block_e] & 1, n_used)
    return pad_start.astype(F32).reshape(N_EXPERTS, 1), block_plan


def _rank_kernel(idx_ref, ps_ref, u_ref, dest_ref, carry):
    @pl.when(pl.program_id(0) == 0)
    def _():
        carry[...] = ps_ref[...]

    e_iota = lax.broadcasted_iota(jnp.int32, (N_EXPERTS, RANK_CHUNK), 0)
    for c in range(RANK_BLOCK // RANK_CHUNK):
        sl = slice(c * RANK_CHUNK, (c + 1) * RANK_CHUNK)
        hit = e_iota == idx_ref[:, sl]
        pref = _dot(jnp.where(hit, 1.0, 0.0).astype(BF16), u_ref[...])
        base = carry[...]
        dest = jnp.sum(jnp.where(hit, pref + base, 0.0), axis=0, keepdims=True) - 1.0
        dest_ref[:, sl] = dest.astype(jnp.int32)
        carry[...] = base + pref[:, RANK_CHUNK - 1:RANK_CHUNK]


def _assignment_slots(idx_t, pad_start):
    tri = jnp.asarray(np.triu(np.ones((RANK_CHUNK, RANK_CHUNK), np.float32)), dtype=BF16)
    blk = pl.BlockSpec((1, RANK_BLOCK), lambda i: (0, i))
    return pl.pallas_call(
        _rank_kernel,
        grid=(N_ASSIGN // RANK_BLOCK,),
        in_specs=[blk, _full((N_EXPERTS, 1)), _full((RANK_CHUNK, RANK_CHUNK))],
        out_specs=blk,
        out_shape=jax.ShapeDtypeStruct((1, N_ASSIGN), jnp.int32),
        scratch_shapes=[pltpu.VMEM((N_EXPERTS, 1), F32)],
        compiler_params=_params("arbitrary"),
        name="moe_rank",
    )(idx_t.reshape(1, N_ASSIGN), pad_start, tri)


INV_UNROLL = 32
INV_CHUNK = 4096
INV_STEPS = N_ASSIGN // INV_CHUNK
N_SLOTS = MOE_CAP + MOE_BM


def _inverse_kernel(dest_ref, pad_hbm, slot_ref, sem):
    i = pl.program_id(0)

    @pl.when(i == 0)
    def _():
        fill = pltpu.make_async_copy(pad_hbm, slot_ref, sem)
        fill.start()
        fill.wait()

    base = i * INV_CHUNK

    def place(j, carry):
        for u in range(INV_UNROLL):
            o = j * INV_UNROLL + u
            slot_ref[dest_ref[0, 0, o]] = base + o
        return carry

    lax.fori_loop(0, INV_CHUNK // INV_UNROLL, place, 0)


def _slot_assignments(dest):
    s = np.arange(N_SLOTS, dtype=np.int32)
    pad_ids = jnp.asarray(N_ASSIGN + (s & (MOE_BM - 1)) + np.where(s >= MOE_CAP, MOE_BM, 0).astype(np.int32))
    return pl.pallas_call(
        _inverse_kernel,
        grid=(INV_STEPS,),
        in_specs=[pl.BlockSpec((1, 1, INV_CHUNK), lambda i: (i, 0, 0), memory_space=pltpu.SMEM),
                  pl.BlockSpec(memory_space=pl.ANY)],
        out_specs=pl.BlockSpec(memory_space=pltpu.SMEM),
        out_shape=jax.ShapeDtypeStruct((N_SLOTS,), jnp.int32),
        scratch_shapes=[pltpu.SemaphoreType.DMA(())],
        compiler_params=_params("arbitrary"),
        name="moe_slots",
    )(dest.reshape(INV_STEPS, 1, INV_CHUNK), pad_ids)


YG_TOKENS = N_ASSIGN + 2 * MOE_BM
BLOCK_TILE_ROWS = MOE_BM * TOK_ROWS
MOE_CHUNK = 256


def _moe_kernel(be_ref, ne_ref, par_ref, nu_ref, sa_prev_ref, sa_ref, sa_next_ref, sa_next2_ref, h_hbm,
                wg_hbm, wu_hbm, wd_hbm, bg_ref, bu_ref, bd_ref, yg_hbm,
                xbuf, ybuf, x_s, act_s, wg_f, wu_f, wd_f, wgu_s, wd_s, gsem, ssem, wsem):
    i = pl.program_id(0)
    n_used = nu_ref[0]
    cur = i % 2
    nxt = 1 - cur
    gcur = lax.rem(i, 3)
    gnext = lax.rem(i + 1, 3)
    gnext2 = lax.rem(i + 2, 3)

    def weight_copies(e, slot):
        return [pltpu.make_async_copy(src.at[e], dst.at[slot], wsem.at[slot])
                for src, dst in ((wg_hbm, wg_f), (wu_hbm, wu_f), (wd_hbm, wd_f))]

    def start_gather(sa, buf):
        for r in range(MOE_BM):
            tok = sa[0, 0, r] & (T_ALL - 1)
            pltpu.make_async_copy(h_hbm.at[pl.ds(tok * TOK_ROWS, TOK_ROWS)],
                                  xbuf.at[buf, pl.ds(r * TOK_ROWS, TOK_ROWS)], gsem.at[buf]).start()

    def start_scatter(sa, buf):
        for r in range(MOE_BM):
            pltpu.async_copy(ybuf.at[buf, pl.ds(r * TOK_ROWS, TOK_ROWS)],
                             yg_hbm.at[pl.ds(sa[0, 0, r] * TOK_ROWS, TOK_ROWS)], ssem.at[buf], priority=1)

    def wait_gather(buf):
        pltpu.make_async_copy(h_hbm.at[pl.ds(0, BLOCK_TILE_ROWS)], xbuf.at[buf], gsem.at[buf]).wait()

    def wait_scatter(buf):
        pltpu.make_async_copy(ybuf.at[buf], yg_hbm.at[pl.ds(0, BLOCK_TILE_ROWS)], ssem.at[buf]).wait()

    @pl.when(i == 0)
    def _():
        for cp in weight_copies(be_ref[0], 0):
            cp.start()
        start_gather(sa_ref, 0)
        start_gather(sa_next_ref, 1)
        ybuf[...] = jnp.zeros_like(ybuf)
        pltpu.make_async_copy(ybuf.at[0], yg_hbm.at[pl.ds(N_ASSIGN * TOK_ROWS, BLOCK_TILE_ROWS)], ssem.at[0]).start()

    @pl.when(i <= n_used)
    def _():
        e = be_ref[i]

        @pl.when(jnp.logical_or(i == 0, e != be_ref[jnp.maximum(i - 1, 0)]))
        def _():
            slot = par_ref[i]
            for cp in weight_copies(e, slot):
                cp.wait()
            for c in range(D_MODEL // MOE_CHUNK):
                cs = slice(c * MOE_CHUNK, (c + 1) * MOE_CHUNK)
                wgu_s[:, 2 * c * MOE_CHUNK:(2 * c + 1) * MOE_CHUNK] = wg_f[slot, :, cs].astype(BF16)
                wgu_s[:, (2 * c + 1) * MOE_CHUNK:(2 * c + 2) * MOE_CHUNK] = wu_f[slot, :, cs].astype(BF16)
            wd_s[...] = wd_f[slot].astype(BF16)

            @pl.when(ne_ref[i] != e)
            def _():
                for cp in weight_copies(ne_ref[i], 1 - slot):
                    cp.start()

        wait_gather(gcur)
        x_s[...] = _load_token_tiles(xbuf.at[gcur], MOE_BM).astype(BF16)
        start_gather(sa_next2_ref, gnext2)
        for c in range(D_MODEL // MOE_CHUNK):
            cs = slice(c * MOE_CHUNK, (c + 1) * MOE_CHUNK)
            gu = _dot(x_s[...], wgu_s[:, 2 * c * MOE_CHUNK:(2 * c + 2) * MOE_CHUNK])
            g = jnp.minimum(gu[:, :MOE_CHUNK] + bg_ref[0, :, cs], SWIGLU_LIMIT)
            u = jnp.clip(gu[:, MOE_CHUNK:] + bu_ref[0, :, cs], -SWIGLU_LIMIT, SWIGLU_LIMIT)
            act_s[:, cs] = ((u + 1.0) * (g * _sigmoid(SWIGLU_ALPHA * g))).astype(BF16)
        wait_scatter(cur)
        start_scatter(sa_prev_ref, nxt)
        half = D_MODEL // 2
        for c in range(2):
            y = _dot(act_s[...], wd_s[:, c * half:(c + 1) * half]) + bd_ref[0, :, c * half:(c + 1) * half]
            for j in range(TOK_ROWS // 2):
                jj = c * (TOK_ROWS // 2) + j
                ybuf.at[cur][pl.ds(jj, MOE_BM, stride=TOK_ROWS), :] = y[:, j * LANES:(j + 1) * LANES]

    @pl.when(i == n_used)
    def _():
        wait_gather(gnext)
        wait_gather(gnext2)
        wait_scatter(nxt)


def _moe_experts(h, slots, block_plan, wg, bg, wu, bu, wd, bd):
    bspec = pl.BlockSpec((1, 1, D_MODEL), lambda i, be, *_: (be[i], 0, 0))
    sspec = lambda f: pl.BlockSpec((1, 1, MOE_BM), lambda i, *_: (f(i), 0, 0), memory_space=pltpu.SMEM)
    hbm = pl.BlockSpec(memory_space=pl.ANY)
    b3 = lambda b: b.reshape(N_EXPERTS, 1, D_MODEL)
    slots3 = slots.reshape(MOE_BLOCKS + 1, 1, MOE_BM)
    wbuf = pltpu.VMEM((2, D_MODEL, D_MODEL), F32)
    return pl.pallas_call(
        _moe_kernel,
        grid_spec=pltpu.PrefetchScalarGridSpec(
            num_scalar_prefetch=4,
            grid=(MOE_BLOCKS + 1,),
            in_specs=[sspec(lambda i: jnp.where(i == 0, MOE_BLOCKS, i - 1)), sspec(lambda i: i),
                      sspec(lambda i: jnp.minimum(i + 1, MOE_BLOCKS)), sspec(lambda i: jnp.minimum(i + 2, MOE_BLOCKS)),
                      hbm, hbm, hbm, hbm, bspec, bspec, bspec],
            out_specs=hbm,
            scratch_shapes=[pltpu.VMEM((3, BLOCK_TILE_ROWS, LANES), F32), pltpu.VMEM((2, BLOCK_TILE_ROWS, LANES), F32),
                            pltpu.VMEM((MOE_BM, D_MODEL), BF16), pltpu.VMEM((MOE_BM, D_MODEL), BF16),
                            wbuf, wbuf, wbuf,
                            pltpu.VMEM((D_MODEL, 2 * D_MODEL), BF16), pltpu.VMEM((D_MODEL, D_MODEL), BF16),
                            pltpu.SemaphoreType.DMA((3,)), pltpu.SemaphoreType.DMA((2,)),
                            pltpu.SemaphoreType.DMA((2,))]),
        out_shape=jax.ShapeDtypeStruct((YG_TOKENS * TOK_ROWS, LANES), F32),
        compiler_params=_params("arbitrary"),
        name="moe_experts",
    )(*block_plan, slots3, slots3, slots3, slots3, h, wg, wu, wd, b3(bg), b3(bu), b3(bd))


def _combine_kernel(*refs, final):
    if final:
        x_ref, y0_ref, y1_ref, y2_ref, y3_ref, gate_ref, mod_ref, fg_ref, o_ref = refs
    else:
        x_ref, y0_ref, y1_ref, y2_ref, y3_ref, gate_ref, mod_ref, o_ref = refs
    gate = gate_ref[...]
    n = x_ref.shape[0]
    g2 = mod_ref[0][5:6]
    sumsq = jnp.zeros((n, 1), F32)
    for j in range(TOK_ROWS):
        rows = pl.ds(j, n, stride=TOK_ROWS)
        sl = slice(j * LANES, (j + 1) * LANES)
        acc = y0_ref[rows, :] * gate[:, 0:1]
        for k, y_ref in enumerate((y1_ref, y2_ref, y3_ref), start=1):
            acc = acc + y_ref[rows, :] * gate[:, k:k + 1]
        xn = x_ref[:, sl] + g2[:, sl] * acc
        o_ref[:, sl] = xn
        if final:
            sumsq = sumsq + jnp.sum(xn * xn, axis=-1, keepdims=True)
    if final:
        inv = lax.rsqrt(sumsq * (1.0 / D_MODEL) + NORM_EPS)
        o_ref[...] = o_ref[...] * inv * fg_ref[...]


def _combine(x, yg, gates, mods, final_g=None, base_rows=0, n_rows=T_ALL):
    final = final_g is not None
    base = base_rows // TM
    n_all = T_ALL // TM
    row = lambda i: (i + base, 0)
    ysp = lambda k: pl.BlockSpec((TM * TOK_ROWS, LANES), lambda i: (k * n_all + i + base, 0))
    in_specs = [pl.BlockSpec((TM, D_MODEL), row), ysp(0), ysp(1), ysp(2), ysp(3),
                pl.BlockSpec((TM, TOP_K), row), _mod_spec(TM, base)]
    args = [x, yg, yg, yg, yg, gates, mods]
    if final:
        in_specs.append(_full((1, D_MODEL)))
        args.append(final_g.reshape(1, -1))
    return pl.pallas_call(
        functools.partial(_combine_kernel, final=final),
        grid=(n_rows // TM,),
        in_specs=in_specs,
        out_specs=pl.BlockSpec((TM, D_MODEL), (lambda i: (i, 0)) if final else row),
        out_shape=jax.ShapeDtypeStruct((n_rows, D_MODEL), F32),
        input_output_aliases={} if final else {0: 0},
        compiler_params=_params("arbitrary"),
        name="moe_combine_final" if final else "moe_combine",
    )(*args)


def _moe_layer(x, mods, norm2, router_w, router_b, wg, bg, wu, bu, wd, bd, final_g=None):
    h, idx_t, gate_t, counts = _router(x, mods, norm2, router_w, router_b)
    pad_start, block_plan = _block_layout(counts)
    slots = _slot_assignments(_assignment_slots(idx_t, pad_start))
    yg = _moe_experts(h, slots, block_plan, wg, bg, wu, bu, wd, bd)
    gates = gate_t.T
    if final_g is None:
        return _combine(x, yg, gates, mods)
    return (_combine(x, yg, gates, mods, final_g, 0, T_CTX),
            _combine(x, yg, gates, mods, final_g, T_CTX, T_LAT))


def _pad_lanes(w, lead_pad, width):
    tail = width - lead_pad - w.shape[-1]
    cfg = [(0, 0)] * (w.ndim - 1) + [(lead_pad, tail)]
    return jnp.pad(w, cfg)


def _mla_weights(wa, wuq, wukv, wo):
    krope_w = wa[:, MLA_Q_RANK + MLA_KV_RANK:]
    wa_ext = jnp.concatenate([wa[:, :MLA_Q_RANK + MLA_KV_RANK],
                              _pad_lanes(krope_w, MLA_ROPE_OFF, LANES),
                              _pad_lanes(krope_w, 0, LANES)], axis=1).astype(BF16)
    q3 = wuq.reshape(MLA_Q_RANK, MLA_HEADS, MLA_NOPE + MLA_ROPE)
    wuq_ext = jnp.concatenate([q3[..., :MLA_NOPE], _pad_lanes(q3[..., MLA_NOPE:], MLA_ROPE // 2, LANES - MLA_NOPE)],
                              axis=-1).reshape(MLA_Q_RANK, MLA_HEADS * LANES).astype(BF16)
    kv3 = wukv.reshape(MLA_KV_RANK, MLA_HEADS, MLA_NOPE + MLA_V)
    wk_ext = _pad_lanes(kv3[..., :MLA_NOPE], 0, LANES).reshape(MLA_KV_RANK, MLA_HEADS * LANES).astype(BF16)
    wv = kv3[..., MLA_NOPE:].reshape(MLA_KV_RANK, MLA_HEADS * MLA_V).astype(BF16)
    return wa_ext, wuq_ext, wk_ext, wv, wo.astype(BF16)


def _swa_weights(wqkv, bqkv, wo):
    nq = SWA_HEADS * SWA_HEAD_DIM
    nkv = SWA_KV_HEADS * SWA_HEAD_DIM

    def slabs(w, heads):
        lead = w.shape[:-1]
        return _pad_lanes(w.reshape(lead + (heads, SWA_HEAD_DIM)), 0, LANES).reshape(lead + (heads * LANES,))

    def ext(w, with_compact):
        parts = [slabs(w[..., :nq], SWA_HEADS), slabs(w[..., nq:nq + nkv], SWA_KV_HEADS),
                 slabs(w[..., nq + nkv:], SWA_KV_HEADS)]
        if with_compact:
            parts.append(w[..., nq:])
        return jnp.concatenate(parts, axis=-1)

    b2 = bqkv.reshape(1, -1)
    wo3 = wo.reshape(SWA_HEADS, SWA_HEAD_DIM, D_MODEL)
    wo_ext = jnp.pad(wo3, ((0, 0), (0, LANES - SWA_HEAD_DIM), (0, 0))).reshape(SWA_QW, D_MODEL).astype(BF16)
    return (ext(wqkv, True).astype(BF16), ext(b2, True), ext(wqkv, False).astype(BF16), ext(b2, False), wo_ext)


def kernel(x_prompt, x_sample, cache_l0_ckv, cache_l0_krope, cache_l1_k, cache_l1_v, cache_l3_ckv, cache_l3_krope, c, c_ctx, l0_mla_wa, l0_mla_q_norm, l0_mla_wuq, l0_mla_kv_norm, l0_mla_wukv, l0_mla_wo, l0_mod_w, l0_mod_b, l0_norm1, l0_norm2, l0_router_w, l0_router_b, l0_moe_wg, l0_moe_bg, l0_moe_wu, l0_moe_bu, l0_moe_wd, l0_moe_bd, l1_swa_wqkv, l1_swa_bqkv, l1_swa_sink, l1_swa_wo, l1_swa_bo, l1_mod_w, l1_mod_b, l1_norm1, l1_norm2, l1_router_w, l1_router_b, l1_moe_wg, l1_moe_bg, l1_moe_wu, l1_moe_bu, l1_moe_wd, l1_moe_bd, l2_conv_w1, l2_conv_b1, l2_conv_dw, l2_conv_dwb, l2_conv_ln_g, l2_conv_ln_b, l2_conv_w2, l2_conv_b2, l2_mod_w, l2_mod_b, l2_norm1, l2_norm2, l2_router_w, l2_router_b, l2_moe_wg, l2_moe_bg, l2_moe_wu, l2_moe_bu, l2_moe_wd, l2_moe_bd, l3_mla_wa, l3_mla_q_norm, l3_mla_wuq, l3_mla_kv_norm, l3_mla_wukv, l3_mla_wo, l3_mod_w, l3_mod_b, l3_norm1, l3_norm2, l3_router_w, l3_router_b, l3_moe_wg, l3_moe_bg, l3_moe_wu, l3_moe_bu, l3_moe_wd, l3_moe_bd, final_norm):
    x = jnp.concatenate([x_prompt.reshape(T_CTX, D_MODEL), x_sample.reshape(T_LAT, D_MODEL)], axis=0)
    cond = jnp.concatenate([c, c_ctx[None, :], jnp.zeros((N_COND - DEC_BATCH - 1, D_MODEL), F32)], axis=0)

    mla_c, mla_s = _rope_tables(MLA_ROPE, MLA_ROPE_LEAD, TM)
    mla_tabs = (mla_c * MLA_SCALE, mla_s * MLA_SCALE, mla_c, mla_s)
    swa_c, swa_s = _rope_tables(SWA_HEAD_DIM, 0, TM)
    swa_tabs = (swa_c * SWA_SCALE, swa_s * SWA_SCALE, swa_c, swa_s)

    def mla_layer(x, mods, norm1, wa, q_norm, wuq, kv_norm, wukv, wo, cache_ckv, cache_krope):
        wa_ext, wuq_ext, wk_ext, wv_ext, wo_ext = _mla_weights(wa, wuq, wukv, wo)
        q, ckv, kp, kst = _mla_proj(x, mods, norm1, wa_ext, q_norm, wuq_ext, kv_norm, mla_tabs)
        x = _mla_attn(x, mods, q, ckv, kp, wk_ext, wv_ext, wo_ext)
        cache = (cache_ckv, _pad_lanes(cache_krope, MLA_ROPE_OFF, LANES).astype(BF16))
        x = _mla_attn(x, mods, q, ckv, kp, wk_ext, wv_ext, wo_ext, cache)
        return x, (ckv[:T_CTX].reshape(BATCH, SEQ, MLA_KV_RANK), kst[:T_CTX].reshape(BATCH, SEQ, MLA_ROPE))

    def moe(x, mods, norm2, rw, rb, wg, bg, wu, bu, wd, bd):
        return _moe_layer(x, mods, norm2, rw, rb, wg, bg, wu, bu, wd, bd)

    mods = _modulation(cond, l0_mod_w, l0_mod_b)
    x, (st0_ckv, st0_krope) = mla_layer(x, mods, l0_norm1, l0_mla_wa, l0_mla_q_norm, l0_mla_wuq, l0_mla_kv_norm,
                                        l0_mla_wukv, l0_mla_wo, cache_l0_ckv, cache_l0_krope)
    x = moe(x, mods, l0_norm2, l0_router_w, l0_router_b, l0_moe_wg, l0_moe_bg, l0_moe_wu, l0_moe_bu,
            l0_moe_wd, l0_moe_bd)

    mods = _modulation(cond, l1_mod_w, l1_mod_b)
    w_ctx, b_ctx, w_lat, b_lat, swa_wo_ext = _swa_weights(l1_swa_wqkv, l1_swa_bqkv, l1_swa_wo)
    q_c, k_c, v_c, kv_c = _swa_proj(x, mods, l1_norm1, w_ctx, b_ctx, None, rope=False)
    q_l, k_l, v_l = _swa_proj(x, mods, l1_norm1, w_lat, b_lat, swa_tabs, rope=True)
    x = _swa_attn(x, mods, q_c, k_c, v_c, l1_swa_sink, swa_wo_ext, l1_swa_bo)
    nkv = SWA_KV_HEADS * SWA_HEAD_DIM
    cache_k = _pad_lanes(cache_l1_k, 0, LANES).reshape(DEC_BATCH, PAST_LEN, SWA_KW).astype(BF16)
    cache_v = _pad_lanes(cache_l1_v, 0, LANES).reshape(DEC_BATCH, PAST_LEN, SWA_KW).astype(BF16)
    x = _swa_attn(x, mods, q_l, k_l, v_l, l1_swa_sink, swa_wo_ext, l1_swa_bo, (cache_k, cache_v))
    st1_k = kv_c[:, :nkv].reshape(BATCH, SEQ, SWA_KV_HEADS, SWA_HEAD_DIM)
    st1_v = kv_c[:, nkv:].reshape(BATCH, SEQ, SWA_KV_HEADS, SWA_HEAD_DIM)
    x = moe(x, mods, l1_norm2, l1_router_w, l1_router_b, l1_moe_wg, l1_moe_bg, l1_moe_wu, l1_moe_bu,
            l1_moe_wd, l1_moe_bd)

    mods = _modulation(cond, l2_mod_w, l2_mod_b)
    x = _conv_layer(x, mods, l2_norm1, l2_conv_w1, l2_conv_b1, l2_conv_dw, l2_conv_dwb, l2_conv_ln_g,
                    l2_conv_ln_b, l2_conv_w2, l2_conv_b2)
    x = moe(x, mods, l2_norm2, l2_router_w, l2_router_b, l2_moe_wg, l2_moe_bg, l2_moe_wu, l2_moe_bu,
            l2_moe_wd, l2_moe_bd)

    mods = _modulation(cond, l3_mod_w, l3_mod_b)
    x, (st3_ckv, st3_krope) = mla_layer(x, mods, l3_norm1, l3_mla_wa, l3_mla_q_norm, l3_mla_wuq, l3_mla_kv_norm,
                                        l3_mla_wukv, l3_mla_wo, cache_l3_ckv, cache_l3_krope)
    y_ctx, y_lat = _moe_layer(x, mods, l3_norm2, l3_router_w, l3_router_b, l3_moe_wg, l3_moe_bg, l3_moe_wu,
                              l3_moe_bu, l3_moe_wd, l3_moe_bd, final_g=final_norm)
    y_prompt = y_ctx.reshape(BATCH, SEQ, D_MODEL)
    y_sample = y_lat.reshape(DEC_BATCH, DEC_SEQ, D_MODEL)
    return (y_prompt, y_sample, st0_ckv, st0_krope, st1_k, st1_v, st3_ckv, st3_krope)
```

```python
import functools

import jax
import jax.numpy as jnp
import numpy as np
from jax import lax
from jax.experimental import pallas as pl
from jax.experimental.pallas import tpu as pltpu

F32 = jnp.float32
BF16 = jnp.bfloat16

D_MODEL = 1024
BATCH = 32
SEQ = 256
DEPTH = 4
DEC_BATCH = 4
DEC_SEQ = 2048
PAST_LEN = 256
GRID_W = 64
NORM_EPS = 1e-6
ROPE_THETA = 10000.0

MLA_HEADS = 16
MLA_NOPE = 64
MLA_ROPE = 32
MLA_V = 64
MLA_Q_RANK = 384
MLA_KV_RANK = 256
MLA_SCALE = (MLA_NOPE + MLA_ROPE) ** -0.5

SWA_HEADS = 16
SWA_KV_HEADS = 4
SWA_GROUP = 4
SWA_HEAD_DIM = 64
SWA_WINDOW = 128
SWA_SCALE = SWA_HEAD_DIM ** -0.5

CONV_WIDTH = 31
CONV_PAD = CONV_WIDTH // 2

N_EXPERTS = 32
TOP_K = 4
SWIGLU_ALPHA = 1.702
SWIGLU_LIMIT = 7.0

LANES = 128
T_CTX = BATCH * SEQ
T_LAT = DEC_BATCH * DEC_SEQ
T_ALL = T_CTX + T_LAT
CTX_COND = DEC_BATCH
N_COND = 8

TM = 512
TQ = 256
CONV_TM = 256
CONV_HALO = 16
MOE_BM = 256
MOE_BLOCKS = (T_ALL * TOP_K + N_EXPERTS * (MOE_BM - 1) + MOE_BM - 1) // MOE_BM
MOE_CAP = MOE_BLOCKS * MOE_BM
VMEM_LIMIT = 56 * 1024 * 1024


def _params(*sem):
    return pltpu.CompilerParams(dimension_semantics=sem, vmem_limit_bytes=VMEM_LIMIT)


def _cond_of_row(row0):
    return jnp.where(row0 < T_CTX, CTX_COND, (row0 - T_CTX) // DEC_SEQ)


def _pos_block(i, tm):
    row0 = i * tm
    return jnp.where(row0 < T_CTX, DEC_SEQ // tm, ((row0 - T_CTX) % DEC_SEQ) // tm)


def _rms(x, g):
    ms = jnp.mean(x * x, axis=-1, keepdims=True)
    return x * lax.rsqrt(ms + NORM_EPS) * g


def _modulated(x, g, mod, j):
    return _rms(x, g) * (1.0 + mod[j + 1:j + 2]) + mod[j:j + 1]


def _sigmoid(x):
    return 1.0 / (1.0 + jnp.exp(-x))


def _dot(a, b):
    return jnp.dot(a, b, preferred_element_type=F32)


def _dot_nt(a, b):
    return lax.dot_general(a, b, (((1,), (1,)), ((), ())), preferred_element_type=F32)


TOK_ROWS = D_MODEL // LANES


def _store_token_tiles(ref, v):
    n = v.shape[0]
    for j in range(TOK_ROWS):
        ref[pl.ds(j, n, stride=TOK_ROWS), :] = v[:, j * LANES:(j + 1) * LANES]


def _load_token_tiles(ref, n):
    return jnp.concatenate([ref[pl.ds(j, n, stride=TOK_ROWS), :] for j in range(TOK_ROWS)], axis=-1)


def _rope_slab(a, c, s, shift):
    return a * c + (pltpu.roll(a, shift, 1) - pltpu.roll(a, LANES - shift, 1)) * s


def _mod_kernel(c_ref, w_ref, b_ref, o_ref):
    c = c_ref[...]
    s = (c * _sigmoid(c)).astype(BF16)
    o_ref[...] = _dot(s, w_ref[...].astype(BF16)) + b_ref[...]


def _modulation(cond, w, b):
    tn = 1536
    out = pl.pallas_call(
        _mod_kernel,
        grid=(6 * D_MODEL // tn,),
        in_specs=[pl.BlockSpec((N_COND, D_MODEL), lambda j: (0, 0)),
                  pl.BlockSpec((D_MODEL, tn), lambda j: (0, j)),
                  pl.BlockSpec((1, tn), lambda j: (0, j))],
        out_specs=pl.BlockSpec((N_COND, tn), lambda j: (0, j)),
        out_shape=jax.ShapeDtypeStruct((N_COND, 6 * D_MODEL), F32),
        compiler_params=_params("arbitrary"),
        name="modulation",
    )(cond, w, b.reshape(1, -1))
    return out.reshape(N_COND, 6, D_MODEL)


def _mod_spec(tm, base=0):
    return pl.BlockSpec((1, 6, D_MODEL), lambda i, *_: (_cond_of_row((i + base) * tm), 0, 0))


def _full(shape):
    nd = len(shape)
    return pl.BlockSpec(shape, lambda *_: (0,) * nd, pipeline_mode=pl.Buffered(1))


def _rope_cos_sin(n_tok, rot_dim):
    rows = n_tok // GRID_W
    row = jnp.repeat(jnp.arange(rows, dtype=F32), GRID_W)
    col = jnp.tile(jnp.arange(GRID_W, dtype=F32), rows)
    n_freq = rot_dim // 4
    inv_freq = ROPE_THETA ** (-jnp.arange(n_freq, dtype=F32) / n_freq)
    ang = jnp.concatenate([row[:, None] * inv_freq, col[:, None] * inv_freq], axis=-1)
    return jnp.cos(ang), jnp.sin(ang)


def _rope_tables(rot_dim, lead, tm):
    cos, sin = _rope_cos_sin(DEC_SEQ, rot_dim)
    gap = rot_dim // 2 if lead else 0
    tail = LANES - lead - gap - rot_dim
    n = DEC_SEQ

    def slab(lead_val, rot):
        return jnp.concatenate([jnp.full((rot.shape[0], lead), lead_val, F32),
                                jnp.zeros((rot.shape[0], gap), F32), rot,
                                jnp.zeros((rot.shape[0], tail), F32)], axis=-1)

    c = slab(1.0, jnp.concatenate([cos, cos], axis=-1))
    s = slab(0.0, jnp.concatenate([sin, sin], axis=-1))
    c_id = slab(1.0, jnp.ones((tm, rot_dim), F32))
    s_id = jnp.zeros((tm, LANES), F32)
    del n
    return jnp.concatenate([c, c_id], axis=0), jnp.concatenate([s, s_id], axis=0)


MLA_A_COLS = MLA_Q_RANK + MLA_KV_RANK + 2 * LANES
MLA_ROPE_LEAD = MLA_NOPE
MLA_ROPE_OFF = MLA_NOPE + MLA_ROPE // 2


def _mla_proj_kernel(x_ref, mod_ref, n1_ref, wa_ref, qn_ref, wuq_ref, kvn_ref,
                     cq_ref, sq_ref, ck_ref, sk_ref,
                     q_ref, ckv_ref, kp_ref, kst_ref):
    h = _modulated(x_ref[...], n1_ref[...], mod_ref[0], 0).astype(BF16)
    a = _dot(h, wa_ref[...])
    cq = _rms(a[:, :MLA_Q_RANK], qn_ref[...]).astype(BF16)
    ckv_ref[...] = _rms(a[:, MLA_Q_RANK:MLA_Q_RANK + MLA_KV_RANK], kvn_ref[...])
    k0 = MLA_Q_RANK + MLA_KV_RANK
    kp = a[:, k0:k0 + LANES]
    kst_ref[...] = a[:, k0 + LANES:k0 + LANES + MLA_ROPE]
    qa = _dot(cq, wuq_ref[...])
    cq_t = cq_ref[...]
    is_ctx = pl.program_id(0) < T_CTX // TM

    @pl.when(is_ctx)
    def _():
        kp_ref[...] = (kp * ck_ref[...]).astype(BF16)
        for hd in range(MLA_HEADS):
            sl = slice(hd * LANES, (hd + 1) * LANES)
            q_ref[:, sl] = (qa[:, sl] * cq_t).astype(BF16)

    @pl.when(jnp.logical_not(is_ctx))
    def _():
        sq_t = sq_ref[...]
        kp_ref[...] = _rope_slab(kp, ck_ref[...], sk_ref[...], MLA_ROPE // 2).astype(BF16)
        for hd in range(MLA_HEADS):
            sl = slice(hd * LANES, (hd + 1) * LANES)
            q_ref[:, sl] = _rope_slab(qa[:, sl], cq_t, sq_t, MLA_ROPE // 2).astype(BF16)


def _mla_proj(x, mods, norm1, wa_ext, q_norm, wuq_ext, kv_norm, tabs):
    cq, sq, ck, sk = tabs
    n_tiles = T_ALL // TM
    row = lambda i: (i, 0)
    tab = pl.BlockSpec((TM, LANES), lambda i: (_pos_block(i, TM), 0))
    return pl.pallas_call(
        _mla_proj_kernel,
        grid=(n_tiles,),
        in_specs=[pl.BlockSpec((TM, D_MODEL), row), _mod_spec(TM), _full((1, D_MODEL)),
                  _full((D_MODEL, MLA_A_COLS)), _full((1, MLA_Q_RANK)),
                  _full((MLA_Q_RANK, MLA_HEADS * LANES)), _full((1, MLA_KV_RANK)),
                  tab, tab, tab, tab],
        out_specs=[pl.BlockSpec((TM, MLA_HEADS * LANES), row), pl.BlockSpec((TM, MLA_KV_RANK), row),
                   pl.BlockSpec((TM, LANES), row), pl.BlockSpec((TM, MLA_ROPE), row)],
        out_shape=[jax.ShapeDtypeStruct((T_ALL, MLA_HEADS * LANES), BF16),
                   jax.ShapeDtypeStruct((T_ALL, MLA_KV_RANK), F32),
                   jax.ShapeDtypeStruct((T_ALL, LANES), BF16),
                   jax.ShapeDtypeStruct((T_ALL, MLA_ROPE), F32)],
        compiler_params=_params("arbitrary"),
        name="mla_proj",
    )(x, mods, norm1.reshape(1, -1), wa_ext, q_norm.reshape(1, -1), wuq_ext, kv_norm.reshape(1, -1),
      cq, sq, ck, sk)


def _mla_attn_kernel(*refs, n_lat, n_cache):
    if n_cache:
        (q_ref, ckv_ref, kp_ref, cckv_ref, ckp_ref, wk_ref, wv_ref, wo_ref, x_ref, mod_ref,
         o_ref, k_s, v_s, o_s) = refs
    else:
        (q_ref, ckv_ref, kp_ref, wk_ref, wv_ref, wo_ref, x_ref, mod_ref, o_ref, k_s, v_s, o_s) = refs

    def expand(ckv_of, kp_of, off, rows):
        rc = min(rows, 256)
        for r0 in range(0, rows, rc):
            c = ckv_of(r0, rc).astype(BF16)
            kp = kp_of(r0, rc).astype(F32)
            kn = _dot(c, wk_ref[...])
            v_s[off + r0:off + r0 + rc, :] = _dot(c, wv_ref[...]).astype(BF16)
            for hd in range(MLA_HEADS):
                sl = slice(hd * LANES, (hd + 1) * LANES)
                k_s[off + r0:off + r0 + rc, sl] = (kn[:, sl] + kp).astype(BF16)

    @pl.when(pl.program_id(1) == 0)
    def _():
        if n_cache:
            expand(lambda r, n: cckv_ref[0, r:r + n, :], lambda r, n: ckp_ref[0, r:r + n, :], 0, n_cache)
        expand(lambda r, n: ckv_ref[r:r + n, :], lambda r, n: kp_ref[r:r + n, :], n_cache, n_lat)

    def head(hd, v_pair):
        sl = slice(hd * LANES, (hd + 1) * LANES)
        s = _dot_nt(q_ref[:, sl], k_s[:, sl])
        m = jnp.max(s, axis=-1, keepdims=True)
        p = jnp.exp(s - m)
        l = jnp.sum(p, axis=-1, keepdims=True)
        return _dot(p.astype(BF16), v_pair) / l

    low_half = lax.broadcasted_iota(jnp.int32, (q_ref.shape[0], LANES), 1) < MLA_V
    for hp in range(MLA_HEADS // 2):
        sl = slice(hp * LANES, (hp + 1) * LANES)
        v_pair = v_s[:, sl]
        o_s[:, sl] = jnp.where(low_half, head(2 * hp, v_pair), head(2 * hp + 1, v_pair)).astype(BF16)
    mix = _dot(o_s[...], wo_ref[...])
    o_ref[...] = x_ref[...] + mod_ref[0][2:3] * mix


def _mla_attn(x, mods, q, ckv, kp, wk_ext, wv_ext, wo_ext, cache=None):
    hw = MLA_HEADS * LANES
    if cache is None:
        n_lat, n_cache, nb, nq, base = SEQ, 0, BATCH, SEQ // TQ, 0
    else:
        n_lat, n_cache, nb, nq, base = DEC_SEQ, PAST_LEN, DEC_BATCH, DEC_SEQ // TQ, T_CTX // TQ
    seq_base = base * TQ // n_lat
    tile = lambda b, qi: (base + b * nq + qi, 0)
    seq = lambda b, qi: (seq_base + b, 0)
    seq_mode = dict(pipeline_mode=pl.Buffered(1)) if cache is not None else {}
    in_specs = [pl.BlockSpec((TQ, hw), tile),
                pl.BlockSpec((n_lat, MLA_KV_RANK), seq, **seq_mode),
                pl.BlockSpec((n_lat, LANES), seq, **seq_mode)]
    args = [q, ckv, kp]
    if cache is not None:
        in_specs += [pl.BlockSpec((1, n_cache, MLA_KV_RANK), lambda b, qi: (b, 0, 0)),
                     pl.BlockSpec((1, n_cache, LANES), lambda b, qi: (b, 0, 0))]
        args += list(cache)
    vw = MLA_HEADS * MLA_V
    in_specs += [_full((MLA_KV_RANK, hw)), _full((MLA_KV_RANK, vw)), _full((vw, D_MODEL)),
                 pl.BlockSpec((TQ, D_MODEL), tile),
                 pl.BlockSpec((1, 6, D_MODEL), lambda b, qi: (_cond_of_row((base + b * nq + qi) * TQ), 0, 0))]
    args += [wk_ext, wv_ext, wo_ext, x, mods]
    sk = n_cache + n_lat
    return pl.pallas_call(
        functools.partial(_mla_attn_kernel, n_lat=n_lat, n_cache=n_cache),
        grid=(nb, nq),
        in_specs=in_specs,
        out_specs=pl.BlockSpec((TQ, D_MODEL), tile),
        out_shape=jax.ShapeDtypeStruct((T_ALL, D_MODEL), F32),
        scratch_shapes=[pltpu.VMEM((sk, hw), BF16), pltpu.VMEM((sk, vw), BF16), pltpu.VMEM((TQ, vw), BF16)],
        input_output_aliases={len(args) - 2: 0},
        compiler_params=_params("arbitrary", "arbitrary"),
        name="mla_attn_lat" if cache is not None else "mla_attn_ctx",
    )(*args)


SWA_QW = SWA_HEADS * LANES
SWA_KW = SWA_KV_HEADS * LANES
SWA_CW = 2 * SWA_KV_HEADS * SWA_HEAD_DIM


def _swa_proj_kernel(*refs, rope):
    if rope:
        (x_ref, mod_ref, n1_ref, w_ref, b_ref, cq_ref, sq_ref, ck_ref, sk_ref, q_ref, k_ref, v_ref) = refs
    else:
        (x_ref, mod_ref, n1_ref, w_ref, b_ref, q_ref, k_ref, v_ref, kv_ref) = refs
    h = _modulated(x_ref[...], n1_ref[...], mod_ref[0], 0).astype(BF16)
    a = _dot(h, w_ref[...]) + b_ref[...]
    if rope:
        cq, sq, ck, sk = cq_ref[...], sq_ref[...], ck_ref[...], sk_ref[...]
        for hd in range(SWA_HEADS):
            sl = slice(hd * LANES, (hd + 1) * LANES)
            q_ref[:, sl] = _rope_slab(a[:, sl], cq, sq, SWA_HEAD_DIM // 2).astype(BF16)
        for hd in range(SWA_KV_HEADS):
            sl = slice(hd * LANES, (hd + 1) * LANES)
            k_ref[:, sl] = _rope_slab(a[:, SWA_QW + hd * LANES:SWA_QW + (hd + 1) * LANES], ck, sk,
                                      SWA_HEAD_DIM // 2).astype(BF16)
    else:
        q_ref[...] = (a[:, :SWA_QW] * SWA_SCALE).astype(BF16)
        k_ref[...] = a[:, SWA_QW:SWA_QW + SWA_KW].astype(BF16)
        kv_ref[...] = a[:, SWA_QW + 2 * SWA_KW:]
    v_ref[...] = a[:, SWA_QW + SWA_KW:SWA_QW + 2 * SWA_KW].astype(BF16)


def _swa_proj(x, mods, norm1, w_ext, b_ext, tabs, rope):
    base = T_CTX // TM if rope else 0
    n_tiles = (T_LAT if rope else T_CTX) // TM
    n_rows = n_tiles * TM
    ncols = w_ext.shape[1]
    row = lambda i: (i, 0)
    in_specs = [pl.BlockSpec((TM, D_MODEL), lambda i: (i + base, 0)), _mod_spec(TM, base), _full((1, D_MODEL)),
                _full((D_MODEL, ncols)), _full((1, ncols))]
    args = [x, mods, norm1.reshape(1, -1), w_ext, b_ext]
    out_specs = [pl.BlockSpec((TM, SWA_QW), row), pl.BlockSpec((TM, SWA_KW), row), pl.BlockSpec((TM, SWA_KW), row)]
    out_shape = [jax.ShapeDtypeStruct((n_rows, SWA_QW), BF16), jax.ShapeDtypeStruct((n_rows, SWA_KW), BF16),
                 jax.ShapeDtypeStruct((n_rows, SWA_KW), BF16)]
    if rope:
        tab = pl.BlockSpec((TM, LANES), lambda i: (_pos_block(i + base, TM), 0))
        in_specs += [tab] * 4
        args += list(tabs)
    else:
        out_specs.append(pl.BlockSpec((TM, SWA_CW), row))
        out_shape.append(jax.ShapeDtypeStruct((n_rows, SWA_CW), F32))
    return pl.pallas_call(
        functools.partial(_swa_proj_kernel, rope=rope),
        grid=(n_tiles,),
        in_specs=in_specs, out_specs=out_specs, out_shape=out_shape,
        compiler_params=_params("arbitrary"),
        name="swa_proj_lat" if rope else "swa_proj_ctx",
    )(*args)


def _swa_ctx_kernel(q_ref, k_ref, v_ref, sink_ref, wo_ref, bo_ref, x_ref, mod_ref, o_ref, o_s):
    for n in range(SWA_KV_HEADS):
        kn = k_ref[:, n * LANES:(n + 1) * LANES]
        vn = v_ref[:, n * LANES:(n + 1) * LANES]
        for g in range(SWA_GROUP):
            j = n * SWA_GROUP + g
            sl = slice(j * LANES, (j + 1) * LANES)
            s = _dot_nt(q_ref[:, sl], kn)
            sink = sink_ref[j]
            m = jnp.maximum(jnp.max(s, axis=-1, keepdims=True), sink)
            p = jnp.exp(s - m)
            l = jnp.sum(p, axis=-1, keepdims=True) + jnp.exp(sink - m)
            o_s[:, sl] = (_dot(p.astype(BF16), vn) / l).astype(BF16)
    mix = _dot(o_s[...], wo_ref[...]) + bo_ref[...]
    o_ref[...] = x_ref[...] + mod_ref[0][2:3] * mix


def _swa_lat_kernel(q_ref, k_ref, v_ref, ck_ref, cv_ref, sink_ref, wo_ref, bo_ref, x_ref, mod_ref, o_ref, o_s):
    span = TQ + 2 * SWA_WINDOW
    start = pl.program_id(1) * TQ
    ks = pl.multiple_of(jnp.clip(start - SWA_WINDOW, 0, DEC_SEQ - span), SWA_WINDOW)
    qpos = start + lax.broadcasted_iota(jnp.int32, (TQ, span), 0)
    kpos = ks + lax.broadcasted_iota(jnp.int32, (TQ, span), 1)
    valid = jnp.abs(kpos - qpos) <= SWA_WINDOW
    for n in range(SWA_KV_HEADS):
        nl = slice(n * LANES, (n + 1) * LANES)
        kc = ck_ref[0, :, nl]
        vc = cv_ref[0, :, nl]
        kl = k_ref[pl.ds(ks, span), nl]
        vl = v_ref[pl.ds(ks, span), nl]
        for g in range(SWA_GROUP):
            j = n * SWA_GROUP + g
            sl = slice(j * LANES, (j + 1) * LANES)
            qj = q_ref[:, sl]
            s_c = _dot_nt(qj, kc)
            s_l = jnp.where(valid, _dot_nt(qj, kl), -jnp.inf)
            sink = sink_ref[j]
            m = jnp.maximum(jnp.maximum(jnp.max(s_c, axis=-1, keepdims=True),
                                        jnp.max(s_l, axis=-1, keepdims=True)), sink)
            p_c = jnp.exp(s_c - m)
            p_l = jnp.exp(s_l - m)
            l = (jnp.sum(p_c, axis=-1, keepdims=True) + jnp.sum(p_l, axis=-1, keepdims=True)
                 + jnp.exp(sink - m))
            o = _dot(p_c.astype(BF16), vc) + _dot(p_l.astype(BF16), vl)
            o_s[:, sl] = (o / l).astype(BF16)
    mix = _dot(o_s[...], wo_ref[...]) + bo_ref[...]
    o_ref[...] = x_ref[...] + mod_ref[0][2:3] * mix


def _swa_attn(x, mods, q, k, v, sink, wo_ext, bo, cache=None):
    smem = pl.BlockSpec(memory_space=pltpu.SMEM)
    if cache is None:
        nq, base = 1, 0
        tile = lambda b, qi: (b, 0)
        in_specs = [pl.BlockSpec((TQ, SWA_QW), tile), pl.BlockSpec((SEQ, SWA_KW), tile),
                    pl.BlockSpec((SEQ, SWA_KW), tile)]
        args = [q, k, v]
        kern, nb, name = _swa_ctx_kernel, BATCH, "swa_attn_ctx"
    else:
        nq, base = DEC_SEQ // TQ, T_CTX // TQ
        tile = lambda b, qi: (b * nq + qi, 0)
        seq = lambda b, qi: (b, 0)
        cspec = pl.BlockSpec((1, PAST_LEN, SWA_KW), lambda b, qi: (b, 0, 0))
        in_specs = [pl.BlockSpec((TQ, SWA_QW), tile), pl.BlockSpec((DEC_SEQ, SWA_KW), seq),
                    pl.BlockSpec((DEC_SEQ, SWA_KW), seq), cspec, cspec]
        args = [q, k, v, cache[0], cache[1]]
        kern, nb, name = _swa_lat_kernel, DEC_BATCH, "swa_attn_lat"
    xtile = lambda b, qi: (base + b * nq + qi, 0)
    in_specs += [smem, _full((SWA_QW, D_MODEL)), _full((1, D_MODEL)), pl.BlockSpec((TQ, D_MODEL), xtile),
                 pl.BlockSpec((1, 6, D_MODEL), lambda b, qi: (_cond_of_row((base + b * nq + qi) * TQ), 0, 0))]
    args += [sink, wo_ext, bo.reshape(1, -1), x, mods]
    return pl.pallas_call(
        kern,
        grid=(nb, nq),
        in_specs=in_specs,
        out_specs=pl.BlockSpec((TQ, D_MODEL), xtile),
        out_shape=jax.ShapeDtypeStruct((T_ALL, D_MODEL), F32),
        scratch_shapes=[pltpu.VMEM((TQ, SWA_QW), BF16)],
        input_output_aliases={len(args) - 2: 0},
        compiler_params=_params("arbitrary", "arbitrary"),
        name=name,
    )(*args)


CONV_EXT = CONV_TM + 2 * CONV_HALO


def _conv_kernel(xp_ref, x_ref, xn_ref, mod_ref, n1_ref, w1_ref, b1_ref, dw_ref, dwb_ref, lng_ref, lnb_ref,
                 w2_ref, b2_ref, o_ref, ext_s, y_s):
    i = pl.program_id(0)
    tiles_per_seq = DEC_SEQ // CONV_TM
    j = (i - T_CTX // CONV_TM) % tiles_per_seq
    latent = i >= T_CTX // CONV_TM
    left_ok = jnp.logical_and(latent, j > 0)
    right_ok = jnp.logical_and(latent, j < tiles_per_seq - 1)
    mod = mod_ref[0]

    def glu_of(xv):
        h = _modulated(xv, n1_ref[...], mod, 0).astype(BF16)
        u = _dot(h, w1_ref[...]) + b1_ref[...]
        return u[:, :D_MODEL] * _sigmoid(u[:, D_MODEL:])

    ext_s[CONV_HALO:CONV_HALO + CONV_TM, :] = glu_of(x_ref[...])
    ext_s[0:CONV_HALO, :] = jnp.where(left_ok, glu_of(xp_ref[...]), 0.0)
    ext_s[CONV_HALO + CONV_TM:, :] = jnp.where(right_ok, glu_of(xn_ref[...]), 0.0)

    rows = 128
    sub = 8
    first = CONV_HALO - CONV_PAD
    span = rows + (first + CONV_WIDTH - 1) // sub * sub
    for c in range(D_MODEL // LANES):
        cl = slice(c * LANES, (c + 1) * LANES)
        dwc = dw_ref[:, cl]
        for r in range(CONV_TM // rows):
            window = ext_s[r * rows:r * rows + span + sub, cl]
            acc = jnp.zeros((rows, LANES), F32)
            for res in range(sub):
                shifted = window if res == 0 else pltpu.roll(window, span + sub - res, 0)
                for q in range(span // sub - rows // sub + 1):
                    w = q * sub + res - first
                    if 0 <= w < CONV_WIDTH:
                        acc = acc + shifted[q * sub:q * sub + rows] * dwc[w:w + 1]
            y_s[r * rows:(r + 1) * rows, cl] = acc
    y = y_s[...] + dwb_ref[...]
    mu = jnp.mean(y, axis=-1, keepdims=True)
    yc = y - mu
    var = jnp.mean(yc * yc, axis=-1, keepdims=True)
    yn = yc * lax.rsqrt(var + NORM_EPS) * lng_ref[...] + lnb_ref[...]
    z = (yn * _sigmoid(yn)).astype(BF16)
    mix = _dot(z, w2_ref[...]) + b2_ref[...]
    o_ref[...] = x_ref[...] + mod[2:3] * mix


def _conv_layer(x, mods, norm1, w1, b1, dw, dwb, ln_g, ln_b, w2, b2):
    n_tiles = T_ALL // CONV_TM
    per = CONV_TM // CONV_HALO
    n_halo_blocks = T_ALL // CONV_HALO
    row = lambda i: (i, 0)
    vec = lambda v: v.reshape(1, -1)
    return pl.pallas_call(
        _conv_kernel,
        grid=(n_tiles,),
        in_specs=[pl.BlockSpec((CONV_HALO, D_MODEL), lambda i: (jnp.maximum(i * per - 1, 0), 0)),
                  pl.BlockSpec((CONV_TM, D_MODEL), row),
                  pl.BlockSpec((CONV_HALO, D_MODEL), lambda i: (jnp.minimum((i + 1) * per, n_halo_blocks - 1), 0)),
                  _mod_spec(CONV_TM), _full((1, D_MODEL)),
                  _full((D_MODEL, 2 * D_MODEL)), _full((1, 2 * D_MODEL)),
                  _full((CONV_WIDTH, D_MODEL)), _full((1, D_MODEL)), _full((1, D_MODEL)), _full((1, D_MODEL)),
                  _full((D_MODEL, D_MODEL)), _full((1, D_MODEL))],
        out_specs=pl.BlockSpec((CONV_TM, D_MODEL), row),
        out_shape=jax.ShapeDtypeStruct((T_ALL, D_MODEL), F32),
        scratch_shapes=[pltpu.VMEM((CONV_EXT, D_MODEL), F32), pltpu.VMEM((CONV_TM, D_MODEL), F32)],
        compiler_params=_params("arbitrary"),
        name="conv_module",
    )(x, x, x, mods, vec(norm1), w1.astype(BF16), vec(b1), dw, vec(dwb), vec(ln_g), vec(ln_b),
      w2.astype(BF16), vec(b2))


def _router_kernel(x_ref, mod_ref, n2_ref, rw_ref, rb_ref, h_ref, idx_ref, gate_ref, cnt_ref):
    h = _modulated(x_ref[...], n2_ref[...], mod_ref[0], 3)
    _store_token_tiles(h_ref, h)
    cur = _dot_nt(rw_ref[...], h.astype(BF16)) + rb_ref[...]
    e_iota = lax.broadcasted_iota(jnp.int32, cur.shape, 0)
    vals, idxs = [], []
    for _ in range(TOP_K):
        m = jnp.max(cur, axis=0, keepdims=True)
        idx = jnp.min(jnp.where(cur == m, e_iota, N_EXPERTS), axis=0, keepdims=True)
        vals.append(m)
        idxs.append(idx)
        cur = jnp.where(e_iota == idx, -jnp.inf, cur)
    ex = [jnp.exp(v - vals[0]) for v in vals]
    tot = ex[0] + ex[1] + ex[2] + ex[3]
    for k in range(TOP_K):
        idx_ref[k:k + 1, :] = idxs[k]
        gate_ref[k:k + 1, :] = ex[k] / tot

    picked = jnp.sum(jnp.where(cur == -jnp.inf, 1.0, 0.0), axis=1, keepdims=True)

    @pl.when(pl.program_id(0) == 0)
    def _():
        cnt_ref[...] = jnp.zeros_like(cnt_ref)

    cnt_ref[...] += picked


def _router(x, mods, norm2, router_w, router_b):
    n_tiles = T_ALL // TM
    row = lambda i: (i, 0)
    col = lambda i: (0, i)
    return pl.pallas_call(
        _router_kernel,
        grid=(n_tiles,),
        in_specs=[pl.BlockSpec((TM, D_MODEL), row), _mod_spec(TM), _full((1, D_MODEL)),
                  _full((N_EXPERTS, D_MODEL)), _full((N_EXPERTS, 1))],
        out_specs=[pl.BlockSpec((TM * TOK_ROWS, LANES), row), pl.BlockSpec((TOP_K, TM), col),
                   pl.BlockSpec((TOP_K, TM), col), pl.BlockSpec((N_EXPERTS, 1), lambda i: (0, 0))],
        out_shape=[jax.ShapeDtypeStruct((T_ALL * TOK_ROWS, LANES), F32),
                   jax.ShapeDtypeStruct((TOP_K, T_ALL), jnp.int32),
                   jax.ShapeDtypeStruct((TOP_K, T_ALL), F32),
                   jax.ShapeDtypeStruct((N_EXPERTS, 1), F32)],
        compiler_params=_params("arbitrary"),
        name="router",
    )(x, mods, norm2.reshape(1, -1), router_w.T.astype(BF16), router_b.reshape(-1, 1))


N_ASSIGN = TOP_K * T_ALL
RANK_CHUNK = 256
RANK_BLOCK = 2048


def _block_layout(counts):
    counts = counts.reshape(N_EXPERTS).astype(jnp.int32)
    padded = (counts + MOE_BM - 1) // MOE_BM * MOE_BM
    pad_end = jnp.cumsum(padded)
    pad_start = pad_end - padded
    starts = jnp.arange(MOE_BLOCKS + 1, dtype=jnp.int32) * MOE_BM
    e_ids = jnp.arange(N_EXPERTS, dtype=jnp.int32)
    used = padded > 0
    last_used = jnp.max(jnp.where(used, e_ids, 0))
    block_e = jnp.minimum(jnp.sum((pad_end[None, :] <= starts[:, None]).astype(jnp.int32), axis=1), last_used)
    n_used = (pad_end[-1] // MOE_BM).reshape(1)
    run = jnp.cumsum(used.astype(jnp.int32)) - 1
    later_used = jnp.where(jnp.logical_and(used[None, :], e_ids[None, :] > e_ids[:, None]), e_ids[None, :], N_EXPERTS)
    next_used = jnp.min(later_used, axis=1)
    next_used = jnp.where(next_used == N_EXPERTS, e_ids, next_used)
    block_plan = (block_e, next_used[block_e], run[block_e] & 1, n_used)
    return pad_start.astype(F32).reshape(N_EXPERTS, 1), block_plan


def _rank_kernel(idx_ref, ps_ref, u_ref, dest_ref, carry):
    @pl.when(pl.program_id(0) == 0)
    def _():
        carry[...] = ps_ref[...]

    e_iota = lax.broadcasted_iota(jnp.int32, (N_EXPERTS, RANK_CHUNK), 0)
    for c in range(RANK_BLOCK // RANK_CHUNK):
        sl = slice(c * RANK_CHUNK, (c + 1) * RANK_CHUNK)
        hit = e_iota == idx_ref[:, sl]
        pref = _dot(jnp.where(hit, 1.0, 0.0).astype(BF16), u_ref[...])
        base = carry[...]
        dest = jnp.sum(jnp.where(hit, pref + base, 0.0), axis=0, keepdims=True) - 1.0
        dest_ref[:, sl] = dest.astype(jnp.int32)
        carry[...] = base + pref[:, RANK_CHUNK - 1:RANK_CHUNK]


def _assignment_slots(idx_t, pad_start):
    tri = jnp.asarray(np.triu(np.ones((RANK_CHUNK, RANK_CHUNK), np.float32)), dtype=BF16)
    blk = pl.BlockSpec((1, RANK_BLOCK), lambda i: (0, i))
    return pl.pallas_call(
        _rank_kernel,
        grid=(N_ASSIGN // RANK_BLOCK,),
        in_specs=[blk, _full((N_EXPERTS, 1)), _full((RANK_CHUNK, RANK_CHUNK))],
        out_specs=blk,
        out_shape=jax.ShapeDtypeStruct((1, N_ASSIGN), jnp.int32),
        scratch_shapes=[pltpu.VMEM((N_EXPERTS, 1), F32)],
        compiler_params=_params("arbitrary"),
        name="moe_rank",
    )(idx_t.reshape(1, N_ASSIGN), pad_start, tri)


INV_UNROLL = 16
INV_CHUNK = 4096
INV_STEPS = N_ASSIGN // INV_CHUNK
N_SLOTS = MOE_CAP + MOE_BM


def _inverse_kernel(dest_ref, pad_hbm, slot_ref, sem):
    i = pl.program_id(0)

    @pl.when(i == 0)
    def _():
        fill = pltpu.make_async_copy(pad_hbm, slot_ref, sem)
        fill.start()
        fill.wait()

    base = i * INV_CHUNK

    def place(j, carry):
        o0 = j * INV_UNROLL
        batch = dest_ref.at[0, 0, pl.ds(o0, INV_UNROLL)]
        dests = [batch[u] for u in range(INV_UNROLL)]
        for u in range(INV_UNROLL):
            slot_ref[dests[u]] = base + o0 + u
        return carry

    lax.fori_loop(0, INV_CHUNK // INV_UNROLL, place, 0)


def _slot_assignments(dest):
    s = np.arange(N_SLOTS, dtype=np.int32)
    pad_ids = jnp.asarray(N_ASSIGN + (s & (MOE_BM - 1)) + np.where(s >= MOE_CAP, MOE_BM, 0).astype(np.int32))
    return pl.pallas_call(
        _inverse_kernel,
        grid=(INV_STEPS,),
        in_specs=[pl.BlockSpec((1, 1, INV_CHUNK), lambda i: (i, 0, 0), memory_space=pltpu.SMEM),
                  pl.BlockSpec(memory_space=pl.ANY)],
        out_specs=pl.BlockSpec(memory_space=pltpu.SMEM),
        out_shape=jax.ShapeDtypeStruct((N_SLOTS,), jnp.int32),
        scratch_shapes=[pltpu.SemaphoreType.DMA(())],
        compiler_params=_params("arbitrary"),
        name="moe_slots",
    )(dest.reshape(INV_STEPS, 1, INV_CHUNK), pad_ids)


YG_TOKENS = N_ASSIGN + 2 * MOE_BM
BLOCK_TILE_ROWS = MOE_BM * TOK_ROWS
MOE_CHUNK = 256
ACT_BUF = 3


def _moe_kernel(be_ref, ne_ref, par_ref, nu_ref, sa_prev_ref, sa_ref, sa_next_ref, sa_next2_ref, h_hbm,
                wg_hbm, wu_hbm, wd_hbm, bg_ref, bu_ref, bd_ref, yg_hbm,
                xbuf, ybuf, x_s, wg_f, wu_f, wd_f, wgu_s, wd_s, gsem, ssem, wsem):
    i = pl.program_id(0)
    n_used = nu_ref[0]
    cur = i % 2
    nxt = 1 - cur
    gcur = lax.rem(i, 3)
    gnext = lax.rem(i + 1, 3)
    gnext2 = lax.rem(i + 2, 3)

    def weight_copies(e, slot):
        return [pltpu.make_async_copy(src.at[e], dst.at[slot], wsem.at[slot])
                for src, dst in ((wg_hbm, wg_f), (wu_hbm, wu_f), (wd_hbm, wd_f))]

    def start_gather(sa, buf, lo=0, hi=MOE_BM):
        for r in range(lo, hi):
            tok = sa[0, 0, r] & (T_ALL - 1)
            pltpu.make_async_copy(h_hbm.at[pl.ds(tok * TOK_ROWS, TOK_ROWS)],
                                  xbuf.at[buf, pl.ds(r * TOK_ROWS, TOK_ROWS)], gsem.at[buf]).start()

    def start_scatter(sa, buf):
        for r in range(MOE_BM):
            pltpu.async_copy(ybuf.at[buf, pl.ds(r * TOK_ROWS, TOK_ROWS)],
                             yg_hbm.at[pl.ds(sa[0, 0, r] * TOK_ROWS, TOK_ROWS)], ssem.at[buf], priority=1)

    def wait_gather(buf):
        pltpu.make_async_copy(h_hbm.at[pl.ds(0, BLOCK_TILE_ROWS)], xbuf.at[buf], gsem.at[buf]).wait()

    def wait_scatter(buf):
        pltpu.make_async_copy(ybuf.at[buf], yg_hbm.at[pl.ds(0, BLOCK_TILE_ROWS)], ssem.at[buf]).wait()

    @pl.when(i == 0)
    def _():
        for cp in weight_copies(be_ref[0], 0):
            cp.start()
        start_gather(sa_ref, 0)
        start_gather(sa_next_ref, 1)
        ybuf[...] = jnp.zeros_like(ybuf)
        pltpu.make_async_copy(ybuf.at[0], yg_hbm.at[pl.ds(N_ASSIGN * TOK_ROWS, BLOCK_TILE_ROWS)], ssem.at[0]).start()

    @pl.when(i <= n_used)
    def _():
        e = be_ref[i]

        @pl.when(jnp.logical_or(i == 0, e != be_ref[jnp.maximum(i - 1, 0)]))
        def _():
            slot = par_ref[i]
            for cp in weight_copies(e, slot):
                cp.wait()
            for c in range(D_MODEL // MOE_CHUNK):
                cs = slice(c * MOE_CHUNK, (c + 1) * MOE_CHUNK)
                wgu_s[:, 2 * c * MOE_CHUNK:(2 * c + 1) * MOE_CHUNK] = wg_f[slot, :, cs].astype(BF16)
                wgu_s[:, (2 * c + 1) * MOE_CHUNK:(2 * c + 2) * MOE_CHUNK] = wu_f[slot, :, cs].astype(BF16)
            wd_s[...] = wd_f[slot].astype(BF16)

            @pl.when(ne_ref[i] != e)
            def _():
                for cp in weight_copies(ne_ref[i], 1 - slot):
                    cp.start()

        wait_gather(gcur)
        x_s[...] = _load_token_tiles(xbuf.at[gcur], MOE_BM).astype(BF16)
        n_chunks = D_MODEL // MOE_CHUNK
        slabs = MOE_CHUNK // LANES
        for c in range(n_chunks):
            cs = slice(c * MOE_CHUNK, (c + 1) * MOE_CHUNK)
            start_gather(sa_next2_ref, gnext2, c * MOE_BM // n_chunks, (c + 1) * MOE_BM // n_chunks)
            gu = _dot(x_s[...], wgu_s[:, 2 * c * MOE_CHUNK:(2 * c + 2) * MOE_CHUNK])
            g = jnp.minimum(gu[:, :MOE_CHUNK] + bg_ref[0, :, cs], SWIGLU_LIMIT)
            u = jnp.clip(gu[:, MOE_CHUNK:] + bu_ref[0, :, cs], -SWIGLU_LIMIT, SWIGLU_LIMIT)
            act = (u + 1.0) * (g * _sigmoid(SWIGLU_ALPHA * g))
            for j in range(slabs):
                jj = c * slabs + j
                xbuf[ACT_BUF, jj * MOE_BM:(jj + 1) * MOE_BM, :] = act[:, j * LANES:(j + 1) * LANES]
        wait_scatter(cur)
        start_scatter(sa_prev_ref, nxt)
        act_b = jnp.concatenate([xbuf[ACT_BUF, j * MOE_BM:(j + 1) * MOE_BM, :] for j in range(TOK_ROWS)],
                                axis=-1).astype(BF16)
        half = D_MODEL // 2
        for c in range(2):
            y = _dot(act_b, wd_s[:, c * half:(c + 1) * half]) + bd_ref[0, :, c * half:(c + 1) * half]
            for j in range(TOK_ROWS // 2):
                jj = c * (TOK_ROWS // 2) + j
                ybuf.at[cur][pl.ds(jj, MOE_BM, stride=TOK_ROWS), :] = y[:, j * LANES:(j + 1) * LANES]

    @pl.when(i == n_used)
    def _():
        wait_gather(gnext)
        wait_gather(gnext2)
        wait_scatter(nxt)


def _moe_experts(h, slots, block_plan, wg, bg, wu, bu, wd, bd):
    bspec = pl.BlockSpec((1, 1, D_MODEL), lambda i, be, *_: (be[i], 0, 0))
    sspec = lambda f: pl.BlockSpec((1, 1, MOE_BM), lambda i, *_: (f(i), 0, 0), memory_space=pltpu.SMEM)
    hbm = pl.BlockSpec(memory_space=pl.ANY)
    b3 = lambda b: b.reshape(N_EXPERTS, 1, D_MODEL)
    slots3 = slots.reshape(MOE_BLOCKS + 1, 1, MOE_BM)
    wbuf = pltpu.VMEM((2, D_MODEL, D_MODEL), F32)
    return pl.pallas_call(
        _moe_kernel,
        grid_spec=pltpu.PrefetchScalarGridSpec(
            num_scalar_prefetch=4,
            grid=(MOE_BLOCKS + 1,),
            in_specs=[sspec(lambda i: jnp.where(i == 0, MOE_BLOCKS, i - 1)), sspec(lambda i: i),
                      sspec(lambda i: jnp.minimum(i + 1, MOE_BLOCKS)), sspec(lambda i: jnp.minimum(i + 2, MOE_BLOCKS)),
                      hbm, hbm, hbm, hbm, bspec, bspec, bspec],
            out_specs=hbm,
            scratch_shapes=[pltpu.VMEM((ACT_BUF + 1, BLOCK_TILE_ROWS, LANES), F32),
                            pltpu.VMEM((2, BLOCK_TILE_ROWS, LANES), F32),
                            pltpu.VMEM((MOE_BM, D_MODEL), BF16),
                            wbuf, wbuf, wbuf,
                            pltpu.VMEM((D_MODEL, 2 * D_MODEL), BF16), pltpu.VMEM((D_MODEL, D_MODEL), BF16),
                            pltpu.SemaphoreType.DMA((3,)), pltpu.SemaphoreType.DMA((2,)),
                            pltpu.SemaphoreType.DMA((2,))]),
        out_shape=jax.ShapeDtypeStruct((YG_TOKENS * TOK_ROWS, LANES), F32),
        compiler_params=_params("arbitrary"),
        name="moe_experts",
    )(*block_plan, slots3, slots3, slots3, slots3, h, wg, wu, wd, b3(bg), b3(bu), b3(bd))


def _combine_kernel(*refs, final):
    if final:
        x_ref, y0_ref, y1_ref, y2_ref, y3_ref, gate_ref, mod_ref, fg_ref, o_ref = refs
    else:
        x_ref, y0_ref, y1_ref, y2_ref, y3_ref, gate_ref, mod_ref, o_ref = refs
    gate = gate_ref[...]
    n = x_ref.shape[0]
    g2 = mod_ref[0][5:6]
    sumsq = jnp.zeros((n, 1), F32)
    for j in range(TOK_ROWS):
        rows = pl.ds(j, n, stride=TOK_ROWS)
        sl = slice(j * LANES, (j + 1) * LANES)
        acc = y0_ref[rows, :] * gate[:, 0:1]
        for k, y_ref in enumerate((y1_ref, y2_ref, y3_ref), start=1):
            acc = acc + y_ref[rows, :] * gate[:, k:k + 1]
        xn = x_ref[:, sl] + g2[:, sl] * acc
        o_ref[:, sl] = xn
        if final:
            sumsq = sumsq + jnp.sum(xn * xn, axis=-1, keepdims=True)
    if final:
        inv = lax.rsqrt(sumsq * (1.0 / D_MODEL) + NORM_EPS)
        o_ref[...] = o_ref[...] * inv * fg_ref[...]


def _combine(x, yg, gates, mods, final_g=None, base_rows=0, n_rows=T_ALL):
    final = final_g is not None
    base = base_rows // TM
    n_all = T_ALL // TM
    row = lambda i: (i + base, 0)
    ysp = lambda k: pl.BlockSpec((TM * TOK_ROWS, LANES), lambda i: (k * n_all + i + base, 0))
    in_specs = [pl.BlockSpec((TM, D_MODEL), row), ysp(0), ysp(1), ysp(2), ysp(3),
                pl.BlockSpec((TM, TOP_K), row), _mod_spec(TM, base)]
    args = [x, yg, yg, yg, yg, gates, mods]
    if final:
        in_specs.append(_full((1, D_MODEL)))
        args.append(final_g.reshape(1, -1))
    return pl.pallas_call(
        functools.partial(_combine_kernel, final=final),
        grid=(n_rows // TM,),
        in_specs=in_specs,
        out_specs=pl.BlockSpec((TM, D_MODEL), (lambda i: (i, 0)) if final else row),
        out_shape=jax.ShapeDtypeStruct((n_rows, D_MODEL), F32),
        input_output_aliases={} if final else {0: 0},
        compiler_params=_params("arbitrary"),
        name="moe_combine_final" if final else "moe_combine",
    )(*args)


def _moe_layer(x, mods, norm2, router_w, router_b, wg, bg, wu, bu, wd, bd, final_g=None):
    h, idx_t, gate_t, counts = _router(x, mods, norm2, router_w, router_b)
    pad_start, block_plan = _block_layout(counts)
    slots = _slot_assignments(_assignment_slots(idx_t, pad_start))
    yg = _moe_experts(h, slots, block_plan, wg, bg, wu, bu, wd, bd)
    gates = gate_t.T
    if final_g is None:
        return _combine(x, yg, gates, mods)
    return (_combine(x, yg, gates, mods, final_g, 0, T_CTX),
            _combine(x, yg, gates, mods, final_g, T_CTX, T_LAT))


def _pad_lanes(w, lead_pad, width):
    tail = width - lead_pad - w.shape[-1]
    cfg = [(0, 0)] * (w.ndim - 1) + [(lead_pad, tail)]
    return jnp.pad(w, cfg)


def _mla_weights(wa, wuq, wukv, wo):
    krope_w = wa[:, MLA_Q_RANK + MLA_KV_RANK:]
    wa_ext = jnp.concatenate([wa[:, :MLA_Q_RANK + MLA_KV_RANK],
                              _pad_lanes(krope_w, MLA_ROPE_OFF, LANES),
                              _pad_lanes(krope_w, 0, LANES)], axis=1).astype(BF16)
    q3 = wuq.reshape(MLA_Q_RANK, MLA_HEADS, MLA_NOPE + MLA_ROPE)
    wuq_ext = jnp.concatenate([q3[..., :MLA_NOPE], _pad_lanes(q3[..., MLA_NOPE:], MLA_ROPE // 2, LANES - MLA_NOPE)],
                              axis=-1).reshape(MLA_Q_RANK, MLA_HEADS * LANES).astype(BF16)
    kv3 = wukv.reshape(MLA_KV_RANK, MLA_HEADS, MLA_NOPE + MLA_V)
    wk_ext = _pad_lanes(kv3[..., :MLA_NOPE], 0, LANES).reshape(MLA_KV_RANK, MLA_HEADS * LANES).astype(BF16)
    wv = kv3[..., MLA_NOPE:].reshape(MLA_KV_RANK, MLA_HEADS * MLA_V).astype(BF16)
    return wa_ext, wuq_ext, wk_ext, wv, wo.astype(BF16)


def _swa_weights(wqkv, bqkv, wo):
    nq = SWA_HEADS * SWA_HEAD_DIM
    nkv = SWA_KV_HEADS * SWA_HEAD_DIM

    def slabs(w, heads):
        lead = w.shape[:-1]
        return _pad_lanes(w.reshape(lead + (heads, SWA_HEAD_DIM)), 0, LANES).reshape(lead + (heads * LANES,))

    def ext(w, with_compact):
        parts = [slabs(w[..., :nq], SWA_HEADS), slabs(w[..., nq:nq + nkv], SWA_KV_HEADS),
                 slabs(w[..., nq + nkv:], SWA_KV_HEADS)]
        if with_compact:
            parts.append(w[..., nq:])
        return jnp.concatenate(parts, axis=-1)

    b2 = bqkv.reshape(1, -1)
    wo3 = wo.reshape(SWA_HEADS, SWA_HEAD_DIM, D_MODEL)
    wo_ext = jnp.pad(wo3, ((0, 0), (0, LANES - SWA_HEAD_DIM), (0, 0))).reshape(SWA_QW, D_MODEL).astype(BF16)
    return (ext(wqkv, True).astype(BF16), ext(b2, True), ext(wqkv, False).astype(BF16), ext(b2, False), wo_ext)


def kernel(x_prompt, x_sample, cache_l0_ckv, cache_l0_krope, cache_l1_k, cache_l1_v, cache_l3_ckv, cache_l3_krope, c, c_ctx, l0_mla_wa, l0_mla_q_norm, l0_mla_wuq, l0_mla_kv_norm, l0_mla_wukv, l0_mla_wo, l0_mod_w, l0_mod_b, l0_norm1, l0_norm2, l0_router_w, l0_router_b, l0_moe_wg, l0_moe_bg, l0_moe_wu, l0_moe_bu, l0_moe_wd, l0_moe_bd, l1_swa_wqkv, l1_swa_bqkv, l1_swa_sink, l1_swa_wo, l1_swa_bo, l1_mod_w, l1_mod_b, l1_norm1, l1_norm2, l1_router_w, l1_router_b, l1_moe_wg, l1_moe_bg, l1_moe_wu, l1_moe_bu, l1_moe_wd, l1_moe_bd, l2_conv_w1, l2_conv_b1, l2_conv_dw, l2_conv_dwb, l2_conv_ln_g, l2_conv_ln_b, l2_conv_w2, l2_conv_b2, l2_mod_w, l2_mod_b, l2_norm1, l2_norm2, l2_router_w, l2_router_b, l2_moe_wg, l2_moe_bg, l2_moe_wu, l2_moe_bu, l2_moe_wd, l2_moe_bd, l3_mla_wa, l3_mla_q_norm, l3_mla_wuq, l3_mla_kv_norm, l3_mla_wukv, l3_mla_wo, l3_mod_w, l3_mod_b, l3_norm1, l3_norm2, l3_router_w, l3_router_b, l3_moe_wg, l3_moe_bg, l3_moe_wu, l3_moe_bu, l3_moe_wd, l3_moe_bd, final_norm):
    x = jnp.concatenate([x_prompt.reshape(T_CTX, D_MODEL), x_sample.reshape(T_LAT, D_MODEL)], axis=0)
    cond = jnp.concatenate([c, c_ctx[None, :], jnp.zeros((N_COND - DEC_BATCH - 1, D_MODEL), F32)], axis=0)

    mla_c, mla_s = _rope_tables(MLA_ROPE, MLA_ROPE_LEAD, TM)
    mla_tabs = (mla_c * MLA_SCALE, mla_s * MLA_SCALE, mla_c, mla_s)
    swa_c, swa_s = _rope_tables(SWA_HEAD_DIM, 0, TM)
    swa_tabs = (swa_c * SWA_SCALE, swa_s * SWA_SCALE, swa_c, swa_s)

    def mla_layer(x, mods, norm1, wa, q_norm, wuq, kv_norm, wukv, wo, cache_ckv, cache_krope):
        wa_ext, wuq_ext, wk_ext, wv_ext, wo_ext = _mla_weights(wa, wuq, wukv, wo)
        q, ckv, kp, kst = _mla_proj(x, mods, norm1, wa_ext, q_norm, wuq_ext, kv_norm, mla_tabs)
        x = _mla_attn(x, mods, q, ckv, kp, wk_ext, wv_ext, wo_ext)
        cache = (cache_ckv, _pad_lanes(cache_krope, MLA_ROPE_OFF, LANES).astype(BF16))
        x = _mla_attn(x, mods, q, ckv, kp, wk_ext, wv_ext, wo_ext, cache)
        return x, (ckv[:T_CTX].reshape(BATCH, SEQ, MLA_KV_RANK), kst[:T_CTX].reshape(BATCH, SEQ, MLA_ROPE))

    def moe(x, mods, norm2, rw, rb, wg, bg, wu, bu, wd, bd):
        return _moe_layer(x, mods, norm2, rw, rb, wg, bg, wu, bu, wd, bd)

    mods = _modulation(cond, l0_mod_w, l0_mod_b)
    x, (st0_ckv, st0_krope) = mla_layer(x, mods, l0_norm1, l0_mla_wa, l0_mla_q_norm, l0_mla_wuq, l0_mla_kv_norm,
                                        l0_mla_wukv, l0_mla_wo, cache_l0_ckv, cache_l0_krope)
    x = moe(x, mods, l0_norm2, l0_router_w, l0_router_b, l0_moe_wg, l0_moe_bg, l0_moe_wu, l0_moe_bu,
            l0_moe_wd, l0_moe_bd)

    mods = _modulation(cond, l1_mod_w, l1_mod_b)
    w_ctx, b_ctx, w_lat, b_lat, swa_wo_ext = _swa_weights(l1_swa_wqkv, l1_swa_bqkv, l1_swa_wo)
    q_c, k_c, v_c, kv_c = _swa_proj(x, mods, l1_norm1, w_ctx, b_ctx, None, rope=False)
    q_l, k_l, v_l = _swa_proj(x, mods, l1_norm1, w_lat, b_lat, swa_tabs, rope=True)
    x = _swa_attn(x, mods, q_c, k_c, v_c, l1_swa_sink, swa_wo_ext, l1_swa_bo)
    nkv = SWA_KV_HEADS * SWA_HEAD_DIM
    cache_k = _pad_lanes(cache_l1_k, 0, LANES).reshape(DEC_BATCH, PAST_LEN, SWA_KW).astype(BF16)
    cache_v = _pad_lanes(cache_l1_v, 0, LANES).reshape(DEC_BATCH, PAST_LEN, SWA_KW).astype(BF16)
    x = _swa_attn(x, mods, q_l, k_l, v_l, l1_swa_sink, swa_wo_ext, l1_swa_bo, (cache_k, cache_v))
    st1_k = kv_c[:, :nkv].reshape(BATCH, SEQ, SWA_KV_HEADS, SWA_HEAD_DIM)
    st1_v = kv_c[:, nkv:].reshape(BATCH, SEQ, SWA_KV_HEADS, SWA_HEAD_DIM)
    x = moe(x, mods, l1_norm2, l1_router_w, l1_router_b, l1_moe_wg, l1_moe_bg, l1_moe_wu, l1_moe_bu,
            l1_moe_wd, l1_moe_bd)

    mods = _modulation(cond, l2_mod_w, l2_mod_b)
    x = _conv_layer(x, mods, l2_norm1, l2_conv_w1, l2_conv_b1, l2_conv_dw, l2_conv_dwb, l2_conv_ln_g,
                    l2_conv_ln_b, l2_conv_w2, l2_conv_b2)
    x = moe(x, mods, l2_norm2, l2_router_w, l2_router_b, l2_moe_wg, l2_moe_bg, l2_moe_wu, l2_moe_bu,
            l2_moe_wd, l2_moe_bd)

    mods = _modulation(cond, l3_mod_w, l3_mod_b)
    x, (st3_ckv, st3_krope) = mla_layer(x, mods, l3_norm1, l3_mla_wa, l3_mla_q_norm, l3_mla_wuq, l3_mla_kv_norm,
                                        l3_mla_wukv, l3_mla_wo, cache_l3_ckv, cache_l3_krope)
    y_ctx, y_lat = _moe_layer(x, mods, l3_norm2, l3_router_w, l3_router_b, l3_moe_wg, l3_moe_bg, l3_moe_wu,
                              l3_moe_bu, l3_moe_wd, l3_moe_bd, final_g=final_norm)
    y_prompt = y_ctx.reshape(BATCH, SEQ, D_MODEL)
    y_sample = y_lat.reshape(DEC_BATCH, DEC_SEQ, D_MODEL)
    return (y_prompt, y_sample, st0_ckv, st0_krope, st1_k, st1_v, st3_ckv, st3_krope)
```

```python
import functools

import jax
import jax.numpy as jnp
import numpy as np
from jax import lax
from jax.experimental import pallas as pl
from jax.experimental.pallas import tpu as pltpu

F32 = jnp.float32
BF16 = jnp.bfloat16

D_MODEL = 1024
BATCH = 32
SEQ = 256
DEPTH = 4
DEC_BATCH = 4
DEC_SEQ = 2048
PAST_LEN = 256
GRID_W = 64
NORM_EPS = 1e-6
ROPE_THETA = 10000.0

MLA_HEADS = 16
MLA_NOPE = 64
MLA_ROPE = 32
MLA_V = 64
MLA_Q_RANK = 384
MLA_KV_RANK = 256
MLA_SCALE = (MLA_NOPE + MLA_ROPE) ** -0.5

SWA_HEADS = 16
SWA_KV_HEADS = 4
SWA_GROUP = 4
SWA_HEAD_DIM = 64
SWA_WINDOW = 128
SWA_SCALE = SWA_HEAD_DIM ** -0.5

CONV_WIDTH = 31
CONV_PAD = CONV_WIDTH // 2

N_EXPERTS = 32
TOP_K = 4
SWIGLU_ALPHA = 1.702
SWIGLU_LIMIT = 7.0

LANES = 128
T_CTX = BATCH * SEQ
T_LAT = DEC_BATCH * DEC_SEQ
T_ALL = T_CTX + T_LAT
CTX_COND = DEC_BATCH
N_COND = 8

TM = 512
TQ = 256
MLA_LAT_TQ = 256
CONV_TM = 256
CONV_HALO = 16
MOE_BM = 256
MOE_BLOCKS = (T_ALL * TOP_K + N_EXPERTS * (MOE_BM - 1) + MOE_BM - 1) // MOE_BM
MOE_CAP = MOE_BLOCKS * MOE_BM
VMEM_LIMIT = 56 * 1024 * 1024


def _params(*sem):
    return pltpu.CompilerParams(dimension_semantics=sem, vmem_limit_bytes=VMEM_LIMIT)


def _cond_of_row(row0):
    return jnp.where(row0 < T_CTX, CTX_COND, (row0 - T_CTX) // DEC_SEQ)


def _pos_block(i, tm):
    row0 = i * tm
    return jnp.where(row0 < T_CTX, DEC_SEQ // tm, ((row0 - T_CTX) % DEC_SEQ) // tm)


def _rms(x, g):
    ms = jnp.mean(x * x, axis=-1, keepdims=True)
    return x * lax.rsqrt(ms + NORM_EPS) * g


def _modulated(x, g, mod, j):
    return _rms(x, g) * (1.0 + mod[j + 1:j + 2]) + mod[j:j + 1]


def _sigmoid(x):
    return 1.0 / (1.0 + jnp.exp(-x))


def _dot(a, b):
    return jnp.dot(a, b, preferred_element_type=F32)


def _dot_nt(a, b):
    return lax.dot_general(a, b, (((1,), (1,)), ((), ())), preferred_element_type=F32)


TOK_ROWS = D_MODEL // LANES


def _store_token_tiles(ref, v):
    n = v.shape[0]
    for j in range(TOK_ROWS):
        ref[pl.ds(j, n, stride=TOK_ROWS), :] = v[:, j * LANES:(j + 1) * LANES]


def _load_token_tiles(ref, n):
    return jnp.concatenate([ref[pl.ds(j, n, stride=TOK_ROWS), :] for j in range(TOK_ROWS)], axis=-1)


def _rope_slab(a, c, s, shift):
    return a * c + (pltpu.roll(a, shift, 1) - pltpu.roll(a, LANES - shift, 1)) * s


def _mod_kernel(c_ref, w_ref, b_ref, o_ref):
    c = c_ref[...]
    s = (c * _sigmoid(c)).astype(BF16)
    o_ref[...] = _dot(s, w_ref[...].astype(BF16)) + b_ref[...]


def _modulation(cond, w, b):
    tn = 1536
    out = pl.pallas_call(
        _mod_kernel,
        grid=(6 * D_MODEL // tn,),
        in_specs=[pl.BlockSpec((N_COND, D_MODEL), lambda j: (0, 0)),
                  pl.BlockSpec((D_MODEL, tn), lambda j: (0, j)),
                  pl.BlockSpec((1, tn), lambda j: (0, j))],
        out_specs=pl.BlockSpec((N_COND, tn), lambda j: (0, j)),
        out_shape=jax.ShapeDtypeStruct((N_COND, 6 * D_MODEL), F32),
        compiler_params=_params("arbitrary"),
        name="modulation",
    )(cond, w, b.reshape(1, -1))
    return out.reshape(N_COND, 6, D_MODEL)


def _mod_spec(tm, base=0):
    return pl.BlockSpec((1, 6, D_MODEL), lambda i, *_: (_cond_of_row((i + base) * tm), 0, 0))


def _full(shape):
    nd = len(shape)
    return pl.BlockSpec(shape, lambda *_: (0,) * nd, pipeline_mode=pl.Buffered(1))


def _rope_cos_sin(n_tok, rot_dim):
    rows = n_tok // GRID_W
    row = np.repeat(np.arange(rows, dtype=np.float32), GRID_W)
    col = np.tile(np.arange(GRID_W, dtype=np.float32), rows)
    n_freq = rot_dim // 4
    inv_freq = np.float32(ROPE_THETA) ** (-np.arange(n_freq, dtype=np.float32) / np.float32(n_freq))
    ang = np.concatenate([row[:, None] * inv_freq, col[:, None] * inv_freq], axis=-1).astype(np.float32)
    return np.cos(ang), np.sin(ang)


def _rope_tables(rot_dim, lead, tm, scale=1.0):
    cos, sin = _rope_cos_sin(DEC_SEQ, rot_dim)
    gap = rot_dim // 2 if lead else 0
    tail = LANES - lead - gap - rot_dim

    def slab(lead_val, rot):
        n = rot.shape[0]
        return np.concatenate([np.full((n, lead), lead_val, np.float32), np.zeros((n, gap), np.float32), rot,
                               np.zeros((n, tail), np.float32)], axis=-1)

    c = slab(1.0, np.concatenate([cos, cos], axis=-1))
    s = slab(0.0, np.concatenate([sin, sin], axis=-1))
    c_id = slab(1.0, np.ones((tm, rot_dim), np.float32))
    s_id = np.zeros((tm, LANES), np.float32)
    scale = np.float32(scale)
    return (jnp.asarray(np.concatenate([c, c_id], axis=0) * scale),
            jnp.asarray(np.concatenate([s, s_id], axis=0) * scale))


MLA_A_COLS = MLA_Q_RANK + MLA_KV_RANK + 2 * LANES
MLA_ROPE_LEAD = MLA_NOPE
MLA_ROPE_OFF = MLA_NOPE + MLA_ROPE // 2


def _mla_proj_kernel(x_ref, mod_ref, n1_ref, wa_ref, qn_ref, wuq_ref, kvn_ref,
                     cq_ref, sq_ref, ck_ref, sk_ref,
                     q_ref, ckv_ref, kp_ref, kst_ref):
    h = _modulated(x_ref[...], n1_ref[...], mod_ref[0], 0).astype(BF16)
    a = _dot(h, wa_ref[...])
    cq = _rms(a[:, :MLA_Q_RANK], qn_ref[...]).astype(BF16)
    ckv_ref[...] = _rms(a[:, MLA_Q_RANK:MLA_Q_RANK + MLA_KV_RANK], kvn_ref[...])
    k0 = MLA_Q_RANK + MLA_KV_RANK
    kp = a[:, k0:k0 + LANES]
    kst_ref[...] = a[:, k0 + LANES:k0 + LANES + MLA_ROPE]
    qa = _dot(cq, wuq_ref[...])
    cq_t = cq_ref[...]
    is_ctx = pl.program_id(0) < T_CTX // TM

    @pl.when(is_ctx)
    def _():
        kp_ref[...] = (kp * ck_ref[...]).astype(BF16)
        for hd in range(MLA_HEADS):
            sl = slice(hd * LANES, (hd + 1) * LANES)
            q_ref[:, sl] = (qa[:, sl] * cq_t).astype(BF16)

    @pl.when(jnp.logical_not(is_ctx))
    def _():
        sq_t = sq_ref[...]
        kp_ref[...] = _rope_slab(kp, ck_ref[...], sk_ref[...], MLA_ROPE // 2).astype(BF16)
        for hd in range(MLA_HEADS):
            sl = slice(hd * LANES, (hd + 1) * LANES)
            q_ref[:, sl] = _rope_slab(qa[:, sl], cq_t, sq_t, MLA_ROPE // 2).astype(BF16)


def _mla_proj(x, mods, norm1, wa_ext, q_norm, wuq_ext, kv_norm, tabs):
    cq, sq, ck, sk = tabs
    n_tiles = T_ALL // TM
    row = lambda i: (i, 0)
    tab = pl.BlockSpec((TM, LANES), lambda i: (_pos_block(i, TM), 0))
    return pl.pallas_call(
        _mla_proj_kernel,
        grid=(n_tiles,),
        in_specs=[pl.BlockSpec((TM, D_MODEL), row), _mod_spec(TM), _full((1, D_MODEL)),
                  _full((D_MODEL, MLA_A_COLS)), _full((1, MLA_Q_RANK)),
                  _full((MLA_Q_RANK, MLA_HEADS * LANES)), _full((1, MLA_KV_RANK)),
                  tab, tab, tab, tab],
        out_specs=[pl.BlockSpec((TM, MLA_HEADS * LANES), row), pl.BlockSpec((TM, MLA_KV_RANK), row),
                   pl.BlockSpec((TM, LANES), row), pl.BlockSpec((TM, MLA_ROPE), row)],
        out_shape=[jax.ShapeDtypeStruct((T_ALL, MLA_HEADS * LANES), BF16),
                   jax.ShapeDtypeStruct((T_ALL, MLA_KV_RANK), F32),
                   jax.ShapeDtypeStruct((T_ALL, LANES), BF16),
                   jax.ShapeDtypeStruct((T_ALL, MLA_ROPE), F32)],
        compiler_params=_params("arbitrary"),
        name="mla_proj",
    )(x, mods, norm1.reshape(1, -1), wa_ext, q_norm.reshape(1, -1), wuq_ext, kv_norm.reshape(1, -1),
      cq, sq, ck, sk)


def _mla_attn_kernel(*refs, n_lat, n_cache):
    if n_cache:
        (q_ref, ckv_ref, kp_ref, cckv_ref, ckp_ref, wk_ref, wv_ref, wo_ref, x_ref, mod_ref,
         o_ref, k_s, v_s, o_s) = refs
    else:
        (q_ref, ckv_ref, kp_ref, wk_ref, wv_ref, wo_ref, x_ref, mod_ref, o_ref, k_s, v_s, o_s) = refs

    def expand(ckv_of, kp_of, off, rows):
        rc = min(rows, 256)
        for r0 in range(0, rows, rc):
            c = ckv_of(r0, rc).astype(BF16)
            kp = kp_of(r0, rc).astype(F32)
            kn = _dot(c, wk_ref[...])
            v_s[off + r0:off + r0 + rc, :] = _dot(c, wv_ref[...]).astype(BF16)
            for hd in range(MLA_HEADS):
                sl = slice(hd * LANES, (hd + 1) * LANES)
                k_s[off + r0:off + r0 + rc, sl] = (kn[:, sl] + kp).astype(BF16)

    @pl.when(pl.program_id(1) == 0)
    def _():
        if n_cache:
            expand(lambda r, n: cckv_ref[0, r:r + n, :], lambda r, n: ckp_ref[0, r:r + n, :], 0, n_cache)
        expand(lambda r, n: ckv_ref[r:r + n, :], lambda r, n: kp_ref[r:r + n, :], n_cache, n_lat)

    def head(hd, v_pair):
        sl = slice(hd * LANES, (hd + 1) * LANES)
        s = _dot_nt(q_ref[:, sl], k_s[:, sl])
        m = jnp.max(s, axis=-1, keepdims=True)
        p = jnp.exp(s - m)
        l = jnp.sum(p, axis=-1, keepdims=True)
        return _dot(p.astype(BF16), v_pair) / l

    low_half = lax.broadcasted_iota(jnp.int32, (q_ref.shape[0], LANES), 1) < MLA_V
    for hp in range(MLA_HEADS // 2):
        sl = slice(hp * LANES, (hp + 1) * LANES)
        v_pair = v_s[:, sl]
        o_s[:, sl] = jnp.where(low_half, head(2 * hp, v_pair), head(2 * hp + 1, v_pair)).astype(BF16)
    mix = _dot(o_s[...], wo_ref[...])
    o_ref[...] = x_ref[...] + mod_ref[0][2:3] * mix


def _mla_attn(x, mods, q, ckv, kp, wk_ext, wv_ext, wo_ext, cache=None):
    hw = MLA_HEADS * LANES
    tq = TQ if cache is None else MLA_LAT_TQ
    if cache is None:
        n_lat, n_cache, nb, nq, base = SEQ, 0, BATCH, SEQ // tq, 0
    else:
        n_lat, n_cache, nb, nq, base = DEC_SEQ, PAST_LEN, DEC_BATCH, DEC_SEQ // tq, T_CTX // tq
    seq_base = base * tq // n_lat
    tile = lambda b, qi: (base + b * nq + qi, 0)
    seq = lambda b, qi: (seq_base + b, 0)
    seq_mode = dict(pipeline_mode=pl.Buffered(1)) if cache is not None else {}
    in_specs = [pl.BlockSpec((tq, hw), tile),
                pl.BlockSpec((n_lat, MLA_KV_RANK), seq, **seq_mode),
                pl.BlockSpec((n_lat, LANES), seq, **seq_mode)]
    args = [q, ckv, kp]
    if cache is not None:
        in_specs += [pl.BlockSpec((1, n_cache, MLA_KV_RANK), lambda b, qi: (b, 0, 0)),
                     pl.BlockSpec((1, n_cache, LANES), lambda b, qi: (b, 0, 0))]
        args += list(cache)
    vw = MLA_HEADS * MLA_V
    in_specs += [_full((MLA_KV_RANK, hw)), _full((MLA_KV_RANK, vw)), _full((vw, D_MODEL)),
                 pl.BlockSpec((tq, D_MODEL), tile),
                 pl.BlockSpec((1, 6, D_MODEL), lambda b, qi: (_cond_of_row((base + b * nq + qi) * tq), 0, 0))]
    args += [wk_ext, wv_ext, wo_ext, x, mods]
    sk = n_cache + n_lat
    return pl.pallas_call(
        functools.partial(_mla_attn_kernel, n_lat=n_lat, n_cache=n_cache),
        grid=(nb, nq),
        in_specs=in_specs,
        out_specs=pl.BlockSpec((tq, D_MODEL), tile),
        out_shape=jax.ShapeDtypeStruct((T_ALL, D_MODEL), F32),
        scratch_shapes=[pltpu.VMEM((sk, hw), BF16), pltpu.VMEM((sk, vw), BF16), pltpu.VMEM((tq, vw), BF16)],
        input_output_aliases={len(args) - 2: 0},
        compiler_params=_params("arbitrary", "arbitrary"),
        name="mla_attn_lat" if cache is not None else "mla_attn_ctx",
    )(*args)


SWA_QW = SWA_HEADS * LANES
SWA_KW = SWA_KV_HEADS * LANES
SWA_CW = 2 * SWA_KV_HEADS * SWA_HEAD_DIM


def _swa_proj_kernel(*refs, rope):
    if rope:
        (x_ref, mod_ref, n1_ref, w_ref, b_ref, cq_ref, sq_ref, ck_ref, sk_ref, q_ref, k_ref, v_ref) = refs
    else:
        (x_ref, mod_ref, n1_ref, w_ref, b_ref, q_ref, k_ref, v_ref, kv_ref) = refs
    h = _modulated(x_ref[...], n1_ref[...], mod_ref[0], 0).astype(BF16)
    a = _dot(h, w_ref[...]) + b_ref[...]
    if rope:
        cq, sq, ck, sk = cq_ref[...], sq_ref[...], ck_ref[...], sk_ref[...]
        for hd in range(SWA_HEADS):
            sl = slice(hd * LANES, (hd + 1) * LANES)
            q_ref[:, sl] = _rope_slab(a[:, sl], cq, sq, SWA_HEAD_DIM // 2).astype(BF16)
        for hd in range(SWA_KV_HEADS):
            sl = slice(hd * LANES, (hd + 1) * LANES)
            k_ref[:, sl] = _rope_slab(a[:, SWA_QW + hd * LANES:SWA_QW + (hd + 1) * LANES], ck, sk,
                                      SWA_HEAD_DIM // 2).astype(BF16)
    else:
        q_ref[...] = (a[:, :SWA_QW] * SWA_SCALE).astype(BF16)
        k_ref[...] = a[:, SWA_QW:SWA_QW + SWA_KW].astype(BF16)
        kv_ref[...] = a[:, SWA_QW + 2 * SWA_KW:]
    v_ref[...] = a[:, SWA_QW + SWA_KW:SWA_QW + 2 * SWA_KW].astype(BF16)


def _swa_proj(x, mods, norm1, w_ext, b_ext, tabs, rope):
    base = T_CTX // TM if rope else 0
    n_tiles = (T_LAT if rope else T_CTX) // TM
    n_rows = n_tiles * TM
    ncols = w_ext.shape[1]
    row = lambda i: (i, 0)
    in_specs = [pl.BlockSpec((TM, D_MODEL), lambda i: (i + base, 0)), _mod_spec(TM, base), _full((1, D_MODEL)),
                _full((D_MODEL, ncols)), _full((1, ncols))]
    args = [x, mods, norm1.reshape(1, -1), w_ext, b_ext]
    out_specs = [pl.BlockSpec((TM, SWA_QW), row), pl.BlockSpec((TM, SWA_KW), row), pl.BlockSpec((TM, SWA_KW), row)]
    out_shape = [jax.ShapeDtypeStruct((n_rows, SWA_QW), BF16), jax.ShapeDtypeStruct((n_rows, SWA_KW), BF16),
                 jax.ShapeDtypeStruct((n_rows, SWA_KW), BF16)]
    if rope:
        tab = pl.BlockSpec((TM, LANES), lambda i: (_pos_block(i + base, TM), 0))
        in_specs += [tab] * 4
        args += list(tabs)
    else:
        out_specs.append(pl.BlockSpec((TM, SWA_CW), row))
        out_shape.append(jax.ShapeDtypeStruct((n_rows, SWA_CW), F32))
    return pl.pallas_call(
        functools.partial(_swa_proj_kernel, rope=rope),
        grid=(n_tiles,),
        in_specs=in_specs, out_specs=out_specs, out_shape=out_shape,
        compiler_params=_params("arbitrary"),
        name="swa_proj_lat" if rope else "swa_proj_ctx",
    )(*args)


def _swa_ctx_kernel(q_ref, k_ref, v_ref, sink_ref, wo_ref, bo_ref, x_ref, mod_ref, o_ref, o_s):
    for n in range(SWA_KV_HEADS):
        kn = k_ref[:, n * LANES:(n + 1) * LANES]
        vn = v_ref[:, n * LANES:(n + 1) * LANES]
        for g in range(SWA_GROUP):
            j = n * SWA_GROUP + g
            sl = slice(j * LANES, (j + 1) * LANES)
            s = _dot_nt(q_ref[:, sl], kn)
            sink = sink_ref[j]
            m = jnp.maximum(jnp.max(s, axis=-1, keepdims=True), sink)
            p = jnp.exp(s - m)
            l = jnp.sum(p, axis=-1, keepdims=True) + jnp.exp(sink - m)
            o_s[:, sl] = (_dot(p.astype(BF16), vn) / l).astype(BF16)
    mix = _dot(o_s[...], wo_ref[...]) + bo_ref[...]
    o_ref[...] = x_ref[...] + mod_ref[0][2:3] * mix


def _swa_lat_kernel(q_ref, k_ref, v_ref, ck_ref, cv_ref, sink_ref, wo_ref, bo_ref, x_ref, mod_ref, o_ref, o_s):
    span = TQ + 2 * SWA_WINDOW
    start = pl.program_id(1) * TQ
    ks = pl.multiple_of(jnp.clip(start - SWA_WINDOW, 0, DEC_SEQ - span), SWA_WINDOW)
    qpos = start + lax.broadcasted_iota(jnp.int32, (TQ, span), 0)
    kpos = ks + lax.broadcasted_iota(jnp.int32, (TQ, span), 1)
    valid = jnp.abs(kpos - qpos) <= SWA_WINDOW
    for n in range(SWA_KV_HEADS):
        nl = slice(n * LANES, (n + 1) * LANES)
        kc = ck_ref[0, :, nl]
        vc = cv_ref[0, :, nl]
        kl = k_ref[pl.ds(ks, span), nl]
        vl = v_ref[pl.ds(ks, span), nl]
        for g in range(SWA_GROUP):
            j = n * SWA_GROUP + g
            sl = slice(j * LANES, (j + 1) * LANES)
            qj = q_ref[:, sl]
            s_c = _dot_nt(qj, kc)
            s_l = jnp.where(valid, _dot_nt(qj, kl), -jnp.inf)
            sink = sink_ref[j]
            m = jnp.maximum(jnp.maximum(jnp.max(s_c, axis=-1, keepdims=True),
                                        jnp.max(s_l, axis=-1, keepdims=True)), sink)
            p_c = jnp.exp(s_c - m)
            p_l = jnp.exp(s_l - m)
            l = (jnp.sum(p_c, axis=-1, keepdims=True) + jnp.sum(p_l, axis=-1, keepdims=True)
                 + jnp.exp(sink - m))
            o = _dot(p_c.astype(BF16), vc) + _dot(p_l.astype(BF16), vl)
            o_s[:, sl] = (o / l).astype(BF16)
    mix = _dot(o_s[...], wo_ref[...]) + bo_ref[...]
    o_ref[...] = x_ref[...] + mod_ref[0][2:3] * mix


def _swa_attn(x, mods, q, k, v, sink, wo_ext, bo, cache=None):
    smem = pl.BlockSpec(memory_space=pltpu.SMEM)
    if cache is None:
        nq, base = 1, 0
        tile = lambda b, qi: (b, 0)
        in_specs = [pl.BlockSpec((TQ, SWA_QW), tile), pl.BlockSpec((SEQ, SWA_KW), tile),
                    pl.BlockSpec((SEQ, SWA_KW), tile)]
        args = [q, k, v]
        kern, nb, name = _swa_ctx_kernel, BATCH, "swa_attn_ctx"
    else:
        nq, base = DEC_SEQ // TQ, T_CTX // TQ
        tile = lambda b, qi: (b * nq + qi, 0)
        seq = lambda b, qi: (b, 0)
        cspec = pl.BlockSpec((1, PAST_LEN, SWA_KW), lambda b, qi: (b, 0, 0))
        in_specs = [pl.BlockSpec((TQ, SWA_QW), tile), pl.BlockSpec((DEC_SEQ, SWA_KW), seq),
                    pl.BlockSpec((DEC_SEQ, SWA_KW), seq), cspec, cspec]
        args = [q, k, v, cache[0], cache[1]]
        kern, nb, name = _swa_lat_kernel, DEC_BATCH, "swa_attn_lat"
    xtile = lambda b, qi: (base + b * nq + qi, 0)
    in_specs += [smem, _full((SWA_QW, D_MODEL)), _full((1, D_MODEL)), pl.BlockSpec((TQ, D_MODEL), xtile),
                 pl.BlockSpec((1, 6, D_MODEL), lambda b, qi: (_cond_of_row((base + b * nq + qi) * TQ), 0, 0))]
    args += [sink, wo_ext, bo.reshape(1, -1), x, mods]
    return pl.pallas_call(
        kern,
        grid=(nb, nq),
        in_specs=in_specs,
        out_specs=pl.BlockSpec((TQ, D_MODEL), xtile),
        out_shape=jax.ShapeDtypeStruct((T_ALL, D_MODEL), F32),
        scratch_shapes=[pltpu.VMEM((TQ, SWA_QW), BF16)],
        input_output_aliases={len(args) - 2: 0},
        compiler_params=_params("arbitrary", "arbitrary"),
        name=name,
    )(*args)


CONV_EXT = CONV_TM + 2 * CONV_HALO


def _conv_kernel(xp_ref, x_ref, xn_ref, mod_ref, n1_ref, w1_ref, b1_ref, dw_ref, dwb_ref, lng_ref, lnb_ref,
                 w2_ref, b2_ref, o_ref, ext_s, y_s):
    i = pl.program_id(0)
    tiles_per_seq = DEC_SEQ // CONV_TM
    j = (i - T_CTX // CONV_TM) % tiles_per_seq
    latent = i >= T_CTX // CONV_TM
    left_ok = jnp.logical_and(latent, j > 0)
    right_ok = jnp.logical_and(latent, j < tiles_per_seq - 1)
    mod = mod_ref[0]

    def glu_of(xv):
        h = _modulated(xv, n1_ref[...], mod, 0).astype(BF16)
        u = _dot(h, w1_ref[...]) + b1_ref[...]
        return u[:, :D_MODEL] * _sigmoid(u[:, D_MODEL:])

    ext_s[CONV_HALO:CONV_HALO + CONV_TM, :] = glu_of(x_ref[...])
    ext_s[0:CONV_HALO, :] = jnp.where(left_ok, glu_of(xp_ref[...]), 0.0)
    ext_s[CONV_HALO + CONV_TM:, :] = jnp.where(right_ok, glu_of(xn_ref[...]), 0.0)

    rows = 128
    sub = 8
    first = CONV_HALO - CONV_PAD
    span = rows + (first + CONV_WIDTH - 1) // sub * sub
    for c in range(D_MODEL // LANES):
        cl = slice(c * LANES, (c + 1) * LANES)
        dwc = dw_ref[:, cl]
        for r in range(CONV_TM // rows):
            window = ext_s[r * rows:r * rows + span + sub, cl]
            acc = jnp.zeros((rows, LANES), F32)
            for res in range(sub):
                shifted = window if res == 0 else pltpu.roll(window, span + sub - res, 0)
                for q in range(span // sub - rows // sub + 1):
                    w = q * sub + res - first
                    if 0 <= w < CONV_WIDTH:
                        acc = acc + shifted[q * sub:q * sub + rows] * dwc[w:w + 1]
            y_s[r * rows:(r + 1) * rows, cl] = acc
    y = y_s[...] + dwb_ref[...]
    mu = jnp.mean(y, axis=-1, keepdims=True)
    yc = y - mu
    var = jnp.mean(yc * yc, axis=-1, keepdims=True)
    yn = yc * lax.rsqrt(var + NORM_EPS) * lng_ref[...] + lnb_ref[...]
    z = (yn * _sigmoid(yn)).astype(BF16)
    mix = _dot(z, w2_ref[...]) + b2_ref[...]
    o_ref[...] = x_ref[...] + mod[2:3] * mix


def _conv_layer(x, mods, norm1, w1, b1, dw, dwb, ln_g, ln_b, w2, b2):
    n_tiles = T_ALL // CONV_TM
    per = CONV_TM // CONV_HALO
    n_halo_blocks = T_ALL // CONV_HALO
    row = lambda i: (i, 0)
    vec = lambda v: v.reshape(1, -1)
    return pl.pallas_call(
        _conv_kernel,
        grid=(n_tiles,),
        in_specs=[pl.BlockSpec((CONV_HALO, D_MODEL), lambda i: (jnp.maximum(i * per - 1, 0), 0)),
                  pl.BlockSpec((CONV_TM, D_MODEL), row),
                  pl.BlockSpec((CONV_HALO, D_MODEL), lambda i: (jnp.minimum((i + 1) * per, n_halo_blocks - 1), 0)),
                  _mod_spec(CONV_TM), _full((1, D_MODEL)),
                  _full((D_MODEL, 2 * D_MODEL)), _full((1, 2 * D_MODEL)),
                  _full((CONV_WIDTH, D_MODEL)), _full((1, D_MODEL)), _full((1, D_MODEL)), _full((1, D_MODEL)),
                  _full((D_MODEL, D_MODEL)), _full((1, D_MODEL))],
        out_specs=pl.BlockSpec((CONV_TM, D_MODEL), row),
        out_shape=jax.ShapeDtypeStruct((T_ALL, D_MODEL), F32),
        scratch_shapes=[pltpu.VMEM((CONV_EXT, D_MODEL), F32), pltpu.VMEM((CONV_TM, D_MODEL), F32)],
        compiler_params=_params("arbitrary"),
        name="conv_module",
    )(x, x, x, mods, vec(norm1), w1.astype(BF16), vec(b1), dw, vec(dwb), vec(ln_g), vec(ln_b),
      w2.astype(BF16), vec(b2))


def _router_kernel(x_ref, mod_ref, n2_ref, rw_ref, rb_ref, h_ref, idx_ref, gate_ref, cnt_ref):
    h = _modulated(x_ref[...], n2_ref[...], mod_ref[0], 3)
    _store_token_tiles(h_ref, h)
    cur = _dot_nt(rw_ref[...], h.astype(BF16)) + rb_ref[...]
    e_iota = lax.broadcasted_iota(jnp.int32, cur.shape, 0)
    vals, idxs = [], []
    for _ in range(TOP_K):
        m = jnp.max(cur, axis=0, keepdims=True)
        idx = jnp.min(jnp.where(cur == m, e_iota, N_EXPERTS), axis=0, keepdims=True)
        vals.append(m)
        idxs.append(idx)
        cur = jnp.where(e_iota == idx, -jnp.inf, cur)
    ex = [jnp.exp(v - vals[0]) for v in vals]
    tot = ex[0] + ex[1] + ex[2] + ex[3]
    for k in range(TOP_K):
        idx_ref[k:k + 1, :] = idxs[k]
        gate_ref[k:k + 1, :] = ex[k] / tot

    picked = jnp.sum(jnp.where(cur == -jnp.inf, 1.0, 0.0), axis=1, keepdims=True)

    @pl.when(pl.program_id(0) == 0)
    def _():
        cnt_ref[...] = jnp.zeros_like(cnt_ref)

    cnt_ref[...] += picked


def _router(x, mods, norm2, router_w, router_b):
    n_tiles = T_ALL // TM
    row = lambda i: (i, 0)
    col = lambda i: (0, i)
    return pl.pallas_call(
        _router_kernel,
        grid=(n_tiles,),
        in_specs=[pl.BlockSpec((TM, D_MODEL), row), _mod_spec(TM), _full((1, D_MODEL)),
                  _full((N_EXPERTS, D_MODEL)), _full((N_EXPERTS, 1))],
        out_specs=[pl.BlockSpec((TM * TOK_ROWS, LANES), row), pl.BlockSpec((TOP_K, TM), col),
                   pl.BlockSpec((TOP_K, TM), col), pl.BlockSpec((N_EXPERTS, 1), lambda i: (0, 0))],
        out_shape=[jax.ShapeDtypeStruct((T_ALL * TOK_ROWS, LANES), F32),
                   jax.ShapeDtypeStruct((TOP_K, T_ALL), jnp.int32),
                   jax.ShapeDtypeStruct((TOP_K, T_ALL), F32),
                   jax.ShapeDtypeStruct((N_EXPERTS, 1), F32)],
        compiler_params=_params("arbitrary"),
        name="router",
    )(x, mods, norm2.reshape(1, -1), router_w.T.astype(BF16), router_b.reshape(-1, 1))


N_ASSIGN = TOP_K * T_ALL
RANK_CHUNK = 256
RANK_BLOCK = 2048


def _block_layout(counts):
    counts = counts.reshape(N_EXPERTS).astype(jnp.int32)
    padded = (counts + MOE_BM - 1) // MOE_BM * MOE_BM
    pad_end = jnp.cumsum(padded)
    pad_start = pad_end - padded
    starts = jnp.arange(MOE_BLOCKS + 1, dtype=jnp.int32) * MOE_BM
    e_ids = jnp.arange(N_EXPERTS, dtype=jnp.int32)
    used = padded > 0
    last_used = jnp.max(jnp.where(used, e_ids, 0))
    block_e = jnp.minimum(jnp.sum((pad_end[None, :] <= starts[:, None]).astype(jnp.int32), axis=1), last_used)
    n_used = (pad_end[-1] // MOE_BM).reshape(1)
    run = jnp.cumsum(used.astype(jnp.int32)) - 1
    later_used = jnp.where(jnp.logical_and(used[None, :], e_ids[None, :] > e_ids[:, None]), e_ids[None, :], N_EXPERTS)
    next_used = jnp.min(later_used, axis=1)
    next_used = jnp.where(next_used == N_EXPERTS, e_ids, next_used)
    onehot = (block_e[:, None] == e_ids[None, :]).astype(jnp.int32)
    block_next = jnp.sum(onehot * next_used[None, :], axis=1)
    block_par = jnp.sum(onehot * run[None, :], axis=1) & 1
    block_plan = (block_e, block_next, block_par, n_used)
    return pad_start.astype(F32).reshape(N_EXPERTS, 1), block_plan


def _rank_kernel(idx_ref, ps_ref, u_ref, dest_ref, carry):
    @pl.when(pl.program_id(0) == 0)
    def _():
        carry[...] = ps_ref[...]

    e_iota = lax.broadcasted_iota(jnp.int32, (N_EXPERTS, RANK_CHUNK), 0)
    for c in range(RANK_BLOCK // RANK_CHUNK):
        sl = slice(c * RANK_CHUNK, (c + 1) * RANK_CHUNK)
        hit = e_iota == idx_ref[:, sl]
        pref = _dot(jnp.where(hit, 1.0, 0.0).astype(BF16), u_ref[...])
        base = carry[...]
        dest = jnp.sum(jnp.where(hit, pref + base, 0.0), axis=0, keepdims=True) - 1.0
        dest_ref[:, sl] = dest.astype(jnp.int32)
        carry[...] = base + pref[:, RANK_CHUNK - 1:RANK_CHUNK]


def _assignment_slots(idx_t, pad_start):
    tri = jnp.asarray(np.triu(np.ones((RANK_CHUNK, RANK_CHUNK), np.float32)), dtype=BF16)
    blk = pl.BlockSpec((1, RANK_BLOCK), lambda i: (0, i))
    return pl.pallas_call(
        _rank_kernel,
        grid=(N_ASSIGN // RANK_BLOCK,),
        in_specs=[blk, _full((N_EXPERTS, 1)), _full((RANK_CHUNK, RANK_CHUNK))],
        out_specs=blk,
        out_shape=jax.ShapeDtypeStruct((1, N_ASSIGN), jnp.int32),
        scratch_shapes=[pltpu.VMEM((N_EXPERTS, 1), F32)],
        compiler_params=_params("arbitrary"),
        name="moe_rank",
    )(idx_t.reshape(1, N_ASSIGN), pad_start, tri)


INV_UNROLL = 16
INV_CHUNK = 4096
INV_STEPS = N_ASSIGN // INV_CHUNK
N_SLOTS = MOE_CAP + MOE_BM


def _inverse_kernel(dest_ref, pad_hbm, slot_ref, sem):
    i = pl.program_id(0)

    @pl.when(i == 0)
    def _():
        fill = pltpu.make_async_copy(pad_hbm, slot_ref, sem)
        fill.start()
        fill.wait()

    base = i * INV_CHUNK

    def place(j, carry):
        o0 = j * INV_UNROLL
        batch = dest_ref.at[0, 0, pl.ds(o0, INV_UNROLL)]
        dests = [batch[u] for u in range(INV_UNROLL)]
        for u in range(INV_UNROLL):
            slot_ref[dests[u]] = base + o0 + u
        return carry

    lax.fori_loop(0, INV_CHUNK // INV_UNROLL, place, 0)


def _slot_assignments(dest):
    s = np.arange(N_SLOTS, dtype=np.int32)
    pad_ids = jnp.asarray(N_ASSIGN + (s & (MOE_BM - 1)) + np.where(s >= MOE_CAP, MOE_BM, 0).astype(np.int32))
    return pl.pallas_call(
        _inverse_kernel,
        grid=(INV_STEPS,),
        in_specs=[pl.BlockSpec((1, 1, INV_CHUNK), lambda i: (i, 0, 0), memory_space=pltpu.SMEM),
                  pl.BlockSpec(memory_space=pl.ANY)],
        out_specs=pl.BlockSpec(memory_space=pltpu.SMEM),
        out_shape=jax.ShapeDtypeStruct((N_SLOTS,), jnp.int32),
        scratch_shapes=[pltpu.SemaphoreType.DMA(())],
        compiler_params=_params("arbitrary"),
        name="moe_slots",
    )(dest.reshape(INV_STEPS, 1, INV_CHUNK), pad_ids)


YG_TOKENS = N_ASSIGN + 2 * MOE_BM
BLOCK_TILE_ROWS = MOE_BM * TOK_ROWS
MOE_CHUNK = 256
ACT_BUF = 3


def _moe_kernel(be_ref, ne_ref, par_ref, nu_ref, sa_prev_ref, sa_ref, sa_next_ref, sa_next2_ref, h_hbm,
                wg_hbm, wu_hbm, wd_hbm, bg_ref, bu_ref, bd_ref, yg_hbm,
                xbuf, ybuf, x_s, wg_f, wu_f, wd_f, wgu_s, wd_s, gsem, ssem, wsem):
    i = pl.program_id(0)
    n_used = nu_ref[0]
    cur = i % 2
    nxt = 1 - cur
    gcur = lax.rem(i, 3)
    gnext = lax.rem(i + 1, 3)
    gnext2 = lax.rem(i + 2, 3)

    def weight_copies(e, slot):
        return [pltpu.make_async_copy(src.at[e], dst.at[slot], wsem.at[slot])
                for src, dst in ((wg_hbm, wg_f), (wu_hbm, wu_f), (wd_hbm, wd_f))]

    def start_gather(sa, buf, lo=0, hi=MOE_BM):
        for r in range(lo, hi):
            tok = sa[0, 0, r] & (T_ALL - 1)
            pltpu.make_async_copy(h_hbm.at[pl.ds(tok * TOK_ROWS, TOK_ROWS)],
                                  xbuf.at[buf, pl.ds(r * TOK_ROWS, TOK_ROWS)], gsem.at[buf]).start()

    def start_scatter(sa, buf):
        for r in range(MOE_BM):
            pltpu.async_copy(ybuf.at[buf, pl.ds(r * TOK_ROWS, TOK_ROWS)],
                             yg_hbm.at[pl.ds(sa[0, 0, r] * TOK_ROWS, TOK_ROWS)], ssem.at[buf], priority=1)

    def wait_gather(buf):
        pltpu.make_async_copy(h_hbm.at[pl.ds(0, BLOCK_TILE_ROWS)], xbuf.at[buf], gsem.at[buf]).wait()

    def wait_scatter(buf):
        pltpu.make_async_copy(ybuf.at[buf], yg_hbm.at[pl.ds(0, BLOCK_TILE_ROWS)], ssem.at[buf]).wait()

    @pl.when(i == 0)
    def _():
        for cp in weight_copies(be_ref[0], 0):
            cp.start()
        start_gather(sa_ref, 0)
        start_gather(sa_next_ref, 1)
        ybuf[...] = jnp.zeros_like(ybuf)
        pltpu.make_async_copy(ybuf.at[0], yg_hbm.at[pl.ds(N_ASSIGN * TOK_ROWS, BLOCK_TILE_ROWS)], ssem.at[0]).start()

    @pl.when(i <= n_used)
    def _():
        e = be_ref[i]

        @pl.when(jnp.logical_or(i == 0, e != be_ref[jnp.maximum(i - 1, 0)]))
        def _():
            slot = par_ref[i]
            for cp in weight_copies(e, slot):
                cp.wait()
            for c in range(D_MODEL // MOE_CHUNK):
                cs = slice(c * MOE_CHUNK, (c + 1) * MOE_CHUNK)
                wgu_s[:, 2 * c * MOE_CHUNK:(2 * c + 1) * MOE_CHUNK] = wg_f[slot, :, cs].astype(BF16)
                wgu_s[:, (2 * c + 1) * MOE_CHUNK:(2 * c + 2) * MOE_CHUNK] = wu_f[slot, :, cs].astype(BF16)
            wd_s[...] = wd_f[slot].astype(BF16)

            @pl.when(ne_ref[i] != e)
            def _():
                for cp in weight_copies(ne_ref[i], 1 - slot):
                    cp.start()

        wait_gather(gcur)
        x_s[...] = _load_token_tiles(xbuf.at[gcur], MOE_BM).astype(BF16)
        n_chunks = D_MODEL // MOE_CHUNK
        slabs = MOE_CHUNK // LANES
        for c in range(n_chunks):
            cs = slice(c * MOE_CHUNK, (c + 1) * MOE_CHUNK)
            start_gather(sa_next2_ref, gnext2, c * MOE_BM // n_chunks, (c + 1) * MOE_BM // n_chunks)
            gu = _dot(x_s[...], wgu_s[:, 2 * c * MOE_CHUNK:(2 * c + 2) * MOE_CHUNK])
            g = jnp.minimum(gu[:, :MOE_CHUNK] + bg_ref[0, :, cs], SWIGLU_LIMIT)
            u = jnp.clip(gu[:, MOE_CHUNK:] + bu_ref[0, :, cs], -SWIGLU_LIMIT, SWIGLU_LIMIT)
            act = (u + 1.0) * (g * _sigmoid(SWIGLU_ALPHA * g))
            for j in range(slabs):
                jj = c * slabs + j
                xbuf[ACT_BUF, jj * MOE_BM:(jj + 1) * MOE_BM, :] = act[:, j * LANES:(j + 1) * LANES]
        wait_scatter(cur)
        start_scatter(sa_prev_ref, nxt)
        act_b =jnp.concatenate([xbuf[ACT_BUF, j * MOE_BM:(j + 1) * MOE_BM, :] for j in range(TOK_ROWS)],
                                axis=-1).astype(BF16)
        half = D_MODEL // 2
        for c in range(2):
            y = _dot(act_b, wd_s[:, c * half:(c + 1) * half]) + bd_ref[0, :, c * half:(c + 1) * half]
            for j in range(TOK_ROWS // 2):
                jj = c * (TOK_ROWS // 2) + j
                ybuf.at[cur][pl.ds(jj, MOE_BM, stride=TOK_ROWS), :] = y[:, j * LANES:(j + 1) * LANES]

    @pl.when(i == n_used)
    def _():
        wait_gather(gnext)
        wait_gather(gnext2)
        wait_scatter(nxt)


def _moe_experts(h, slots, block_plan, wg, bg, wu, bu, wd, bd):
    bspec = pl.BlockSpec((1, 1, D_MODEL), lambda i, be, *_: (be[i], 0, 0))
    sspec = lambda f: pl.BlockSpec((1, 1, MOE_BM), lambda i, *_: (f(i), 0, 0), memory_space=pltpu.SMEM)
    hbm = pl.BlockSpec(memory_space=pl.ANY)
    b3 = lambda b: b.reshape(N_EXPERTS, 1, D_MODEL)
    slots3 = slots.reshape(MOE_BLOCKS + 1, 1, MOE_BM)
    wbuf = pltpu.VMEM((2, D_MODEL, D_MODEL), F32)
    return pl.pallas_call(
        _moe_kernel,
        grid_spec=pltpu.PrefetchScalarGridSpec(
            num_scalar_prefetch=4,
            grid=(MOE_BLOCKS + 1,),
            in_specs=[sspec(lambda i: jnp.where(i == 0, MOE_BLOCKS, i - 1)), sspec(lambda i: i),
                      sspec(lambda i: jnp.minimum(i + 1, MOE_BLOCKS)), sspec(lambda i: jnp.minimum(i + 2, MOE_BLOCKS)),
                      hbm, hbm, hbm, hbm, bspec, bspec, bspec],
            out_specs=hbm,
            scratch_shapes=[pltpu.VMEM((ACT_BUF + 1, BLOCK_TILE_ROWS, LANES), F32),
                            pltpu.VMEM((2, BLOCK_TILE_ROWS, LANES), F32),
                            pltpu.VMEM((MOE_BM, D_MODEL), BF16),
                            wbuf, wbuf, wbuf,
                            pltpu.VMEM((D_MODEL, 2 * D_MODEL), BF16), pltpu.VMEM((D_MODEL, D_MODEL), BF16),
                            pltpu.SemaphoreType.DMA((3,)), pltpu.SemaphoreType.DMA((2,)),
                            pltpu.SemaphoreType.DMA((2,))]),
        out_shape=jax.ShapeDtypeStruct((YG_TOKENS * TOK_ROWS, LANES), F32),
        compiler_params=_params("arbitrary"),
        name="moe_experts",
    )(*block_plan, slots3, slots3, slots3, slots3, h, wg, wu, wd, b3(bg), b3(bu), b3(bd))


def _combine_kernel(*refs, final):
    if final:
        x_ref, y0_ref, y1_ref, y2_ref, y3_ref, gate_ref, mod_ref, fg_ref, o_ref = refs
    else:
        x_ref, y0_ref, y1_ref, y2_ref, y3_ref, gate_ref, mod_ref, o_ref = refs
    gate = gate_ref[...]
    n = x_ref.shape[0]
    g2 = mod_ref[0][5:6]
    sumsq = jnp.zeros((n, 1), F32)
    for j in range(TOK_ROWS):
        rows = pl.ds(j, n, stride=TOK_ROWS)
        sl = slice(j * LANES, (j + 1) * LANES)
        acc = y0_ref[rows, :] * gate[:, 0:1]
        for k, y_ref in enumerate((y1_ref, y2_ref, y3_ref), start=1):
            acc = acc + y_ref[rows, :] * gate[:, k:k + 1]
        xn = x_ref[:, sl] + g2[:, sl] * acc
        o_ref[:, sl] = xn
        if final:
            sumsq = sumsq + jnp.sum(xn * xn, axis=-1, keepdims=True)
    if final:
        inv = lax.rsqrt(sumsq * (1.0 / D_MODEL) + NORM_EPS)
        o_ref[...] = o_ref[...] * inv * fg_ref[...]


def _combine(x, yg, gates, mods, final_g=None, base_rows=0, n_rows=T_ALL):
    final = final_g is not None
    base = base_rows // TM
    n_all = T_ALL // TM
    row = lambda i: (i + base, 0)
    ysp = lambda k: pl.BlockSpec((TM * TOK_ROWS, LANES), lambda i: (k * n_all + i + base, 0))
    in_specs = [pl.BlockSpec((TM, D_MODEL), row), ysp(0), ysp(1), ysp(2), ysp(3),
                pl.BlockSpec((TM, TOP_K), row), _mod_spec(TM, base)]
    args = [x, yg, yg, yg, yg, gates, mods]
    if final:
        in_specs.append(_full((1, D_MODEL)))
        args.append(final_g.reshape(1, -1))
    return pl.pallas_call(
        functools.partial(_combine_kernel, final=final),
        grid=(n_rows // TM,),
        in_specs=in_specs,
        out_specs=pl.BlockSpec((TM, D_MODEL), (lambda i: (i, 0)) if final else row),
        out_shape=jax.ShapeDtypeStruct((n_rows, D_MODEL), F32),
        input_output_aliases={} if final else {0: 0},
        compiler_params=_params("arbitrary"),
        name="moe_combine_final" if final else "moe_combine",
    )(*args)


def _moe_layer(x, mods, norm2, router_w, router_b, wg, bg, wu, bu, wd, bd, final_g=None):
    h, idx_t, gate_t, counts = _router(x, mods, norm2, router_w, router_b)
    pad_start, block_plan = _block_layout(counts)
    slots = _slot_assignments(_assignment_slots(idx_t, pad_start))
    yg = _moe_experts(h, slots, block_plan, wg, bg, wu, bu, wd, bd)
    gates = gate_t.T
    if final_g is None:
        return _combine(x, yg, gates, mods)
    return (_combine(x, yg, gates, mods, final_g, 0, T_CTX),
            _combine(x, yg, gates, mods, final_g, T_CTX, T_LAT))


def _pad_lanes(w, lead_pad, width):
    tail = width - lead_pad - w.shape[-1]
    cfg = [(0, 0)] * (w.ndim - 1) + [(lead_pad, tail)]
    return jnp.pad(w, cfg)


def _mla_weights(wa, wuq, wukv, wo):
    krope_w = wa[:, MLA_Q_RANK + MLA_KV_RANK:]
    wa_ext = jnp.concatenate([wa[:, :MLA_Q_RANK + MLA_KV_RANK],
                              _pad_lanes(krope_w, MLA_ROPE_OFF, LANES),
                              _pad_lanes(krope_w, 0, LANES)], axis=1).astype(BF16)
    q3 = wuq.reshape(MLA_Q_RANK, MLA_HEADS, MLA_NOPE + MLA_ROPE)
    wuq_ext = jnp.concatenate([q3[..., :MLA_NOPE], _pad_lanes(q3[..., MLA_NOPE:], MLA_ROPE // 2, LANES - MLA_NOPE)],
                              axis=-1).reshape(MLA_Q_RANK, MLA_HEADS * LANES).astype(BF16)
    kv3 = wukv.reshape(MLA_KV_RANK, MLA_HEADS, MLA_NOPE + MLA_V)
    wk_ext = _pad_lanes(kv3[..., :MLA_NOPE], 0, LANES).reshape(MLA_KV_RANK, MLA_HEADS * LANES).astype(BF16)
    wv = kv3[..., MLA_NOPE:].reshape(MLA_KV_RANK, MLA_HEADS * MLA_V).astype(BF16)
    return wa_ext, wuq_ext, wk_ext, wv, wo.astype(BF16)


def _swa_weights(wqkv, bqkv, wo):
    nq = SWA_HEADS * SWA_HEAD_DIM
    nkv = SWA_KV_HEADS * SWA_HEAD_DIM

    def slabs(w, heads):
        lead = w.shape[:-1]
        return _pad_lanes(w.reshape(lead + (heads, SWA_HEAD_DIM)), 0, LANES).reshape(lead + (heads * LANES,))

    def ext(w, with_compact):
        parts = [slabs(w[..., :nq], SWA_HEADS), slabs(w[..., nq:nq + nkv], SWA_KV_HEADS),
                 slabs(w[..., nq + nkv:], SWA_KV_HEADS)]
        if with_compact:
            parts.append(w[..., nq:])
        return jnp.concatenate(parts, axis=-1)

    b2 = bqkv.reshape(1, -1)
    wo3 = wo.reshape(SWA_HEADS, SWA_HEAD_DIM, D_MODEL)
    wo_ext = jnp.pad(wo3, ((0, 0), (0, LANES - SWA_HEAD_DIM), (0, 0))).reshape(SWA_QW, D_MODEL).astype(BF16)
    return (ext(wqkv, True).astype(BF16), ext(b2, True), ext(wqkv, False).astype(BF16), ext(b2, False), wo_ext)


def kernel(x_prompt, x_sample, cache_l0_ckv, cache_l0_krope, cache_l1_k, cache_l1_v, cache_l3_ckv, cache_l3_krope, c, c_ctx, l0_mla_wa, l0_mla_q_norm, l0_mla_wuq, l0_mla_kv_norm, l0_mla_wukv, l0_mla_wo, l0_mod_w, l0_mod_b, l0_norm1, l0_norm2, l0_router_w, l0_router_b, l0_moe_wg, l0_moe_bg, l0_moe_wu, l0_moe_bu, l0_moe_wd, l0_moe_bd, l1_swa_wqkv, l1_swa_bqkv, l1_swa_sink, l1_swa_wo, l1_swa_bo, l1_mod_w, l1_mod_b, l1_norm1, l1_norm2, l1_router_w, l1_router_b, l1_moe_wg, l1_moe_bg, l1_moe_wu, l1_moe_bu, l1_moe_wd, l1_moe_bd, l2_conv_w1, l2_conv_b1, l2_conv_dw, l2_conv_dwb, l2_conv_ln_g, l2_conv_ln_b, l2_conv_w2, l2_conv_b2, l2_mod_w, l2_mod_b, l2_norm1, l2_norm2, l2_router_w, l2_router_b, l2_moe_wg, l2_moe_bg, l2_moe_wu, l2_moe_bu, l2_moe_wd, l2_moe_bd, l3_mla_wa, l3_mla_q_norm, l3_mla_wuq, l3_mla_kv_norm, l3_mla_wukv, l3_mla_wo, l3_mod_w, l3_mod_b, l3_norm1, l3_norm2, l3_router_w, l3_router_b, l3_moe_wg, l3_moe_bg, l3_moe_wu, l3_moe_bu, l3_moe_wd, l3_moe_bd, final_norm):
    x = jnp.concatenate([x_prompt.reshape(T_CTX, D_MODEL), x_sample.reshape(T_LAT, D_MODEL)], axis=0)
    cond = jnp.concatenate([c, c_ctx[None, :], jnp.zeros((N_COND - DEC_BATCH - 1, D_MODEL), F32)], axis=0)

    mla_tabs = (_rope_tables(MLA_ROPE, MLA_ROPE_LEAD, TM, MLA_SCALE) + _rope_tables(MLA_ROPE, MLA_ROPE_LEAD, TM))
    swa_tabs = _rope_tables(SWA_HEAD_DIM, 0, TM, SWA_SCALE) + _rope_tables(SWA_HEAD_DIM, 0, TM)

    def mla_layer(x, mods, norm1, wa, q_norm, wuq, kv_norm, wukv, wo, cache_ckv, cache_krope):
        wa_ext, wuq_ext, wk_ext, wv_ext, wo_ext = _mla_weights(wa, wuq, wukv, wo)
        q, ckv, kp, kst = _mla_proj(x, mods, norm1, wa_ext, q_norm, wuq_ext, kv_norm, mla_tabs)
        x = _mla_attn(x, mods, q, ckv, kp, wk_ext, wv_ext, wo_ext)
        cache = (cache_ckv, _pad_lanes(cache_krope, MLA_ROPE_OFF, LANES).astype(BF16))
        x = _mla_attn(x, mods, q, ckv, kp, wk_ext, wv_ext, wo_ext, cache)
        return x, (ckv[:T_CTX].reshape(BATCH, SEQ, MLA_KV_RANK), kst[:T_CTX].reshape(BATCH, SEQ, MLA_ROPE))

    def moe(x, mods, norm2, rw, rb, wg, bg, wu, bu, wd, bd):
        return _moe_layer(x, mods, norm2, rw, rb, wg, bg, wu, bu, wd, bd)

    mods = _modulation(cond, l0_mod_w, l0_mod_b)
    x, (st0_ckv, st0_krope) = mla_layer(x, mods, l0_norm1, l0_mla_wa, l0_mla_q_norm, l0_mla_wuq, l0_mla_kv_norm,
                                        l0_mla_wukv, l0_mla_wo, cache_l0_ckv, cache_l0_krope)
    x = moe(x, mods, l0_norm2, l0_router_w, l0_router_b, l0_moe_wg, l0_moe_bg, l0_moe_wu, l0_moe_bu,
            l0_moe_wd, l0_moe_bd)

    mods = _modulation(cond, l1_mod_w, l1_mod_b)
    w_ctx, b_ctx, w_lat, b_lat, swa_wo_ext = _swa_weights(l1_swa_wqkv, l1_swa_bqkv, l1_swa_wo)
    q_c, k_c, v_c, kv_c = _swa_proj(x, mods, l1_norm1, w_ctx, b_ctx, None, rope=False)
    q_l, k_l, v_l = _swa_proj(x, mods, l1_norm1, w_lat, b_lat, swa_tabs, rope=True)
    x = _swa_attn(x, mods, q_c, k_c, v_c, l1_swa_sink, swa_wo_ext, l1_swa_bo)
    nkv = SWA_KV_HEADS * SWA_HEAD_DIM
    cache_k = _pad_lanes(cache_l1_k, 0, LANES).reshape(DEC_BATCH, PAST_LEN, SWA_KW).astype(BF16)
    cache_v = _pad_lanes(cache_l1_v, 0, LANES).reshape(DEC_BATCH, PAST_LEN, SWA_KW).astype(BF16)
    x = _swa_attn(x, mods, q_l, k_l, v_l, l1_swa_sink, swa_wo_ext, l1_swa_bo, (cache_k, cache_v))
    st1_k = kv_c[:, :nkv].reshape(BATCH, SEQ, SWA_KV_HEADS, SWA_HEAD_DIM)
    st1_v = kv_c[:, nkv:].reshape(BATCH, SEQ, SWA_KV_HEADS, SWA_HEAD_DIM)
    x = moe(x, mods, l1_norm2, l1_router_w, l1_router_b, l1_moe_wg, l1_moe_bg, l1_moe_wu, l1_moe_bu,
            l1_moe_wd, l1_moe_bd)

    mods = _modulation(cond, l2_mod_w, l2_mod_b)
    x = _conv_layer(x, mods, l2_norm1, l2_conv_w1, l2_conv_b1, l2_conv_dw, l2_conv_dwb, l2_conv_ln_g,
                    l2_conv_ln_b, l2_conv_w2, l2_conv_b2)
    x = moe(x, mods, l2_norm2, l2_router_w, l2_router_b, l2_moe_wg, l2_moe_bg, l2_moe_wu, l2_moe_bu,
            l2_moe_wd, l2_moe_bd)

    mods = _modulation(cond, l3_mod_w, l3_mod_b)
    x, (st3_ckv, st3_krope) = mla_layer(x, mods, l3_norm1, l3_mla_wa, l3_mla_q_norm, l3_mla_wuq, l3_mla_kv_norm,
                                        l3_mla_wukv, l3_mla_wo, cache_l3_ckv, cache_l3_krope)
    y_ctx, y_lat = _moe_layer(x, mods, l3_norm2, l3_router_w, l3_router_b, l3_moe_wg, l3_moe_bg, l3_moe_wu,
                              l3_moe_bu, l3_moe_wd, l3_moe_bd, final_g=final_norm)
    y_prompt = y_ctx.reshape(BATCH, SEQ, D_MODEL)
    y_sample = y_lat.reshape(DEC_BATCH, DEC_SEQ, D_MODEL)
    return (y_prompt, y_sample, st0_ckv, st0_krope, st1_k, st1_v, st3_ckv, st3_krope)
```

```python
import functools

import jax
import jax.numpy as jnp
import numpy as np
from jax import lax
from jax.experimental import pallas as pl
from jax.experimental.pallas import tpu as pltpu

F32 = jnp.float32
BF16 = jnp.bfloat16

D_MODEL = 1024
BATCH = 32
SEQ = 256
DEPTH = 4
DEC_BATCH = 4
DEC_SEQ = 2048
PAST_LEN = 256
GRID_W = 64
NORM_EPS = 1e-6
ROPE_THETA = 10000.0

MLA_HEADS = 16
MLA_NOPE = 64
MLA_ROPE = 32
MLA_V = 64
MLA_Q_RANK = 384
MLA_KV_RANK = 256
MLA_SCALE = (MLA_NOPE + MLA_ROPE) ** -0.5

SWA_HEADS = 16
SWA_KV_HEADS = 4
SWA_GROUP = 4
SWA_HEAD_DIM = 64
SWA_WINDOW = 128
SWA_SCALE = SWA_HEAD_DIM ** -0.5

CONV_WIDTH = 31
CONV_PAD = CONV_WIDTH // 2

N_EXPERTS = 32
TOP_K = 4
SWIGLU_ALPHA = 1.702
SWIGLU_LIMIT = 7.0

LANES = 128
T_CTX = BATCH * SEQ
T_LAT = DEC_BATCH * DEC_SEQ
T_ALL = T_CTX + T_LAT
CTX_COND = DEC_BATCH
N_COND = 8

TM = 512
TQ = 256
MLA_LAT_TQ = 256
CONV_TM = 256
CONV_HALO = 16
MOE_BM = 256
MOE_BLOCKS = (T_ALL * TOP_K + N_EXPERTS * (MOE_BM - 1) + MOE_BM - 1) // MOE_BM
MOE_CAP = MOE_BLOCKS * MOE_BM
VMEM_LIMIT = 56 * 1024 * 1024


def _params(*sem):
    return pltpu.CompilerParams(dimension_semantics=sem, vmem_limit_bytes=VMEM_LIMIT)


def _cond_of_row(row0):
    return jnp.where(row0 < T_CTX, CTX_COND, (row0 - T_CTX) // DEC_SEQ)


def _pos_block(i, tm):
    row0 = i * tm
    return jnp.where(row0 < T_CTX, DEC_SEQ // tm, ((row0 - T_CTX) % DEC_SEQ) // tm)


def _rms(x, g):
    ms = jnp.mean(x * x, axis=-1, keepdims=True)
    return x * lax.rsqrt(ms + NORM_EPS) * g


def _modulated(x, g, mod, j):
    return _rms(x, g) * (1.0 + mod[j + 1:j + 2]) + mod[j:j + 1]


def _sigmoid(x):
    return 1.0 / (1.0 + jnp.exp(-x))


def _dot(a, b):
    return jnp.dot(a, b, preferred_element_type=F32)


def _dot_nt(a, b):
    return lax.dot_general(a, b, (((1,), (1,)), ((), ())), preferred_element_type=F32)


TOK_ROWS = D_MODEL // LANES


def _store_token_tiles(ref, v):
    n = v.shape[0]
    for j in range(TOK_ROWS):
        ref[pl.ds(j, n, stride=TOK_ROWS), :] = v[:, j * LANES:(j + 1) * LANES]


def _load_token_tiles(ref, n):
    return jnp.concatenate([ref[pl.ds(j, n, stride=TOK_ROWS), :] for j in range(TOK_ROWS)], axis=-1)


def _rope_slab(a, c, s, shift):
    return a * c + (pltpu.roll(a, shift, 1) - pltpu.roll(a, LANES - shift, 1)) * s


def _mod_kernel(c_ref, w_ref, b_ref, o_ref):
    c = c_ref[...]
    s = (c * _sigmoid(c)).astype(BF16)
    o_ref[...] = _dot(s, w_ref[...].astype(BF16)) + b_ref[...]


def _modulation(cond, w, b):
    tn = 1536
    out = pl.pallas_call(
        _mod_kernel,
        grid=(6 * D_MODEL // tn,),
        in_specs=[pl.BlockSpec((N_COND, D_MODEL), lambda j: (0, 0)),
                  pl.BlockSpec((D_MODEL, tn), lambda j: (0, j)),
                  pl.BlockSpec((1, tn), lambda j: (0, j))],
        out_specs=pl.BlockSpec((N_COND, tn), lambda j: (0, j)),
        out_shape=jax.ShapeDtypeStruct((N_COND, 6 * D_MODEL), F32),
        compiler_params=_params("arbitrary"),
        name="modulation",
    )(cond, w, b.reshape(1, -1))
    return out.reshape(N_COND, 6, D_MODEL)


def _mod_spec(tm, base=0):
    return pl.BlockSpec((1, 6, D_MODEL), lambda i, *_: (_cond_of_row((i + base) * tm), 0, 0))


def _x_specs(x, tm):
    if isinstance(x, tuple):
        n_ctx = T_CTX // tm
        return ([pl.BlockSpec((tm, D_MODEL), lambda i, *_: (jnp.minimum(i, n_ctx - 1), 0)),
                 pl.BlockSpec((tm, D_MODEL), lambda i, *_: (jnp.maximum(i - n_ctx, 0), 0))], list(x))
    return [pl.BlockSpec((tm, D_MODEL), lambda i, *_: (i, 0))], [x]


def _x_tile(x_refs, tm, cols=slice(None)):
    if len(x_refs) == 2:
        return jnp.where(pl.program_id(0) < T_CTX // tm, x_refs[0][:, cols], x_refs[1][:, cols])
    return x_refs[0][:, cols]


def _full(shape):
    nd = len(shape)
    return pl.BlockSpec(shape, lambda *_: (0,) * nd, pipeline_mode=pl.Buffered(1))


def _rope_cos_sin(n_tok, rot_dim):
    rows = n_tok // GRID_W
    row = np.repeat(np.arange(rows, dtype=np.float32), GRID_W)
    col = np.tile(np.arange(GRID_W, dtype=np.float32), rows)
    n_freq = rot_dim // 4
    inv_freq = np.float32(ROPE_THETA) ** (-np.arange(n_freq, dtype=np.float32) / np.float32(n_freq))
    ang = np.concatenate([row[:, None] * inv_freq, col[:, None] * inv_freq], axis=-1).astype(np.float32)
    return np.cos(ang), np.sin(ang)


def _rope_tables(rot_dim, lead, tm, scale=1.0):
    cos, sin = _rope_cos_sin(DEC_SEQ, rot_dim)
    gap = rot_dim // 2 if lead else 0
    tail = LANES - lead - gap - rot_dim

    def slab(lead_val, rot):
        n = rot.shape[0]
        return np.concatenate([np.full((n, lead), lead_val, np.float32), np.zeros((n, gap), np.float32), rot,
                               np.zeros((n, tail), np.float32)], axis=-1)

    c = slab(1.0, np.concatenate([cos, cos], axis=-1))
    s = slab(0.0, np.concatenate([sin, sin], axis=-1))
    c_id = slab(1.0, np.ones((tm, rot_dim), np.float32))
    s_id = np.zeros((tm, LANES), np.float32)
    scale = np.float32(scale)
    return (jnp.asarray(np.concatenate([c, c_id], axis=0) * scale),
            jnp.asarray(np.concatenate([s, s_id], axis=0) * scale))


MLA_A_COLS = MLA_Q_RANK + MLA_KV_RANK + 2 * LANES
MLA_ROPE_LEAD = MLA_NOPE
MLA_ROPE_OFF = MLA_NOPE + MLA_ROPE // 2


def _mla_proj_kernel(*refs, n_x):
    (mod_ref, n1_ref, wa_ref, qn_ref, wuq_ref, kvn_ref, cq_ref, sq_ref, ck_ref, sk_ref,
     q_ref, ckv_ref, kp_ref, kst_ref) = refs[n_x:]
    h = _modulated(_x_tile(refs[:n_x], TM), n1_ref[...], mod_ref[0], 0).astype(BF16)
    a = _dot(h, wa_ref[...])
    cq = _rms(a[:, :MLA_Q_RANK], qn_ref[...]).astype(BF16)
    ckv_ref[...] = _rms(a[:, MLA_Q_RANK:MLA_Q_RANK + MLA_KV_RANK], kvn_ref[...])
    k0 = MLA_Q_RANK + MLA_KV_RANK
    kp = a[:, k0:k0 + LANES]
    kst_ref[...] = a[:, k0 + LANES:k0 + LANES + MLA_ROPE]
    qa = _dot(cq, wuq_ref[...])
    cq_t = cq_ref[...]
    is_ctx = pl.program_id(0) < T_CTX // TM

    @pl.when(is_ctx)
    def _():
        kp_ref[...] = (kp * ck_ref[...]).astype(BF16)
        for hd in range(MLA_HEADS):
            sl = slice(hd * LANES, (hd + 1) * LANES)
            q_ref[:, sl] = (qa[:, sl] * cq_t).astype(BF16)

    @pl.when(jnp.logical_not(is_ctx))
    def _():
        sq_t = sq_ref[...]
        kp_ref[...] = _rope_slab(kp, ck_ref[...], sk_ref[...], MLA_ROPE // 2).astype(BF16)
        for hd in range(MLA_HEADS):
            sl = slice(hd * LANES, (hd + 1) * LANES)
            q_ref[:, sl] = _rope_slab(qa[:, sl], cq_t, sq_t, MLA_ROPE // 2).astype(BF16)


def _mla_proj(x, mods, norm1, wa_ext, q_norm, wuq_ext, kv_norm, tabs):
    cq, sq, ck, sk = tabs
    n_tiles = T_ALL // TM
    row = lambda i: (i, 0)
    tab = pl.BlockSpec((TM, LANES), lambda i: (_pos_block(i, TM), 0))
    x_specs, x_args = _x_specs(x, TM)
    return pl.pallas_call(
        functools.partial(_mla_proj_kernel, n_x=len(x_args)),
        grid=(n_tiles,),
        in_specs=x_specs + [_mod_spec(TM), _full((1, D_MODEL)),
                            _full((D_MODEL, MLA_A_COLS)), _full((1, MLA_Q_RANK)),
                            _full((MLA_Q_RANK, MLA_HEADS * LANES)), _full((1, MLA_KV_RANK)),
                            tab, tab, tab, tab],
        out_specs=[pl.BlockSpec((TM, MLA_HEADS * LANES), row), pl.BlockSpec((TM, MLA_KV_RANK), row),
                   pl.BlockSpec((TM, LANES), row), pl.BlockSpec((TM, MLA_ROPE), row)],
        out_shape=[jax.ShapeDtypeStruct((T_ALL, MLA_HEADS * LANES), BF16),
                   jax.ShapeDtypeStruct((T_ALL, MLA_KV_RANK), F32),
                   jax.ShapeDtypeStruct((T_ALL, LANES), BF16),
                   jax.ShapeDtypeStruct((T_ALL, MLA_ROPE), F32)],
        compiler_params=_params("arbitrary"),
        name="mla_proj",
    )(*x_args, mods, norm1.reshape(1, -1), wa_ext, q_norm.reshape(1, -1), wuq_ext, kv_norm.reshape(1, -1),
      cq, sq, ck, sk)


def _mla_attn_kernel(*refs, n_lat, n_cache):
    if n_cache:
        (q_ref, ckv_ref, kp_ref, cckv_ref, ckp_ref, wk_ref, wv_ref, wo_ref, x_ref, mod_ref,
         o_ref, k_s, v_s, o_s) = refs
    else:
        (q_ref, ckv_ref, kp_ref, wk_ref, wv_ref, wo_ref, x_ref, mod_ref, o_ref, k_s, v_s, o_s) = refs

    def expand(ckv_of, kp_of, off, rows):
        rc = min(rows, 256)
        for r0 in range(0, rows, rc):
            c = ckv_of(r0, rc).astype(BF16)
            kp = kp_of(r0, rc).astype(F32)
            kn = _dot(c, wk_ref[...])
            v_s[off + r0:off + r0 + rc, :] = _dot(c, wv_ref[...]).astype(BF16)
            for hd in range(MLA_HEADS):
                sl = slice(hd * LANES, (hd + 1) * LANES)
                k_s[off + r0:off + r0 + rc, sl] = (kn[:, sl] + kp).astype(BF16)

    @pl.when(pl.program_id(1) == 0)
    def _():
        if n_cache:
            expand(lambda r, n: cckv_ref[0, r:r + n, :], lambda r, n: ckp_ref[0, r:r + n, :], 0, n_cache)
        expand(lambda r, n: ckv_ref[r:r + n, :], lambda r, n: kp_ref[r:r + n, :], n_cache, n_lat)

    def head(hd, v_pair):
        sl = slice(hd * LANES, (hd + 1) * LANES)
        s = _dot_nt(q_ref[:, sl], k_s[:, sl])
        m = jnp.max(s, axis=-1, keepdims=True)
        p = jnp.exp(s - m)
        l = jnp.sum(p, axis=-1, keepdims=True)
        return _dot(p.astype(BF16), v_pair) / l

    low_half = lax.broadcasted_iota(jnp.int32, (q_ref.shape[0], LANES), 1) < MLA_V
    for hp in range(MLA_HEADS // 2):
        sl = slice(hp * LANES, (hp + 1) * LANES)
        v_pair = v_s[:, sl]
        o_s[:, sl] = jnp.where(low_half, head(2 * hp, v_pair), head(2 * hp + 1, v_pair)).astype(BF16)
    mix = _dot(o_s[...], wo_ref[...])
    o_ref[...] = x_ref[...] + mod_ref[0][2:3] * mix


def _mla_attn(x, mods, q, ckv, kp, wk_ext, wv_ext, wo_ext, cache=None, own_rows=False):
    hw = MLA_HEADS * LANES
    tq = TQ if cache is None else MLA_LAT_TQ
    if cache is None:
        n_lat, n_cache, nb, nq, base = SEQ, 0, BATCH, SEQ // tq, 0
    else:
        n_lat, n_cache, nb, nq, base = DEC_SEQ, PAST_LEN, DEC_BATCH, DEC_SEQ // tq, T_CTX // tq
    seq_base = base * tq // n_lat
    tile = lambda b, qi: (base + b * nq + qi, 0)
    seq = lambda b, qi: (seq_base + b, 0)
    seq_mode = dict(pipeline_mode=pl.Buffered(1)) if cache is not None else {}
    in_specs = [pl.BlockSpec((tq, hw), tile),
                pl.BlockSpec((n_lat, MLA_KV_RANK), seq, **seq_mode),
                pl.BlockSpec((n_lat, LANES), seq, **seq_mode)]
    args = [q, ckv, kp]
    if cache is not None:
        in_specs += [pl.BlockSpec((1, n_cache, MLA_KV_RANK), lambda b, qi: (b, 0, 0)),
                     pl.BlockSpec((1, n_cache, LANES), lambda b, qi: (b, 0, 0))]
        args += list(cache)
    vw = MLA_HEADS * MLA_V
    x_tile = (lambda b, qi: (b * nq + qi, 0)) if own_rows else tile
    in_specs += [_full((MLA_KV_RANK, hw)), _full((MLA_KV_RANK, vw)), _full((vw, D_MODEL)),
                 pl.BlockSpec((tq, D_MODEL), x_tile),
                 pl.BlockSpec((1, 6, D_MODEL), lambda b, qi: (_cond_of_row((base + b * nq + qi) * tq), 0, 0))]
    args += [wk_ext, wv_ext, wo_ext, x, mods]
    sk = n_cache + n_lat
    return pl.pallas_call(
        functools.partial(_mla_attn_kernel, n_lat=n_lat, n_cache=n_cache),
        grid=(nb, nq),
        in_specs=in_specs,
        out_specs=pl.BlockSpec((tq, D_MODEL), x_tile),
        out_shape=jax.ShapeDtypeStruct(x.shape, F32),
        scratch_shapes=[pltpu.VMEM((sk, hw), BF16), pltpu.VMEM((sk, vw), BF16), pltpu.VMEM((tq, vw), BF16)],
        input_output_aliases={} if own_rows else {len(args) - 2: 0},
        compiler_params=_params("arbitrary", "arbitrary"),
        name="mla_attn_lat" if cache is not None else "mla_attn_ctx",
    )(*args)


SWA_QW = SWA_HEADS * LANES
SWA_KW = SWA_KV_HEADS * LANES
SWA_CW = 2 * SWA_KV_HEADS * SWA_HEAD_DIM


def _swa_proj_kernel(*refs, rope):
    if rope:
        (x_ref, mod_ref, n1_ref, w_ref, b_ref, cq_ref, sq_ref, ck_ref, sk_ref, q_ref, k_ref, v_ref) = refs
    else:
        (x_ref, mod_ref, n1_ref, w_ref, b_ref, q_ref, k_ref, v_ref, kv_ref) = refs
    h = _modulated(x_ref[...], n1_ref[...], mod_ref[0], 0).astype(BF16)
    a = _dot(h, w_ref[...]) + b_ref[...]
    if rope:
        cq, sq, ck, sk = cq_ref[...], sq_ref[...], ck_ref[...], sk_ref[...]
        for hd in range(SWA_HEADS):
            sl = slice(hd * LANES, (hd + 1) * LANES)
            q_ref[:, sl] = _rope_slab(a[:, sl], cq, sq, SWA_HEAD_DIM // 2).astype(BF16)
        for hd in range(SWA_KV_HEADS):
            sl = slice(hd * LANES, (hd + 1) * LANES)
            k_ref[:, sl] = _rope_slab(a[:, SWA_QW + hd * LANES:SWA_QW + (hd + 1) * LANES], ck, sk,
                                      SWA_HEAD_DIM // 2).astype(BF16)
    else:
        q_ref[...] = (a[:, :SWA_QW] * SWA_SCALE).astype(BF16)
        k_ref[...] = a[:, SWA_QW:SWA_QW + SWA_KW].astype(BF16)
        kv_ref[...] = a[:, SWA_QW + 2 * SWA_KW:]
    v_ref[...] = a[:, SWA_QW + SWA_KW:SWA_QW + 2 * SWA_KW].astype(BF16)


def _swa_proj(x, mods, norm1, w_ext, b_ext, tabs, rope):
    base = T_CTX // TM if rope else 0
    n_tiles = (T_LAT if rope else T_CTX) // TM
    n_rows = n_tiles * TM
    ncols = w_ext.shape[1]
    row = lambda i: (i, 0)
    in_specs = [pl.BlockSpec((TM, D_MODEL), lambda i: (i + base, 0)), _mod_spec(TM, base), _full((1, D_MODEL)),
                _full((D_MODEL, ncols)), _full((1, ncols))]
    args = [x, mods, norm1.reshape(1, -1), w_ext, b_ext]
    out_specs = [pl.BlockSpec((TM, SWA_QW), row), pl.BlockSpec((TM, SWA_KW), row), pl.BlockSpec((TM, SWA_KW), row)]
    out_shape = [jax.ShapeDtypeStruct((n_rows, SWA_QW), BF16), jax.ShapeDtypeStruct((n_rows, SWA_KW), BF16),
                 jax.ShapeDtypeStruct((n_rows, SWA_KW), BF16)]
    if rope:
        tab = pl.BlockSpec((TM, LANES), lambda i: (_pos_block(i + base, TM), 0))
        in_specs += [tab] * 4
        args += list(tabs)
    else:
        out_specs.append(pl.BlockSpec((TM, SWA_CW), row))
        out_shape.append(jax.ShapeDtypeStruct((n_rows, SWA_CW), F32))
    return pl.pallas_call(
        functools.partial(_swa_proj_kernel, rope=rope),
        grid=(n_tiles,),
        in_specs=in_specs, out_specs=out_specs, out_shape=out_shape,
        compiler_params=_params("arbitrary"),
        name="swa_proj_lat" if rope else "swa_proj_ctx",
    )(*args)


def _swa_ctx_kernel(q_ref, k_ref, v_ref, sink_ref, wo_ref, bo_ref, x_ref, mod_ref, o_ref, o_s):
    for n in range(SWA_KV_HEADS):
        kn = k_ref[:, n * LANES:(n + 1) * LANES]
        vn = v_ref[:, n * LANES:(n + 1) * LANES]
        for g in range(SWA_GROUP):
            j = n * SWA_GROUP + g
            sl = slice(j * LANES, (j + 1) * LANES)
            s = _dot_nt(q_ref[:, sl], kn)
            sink = sink_ref[j]
            m = jnp.maximum(jnp.max(s, axis=-1, keepdims=True), sink)
            p = jnp.exp(s - m)
            l = jnp.sum(p, axis=-1, keepdims=True) + jnp.exp(sink - m)
            o_s[:, sl] = (_dot(p.astype(BF16), vn) / l).astype(BF16)
    mix = _dot(o_s[...], wo_ref[...]) + bo_ref[...]
    o_ref[...] = x_ref[...] + mod_ref[0][2:3] * mix


def _swa_lat_kernel(q_ref, k_ref, v_ref, ck_ref, cv_ref, sink_ref, wo_ref, bo_ref, x_ref, mod_ref, o_ref, o_s):
    span = TQ + 2 * SWA_WINDOW
    start = pl.program_id(1) * TQ
    ks = pl.multiple_of(jnp.clip(start - SWA_WINDOW, 0, DEC_SEQ - span), SWA_WINDOW)
    qpos = start + lax.broadcasted_iota(jnp.int32, (TQ, span), 0)
    kpos = ks + lax.broadcasted_iota(jnp.int32, (TQ, span), 1)
    valid = jnp.abs(kpos - qpos) <= SWA_WINDOW
    for n in range(SWA_KV_HEADS):
        nl = slice(n * LANES, (n + 1) * LANES)
        kc = ck_ref[0, :, nl]
        vc = cv_ref[0, :, nl]
        kl = k_ref[pl.ds(ks, span), nl]
        vl = v_ref[pl.ds(ks, span), nl]
        for g in range(SWA_GROUP):
            j = n * SWA_GROUP + g
            sl = slice(j * LANES, (j + 1) * LANES)
            qj = q_ref[:, sl]
            s_c = _dot_nt(qj, kc)
            s_l = jnp.where(valid, _dot_nt(qj, kl), -jnp.inf)
            sink = sink_ref[j]
            m = jnp.maximum(jnp.maximum(jnp.max(s_c, axis=-1, keepdims=True),
                                        jnp.max(s_l, axis=-1, keepdims=True)), sink)
            p_c = jnp.exp(s_c - m)
            p_l = jnp.exp(s_l - m)
            l = (jnp.sum(p_c, axis=-1, keepdims=True) + jnp.sum(p_l, axis=-1, keepdims=True)
                 + jnp.exp(sink - m))
            o = _dot(p_c.astype(BF16), vc) + _dot(p_l.astype(BF16), vl)
            o_s[:, sl] = (o / l).astype(BF16)
    mix = _dot(o_s[...], wo_ref[...]) + bo_ref[...]
    o_ref[...] = x_ref[...] + mod_ref[0][2:3] * mix


def _swa_attn(x, mods, q, k, v, sink, wo_ext, bo, cache=None):
    smem = pl.BlockSpec(memory_space=pltpu.SMEM)
    if cache is None:
        nq, base = 1, 0
        tile = lambda b, qi: (b, 0)
        in_specs = [pl.BlockSpec((TQ, SWA_QW), tile), pl.BlockSpec((SEQ, SWA_KW), tile),
                    pl.BlockSpec((SEQ, SWA_KW), tile)]
        args = [q, k, v]
        kern, nb, name = _swa_ctx_kernel, BATCH, "swa_attn_ctx"
    else:
        nq, base = DEC_SEQ // TQ, T_CTX // TQ
        tile = lambda b, qi: (b * nq + qi, 0)
        seq = lambda b, qi: (b, 0)
        cspec = pl.BlockSpec((1, PAST_LEN, SWA_KW), lambda b, qi: (b, 0, 0))
        in_specs = [pl.BlockSpec((TQ, SWA_QW), tile), pl.BlockSpec((DEC_SEQ, SWA_KW), seq),
                    pl.BlockSpec((DEC_SEQ, SWA_KW), seq), cspec, cspec]
        args = [q, k, v, cache[0], cache[1]]
        kern, nb, name = _swa_lat_kernel, DEC_BATCH, "swa_attn_lat"
    xtile = lambda b, qi: (base + b * nq + qi, 0)
    in_specs += [smem, _full((SWA_QW, D_MODEL)), _full((1, D_MODEL)), pl.BlockSpec((TQ, D_MODEL), xtile),
                 pl.BlockSpec((1, 6, D_MODEL), lambda b, qi: (_cond_of_row((base + b * nq + qi) * TQ), 0, 0))]
    args += [sink, wo_ext, bo.reshape(1, -1), x, mods]
    return pl.pallas_call(
        kern,
        grid=(nb, nq),
        in_specs=in_specs,
        out_specs=pl.BlockSpec((TQ, D_MODEL), xtile),
        out_shape=jax.ShapeDtypeStruct((T_ALL, D_MODEL), F32),
        scratch_shapes=[pltpu.VMEM((TQ, SWA_QW), BF16)],
        input_output_aliases={len(args) - 2: 0},
        compiler_params=_params("arbitrary", "arbitrary"),
        name=name,
    )(*args)


CONV_EXT = CONV_TM + 2 * CONV_HALO


def _conv_kernel(xp_ref, x_ref, xn_ref, mod_ref, n1_ref, w1_ref, b1_ref, dw_ref, dwb_ref, lng_ref, lnb_ref,
                 w2_ref, b2_ref, o_ref, ext_s, y_s):
    i = pl.program_id(0)
    tiles_per_seq = DEC_SEQ // CONV_TM
    j = (i - T_CTX // CONV_TM) % tiles_per_seq
    latent = i >= T_CTX // CONV_TM
    left_ok = jnp.logical_and(latent, j > 0)
    right_ok = jnp.logical_and(latent, j < tiles_per_seq - 1)
    mod = mod_ref[0]

    def glu_of(xv):
        h = _modulated(xv, n1_ref[...], mod, 0).astype(BF16)
        u = _dot(h, w1_ref[...]) + b1_ref[...]
        return u[:, :D_MODEL] * _sigmoid(u[:, D_MODEL:])

    ext_s[CONV_HALO:CONV_HALO + CONV_TM, :] = glu_of(x_ref[...])
    ext_s[0:CONV_HALO, :] = jnp.where(left_ok, glu_of(xp_ref[...]), 0.0)
    ext_s[CONV_HALO + CONV_TM:, :] = jnp.where(right_ok, glu_of(xn_ref[...]), 0.0)

    rows = 128
    sub = 8
    first = CONV_HALO - CONV_PAD
    span = rows + (first + CONV_WIDTH - 1) // sub * sub
    for c in range(D_MODEL // LANES):
        cl = slice(c * LANES, (c + 1) * LANES)
        dwc = dw_ref[:, cl]
        for r in range(CONV_TM // rows):
            window = ext_s[r * rows:r * rows + span + sub, cl]
            acc = jnp.zeros((rows, LANES), F32)
            for res in range(sub):
                shifted = window if res == 0 else pltpu.roll(window, span + sub - res, 0)
                for q in range(span // sub - rows // sub + 1):
                    w = q * sub + res - first
                    if 0 <= w < CONV_WIDTH:
                        acc = acc + shifted[q * sub:q * sub + rows] * dwc[w:w + 1]
            y_s[r * rows:(r + 1) * rows, cl] = acc
    y = y_s[...] + dwb_ref[...]
    mu = jnp.mean(y, axis=-1, keepdims=True)
    yc = y - mu
    var = jnp.mean(yc * yc, axis=-1, keepdims=True)
    yn = yc * lax.rsqrt(var + NORM_EPS) * lng_ref[...] + lnb_ref[...]
    z = (yn * _sigmoid(yn)).astype(BF16)
    mix = _dot(z, w2_ref[...]) + b2_ref[...]
    o_ref[...] = x_ref[...] + mod[2:3] * mix


def _conv_layer(x, mods, norm1, w1, b1, dw, dwb, ln_g, ln_b, w2, b2):
    n_tiles = T_ALL // CONV_TM
    per = CONV_TM // CONV_HALO
    n_halo_blocks = T_ALL // CONV_HALO
    row = lambda i: (i, 0)
    vec = lambda v: v.reshape(1, -1)
    return pl.pallas_call(
        _conv_kernel,
        grid=(n_tiles,),
        in_specs=[pl.BlockSpec((CONV_HALO, D_MODEL), lambda i: (jnp.maximum(i * per - 1, 0), 0)),
                  pl.BlockSpec((CONV_TM, D_MODEL), row),
                  pl.BlockSpec((CONV_HALO, D_MODEL), lambda i: (jnp.minimum((i + 1) * per, n_halo_blocks - 1), 0)),
                  _mod_spec(CONV_TM), _full((1, D_MODEL)),
                  _full((D_MODEL, 2 * D_MODEL)), _full((1, 2 * D_MODEL)),
                  _full((CONV_WIDTH, D_MODEL)), _full((1, D_MODEL)), _full((1, D_MODEL)), _full((1, D_MODEL)),
                  _full((D_MODEL, D_MODEL)), _full((1, D_MODEL))],
        out_specs=pl.BlockSpec((CONV_TM, D_MODEL), row),
        out_shape=jax.ShapeDtypeStruct((T_ALL, D_MODEL), F32),
        scratch_shapes=[pltpu.VMEM((CONV_EXT, D_MODEL), F32), pltpu.VMEM((CONV_TM, D_MODEL), F32)],
        compiler_params=_params("arbitrary"),
        name="conv_module",
    )(x, x, x, mods, vec(norm1), w1.astype(BF16), vec(b1), dw, vec(dwb), vec(ln_g), vec(ln_b),
      w2.astype(BF16), vec(b2))


def _router_kernel(*refs, n_x):
    mod_ref, n2_ref, rw_ref, rb_ref, h_ref, idx_ref, gate_ref, cnt_ref = refs[n_x:]
    h = _modulated(_x_tile(refs[:n_x], TM), n2_ref[...], mod_ref[0], 3)
    _store_token_tiles(h_ref, h)
    cur = _dot_nt(rw_ref[...], h.astype(BF16)) + rb_ref[...]
    e_iota = lax.broadcasted_iota(jnp.int32, cur.shape, 0)
    vals, idxs = [], []
    for _ in range(TOP_K):
        m = jnp.max(cur, axis=0, keepdims=True)
        idx = jnp.min(jnp.where(cur == m, e_iota, N_EXPERTS), axis=0, keepdims=True)
        vals.append(m)
        idxs.append(idx)
        cur = jnp.where(e_iota == idx, -jnp.inf, cur)
    ex = [jnp.exp(v - vals[0]) for v in vals]
    tot = ex[0] + ex[1] + ex[2] + ex[3]
    for k in range(TOP_K):
        idx_ref[k:k + 1, :] = idxs[k]
        gate_ref[k:k + 1, :] = ex[k] / tot

    picked = jnp.sum(jnp.where(cur == -jnp.inf, 1.0, 0.0), axis=1, keepdims=True)

    @pl.when(pl.program_id(0) == 0)
    def _():
        cnt_ref[...] = jnp.zeros_like(cnt_ref)

    cnt_ref[...] += picked


def _router(x, mods, norm2, router_w, router_b):
    n_tiles = T_ALL // TM
    row = lambda i: (i, 0)
    col = lambda i: (0, i)
    x_specs, x_args = _x_specs(x, TM)
    return pl.pallas_call(
        functools.partial(_router_kernel, n_x=len(x_args)),
        grid=(n_tiles,),
        in_specs=x_specs + [_mod_spec(TM), _full((1, D_MODEL)),
                            _full((N_EXPERTS, D_MODEL)), _full((N_EXPERTS, 1))],
        out_specs=[pl.BlockSpec((TM * TOK_ROWS, LANES), row), pl.BlockSpec((TOP_K, TM), col),
                   pl.BlockSpec((TOP_K, TM), col), pl.BlockSpec((N_EXPERTS, 1), lambda i: (0, 0))],
        out_shape=[jax.ShapeDtypeStruct((T_ALL * TOK_ROWS, LANES), F32),
                   jax.ShapeDtypeStruct((TOP_K, T_ALL), jnp.int32),
                   jax.ShapeDtypeStruct((TOP_K, T_ALL), F32),
                   jax.ShapeDtypeStruct((N_EXPERTS, 1), F32)],
        compiler_params=_params("arbitrary"),
        name="router",
    )(*x_args, mods, norm2.reshape(1, -1), router_w.T.astype(BF16), router_b.reshape(-1, 1))


N_ASSIGN = TOP_K * T_ALL
RANK_CHUNK = 256
RANK_BLOCK = 2048


def _block_layout(counts):
    counts = counts.reshape(N_EXPERTS).astype(jnp.int32)
    padded = (counts + MOE_BM - 1) // MOE_BM * MOE_BM
    pad_end = jnp.cumsum(padded)
    pad_start = pad_end - padded
    starts = jnp.arange(MOE_BLOCKS + 1, dtype=jnp.int32) * MOE_BM
    e_ids = jnp.arange(N_EXPERTS, dtype=jnp.int32)
    used = padded > 0
    last_used = jnp.max(jnp.where(used, e_ids, 0))
    block_e = jnp.minimum(jnp.sum((pad_end[None, :] <= starts[:, None]).astype(jnp.int32), axis=1), last_used)
    n_used = (pad_end[-1] // MOE_BM).reshape(1)
    run = jnp.cumsum(used.astype(jnp.int32)) - 1
    later_used = jnp.where(jnp.logical_and(used[None, :], e_ids[None, :] > e_ids[:, None]), e_ids[None, :], N_EXPERTS)
    next_used = jnp.min(later_used, axis=1)
    next_used = jnp.where(next_used == N_EXPERTS, e_ids, next_used)
    onehot = (block_e[:, None] == e_ids[None, :]).astype(jnp.int32)
    block_next = jnp.sum(onehot * next_used[None, :], axis=1)
    block_par = jnp.sum(onehot * run[None, :], axis=1) & 1
    block_plan = (block_e, block_next, block_par, n_used)
    return pad_start.astype(F32).reshape(N_EXPERTS, 1), block_plan


def _rank_kernel(idx_ref, ps_ref, u_ref, dest_ref, carry):
    @pl.when(pl.program_id(0) == 0)
    def _():
        carry[...] = ps_ref[...]

    e_iota = lax.broadcasted_iota(jnp.int32, (N_EXPERTS, RANK_CHUNK), 0)
    for c in range(RANK_BLOCK // RANK_CHUNK):
        sl = slice(c * RANK_CHUNK, (c + 1) * RANK_CHUNK)
        hit = e_iota == idx_ref[:, sl]
        pref = _dot(jnp.where(hit, 1.0, 0.0).astype(BF16), u_ref[...])
        base = carry[...]
        dest = jnp.sum(jnp.where(hit, pref + base, 0.0), axis=0, keepdims=True) - 1.0
        dest_ref[:, sl] = dest.astype(jnp.int32)
        carry[...] = base + pref[:, RANK_CHUNK - 1:RANK_CHUNK]


def _assignment_slots(idx_t, pad_start):
    tri = jnp.asarray(np.triu(np.ones((RANK_CHUNK, RANK_CHUNK), np.float32)), dtype=BF16)
    blk = pl.BlockSpec((1, RANK_BLOCK), lambda i: (0, i))
    return pl.pallas_call(
        _rank_kernel,
        grid=(N_ASSIGN // RANK_BLOCK,),
        in_specs=[blk, _full((N_EXPERTS, 1)), _full((RANK_CHUNK, RANK_CHUNK))],
        out_specs=blk,
        out_shape=jax.ShapeDtypeStruct((1, N_ASSIGN), jnp.int32),
        scratch_shapes=[pltpu.VMEM((N_EXPERTS, 1), F32)],
        compiler_params=_params("arbitrary"),
        name="moe_rank",
    )(idx_t.reshape(1, N_ASSIGN), pad_start, tri)


INV_UNROLL = 16
INV_CHUNK = 4096
INV_STEPS = N_ASSIGN // INV_CHUNK
N_SLOTS = MOE_CAP + MOE_BM


def _inverse_kernel(dest_ref, pad_hbm, slot_ref, sem):
    i = pl.program_id(0)

    @pl.when(i == 0)
    def _():
        fill = pltpu.make_async_copy(pad_hbm, slot_ref, sem)
        fill.start()
        fill.wait()

    base = i * INV_CHUNK

    def place(j, carry):
        o0 = j * INV_UNROLL
        batch = dest_ref.at[0, 0, pl.ds(o0, INV_UNROLL)]
        dests = [batch[u] for u in range(INV_UNROLL)]
        for u in range(INV_UNROLL):
            slot_ref[dests[u]] = base + o0 + u
        return carry

    lax.fori_loop(0, INV_CHUNK // INV_UNROLL, place, 0)


def _slot_assignments(dest):
    s = np.arange(N_SLOTS, dtype=np.int32)
    pad_ids = jnp.asarray(N_ASSIGN + (s & (MOE_BM - 1)) + np.where(s >= MOE_CAP, MOE_BM, 0).astype(np.int32))
    return pl.pallas_call(
        _inverse_kernel,
        grid=(INV_STEPS,),
        in_specs=[pl.BlockSpec((1, 1, INV_CHUNK), lambda i: (i, 0, 0), memory_space=pltpu.SMEM),
                  pl.BlockSpec(memory_space=pl.ANY)],
        out_specs=pl.BlockSpec(memory_space=pltpu.SMEM),
        out_shape=jax.ShapeDtypeStruct((N_SLOTS,), jnp.int32),
        scratch_shapes=[pltpu.SemaphoreType.DMA(())],
        compiler_params=_params("arbitrary"),
        name="moe_slots",
    )(dest.reshape(INV_STEPS, 1, INV_CHUNK), pad_ids)


YG_TOKENS = N_ASSIGN + 2 * MOE_BM
BLOCK_TILE_ROWS = MOE_BM * TOK_ROWS
MOE_CHUNK = 256
ACT_BUF = 3


def _moe_kernel(be_ref, ne_ref, par_ref, nu_ref, sa_prev_ref, sa_ref, sa_next_ref, sa_next2_ref, h_hbm,
                wg_hbm, wu_hbm, wd_hbm, bg_ref, bu_ref, bd_ref, yg_hbm,
                xbuf, ybuf, x_s, wg_f, wu_f, wd_f, wgu_s, wd_s, gsem, ssem, wsem):
    i = pl.program_id(0)
    n_used = nu_ref[0]
    cur = i % 2
    nxt = 1 - cur
    gcur = lax.rem(i, 3)
    gnext = lax.rem(i + 1, 3)
    gnext2 = lax.rem(i + 2, 3)

    def weight_copies(e, slot):
        return [pltpu.make_async_copy(src.at[e], dst.at[slot], wsem.at[slot])
                for src, dst in ((wg_hbm, wg_f), (wu_hbm, wu_f), (wd_hbm, wd_f))]

    def start_gather(sa, buf, lo=0, hi=MOE_BM):
        for r in range(lo, hi):
            tok = sa[0, 0, r] & (T_ALL - 1)
            pltpu.make_async_copy(h_hbm.at[pl.ds(tok * TOK_ROWS, TOK_ROWS)],
                                  xbuf.at[buf, pl.ds(r * TOK_ROWS, TOK_ROWS)], gsem.at[buf]).start()

    def start_scatter(sa, buf):
        for r in range(MOE_BM):
            pltpu.async_copy(ybuf.at[buf, pl.ds(r * TOK_ROWS, TOK_ROWS)],
                             yg_hbm.at[pl.ds(sa[0, 0, r] * TOK_ROWS, TOK_ROWS)], ssem.at[buf], priority=1)

    def wait_gather(buf):
        pltpu.make_async_copy(h_hbm.at[pl.ds(0, BLOCK_TILE_ROWS)], xbuf.at[buf], gsem.at[buf]).wait()

    def wait_scatter(buf):
        pltpu.make_async_copy(ybuf.at[buf], yg_hbm.at[pl.ds(0, BLOCK_TILE_ROWS)], ssem.at[buf]).wait()

    @pl.when(i == 0)
    def _():
        for cp in weight_copies(be_ref[0], 0):
            cp.start()
        start_gather(sa_ref, 0)
        start_gather(sa_next_ref, 1)
        ybuf[...] = jnp.zeros_like(ybuf)
        pltpu.make_async_copy(ybuf.at[0], yg_hbm.at[pl.ds(N_ASSIGN * TOK_ROWS, BLOCK_TILE_ROWS)], ssem.at[0]).start()

    @pl.when(i <= n_used)
    def _():
        e = be_ref[i]

        @pl.when(jnp.logical_or(i == 0, e != be_ref[jnp.maximum(i - 1, 0)]))
        def _():
            slot = par_ref[i]
            for cp in weight_copies(e, slot):
                cp.wait()
            for c in range(D_MODEL // MOE_CHUNK):
                cs = slice(c * MOE_CHUNK, (c + 1) * MOE_CHUNK)
                wgu_s[:, 2 * c * MOE_CHUNK:(2 * c + 1) * MOE_CHUNK] = wg_f[slot, :, cs].astype(BF16)
                wgu_s[:, (2 * c + 1) * MOE_CHUNK:(2 * c + 2) * MOE_CHUNK] = wu_f[slot, :, cs].astype(BF16)
            wd_s[...] = wd_f[slot].astype(BF16)

            @pl.when(ne_ref[i] != e)
            def _():
                for cp in weight_copies(ne_ref[i], 1 - slot):
                    cp.start()

        wait_gather(gcur)
        x_s[...] = _load_token_tiles(xbuf.at[gcur], MOE_BM).astype(BF16)
        n_chunks = D_MODEL // MOE_CHUNK
        slabs = MOE_CHUNK // LANES
        for c in range(n_chunks):
            cs = slice(c * MOE_CHUNK, (c + 1) * MOE_CHUNK)
            start_gather(sa_next2_ref, gnext2, c * MOE_BM // n_chunks, (c + 1) * MOE_BM // n_chunks)
            gu = _dot(x_s[...], wgu_s[:, 2 * c * MOE_CHUNK:(2 * c + 2) * MOE_CHUNK])
            g = jnp.minimum(gu[:, :MOE_CHUNK] + bg_ref[0, :, cs], SWIGLU_LIMIT)
            u = jnp.clip(gu[:, MOE_CHUNK:] + bu_ref[0, :, cs], -SWIGLU_LIMIT, SWIGLU_LIMIT)
            act = (u + 1.0) * (g * _sigmoid(SWIGLU_ALPHA * g))
            for j in range(slabs):
                jj = c * slabs + j
                xbuf[ACT_BUF, jj * MOE_BM:(jj + 1) * MOE_BM, :] = act[:, j * LANES:(j + 1) * LANES]
        wait_scatter(cur)
        start_scatter(sa_prev_ref, nxt)
        act_b =jnp.concatenate([xbuf[ACT_BUF, j * MOE_BM:(j + 1) * MOE_BM, :] for j in range(TOK_ROWS)],
                                axis=-1).astype(BF16)
        half = D_MODEL // 2
        for c in range(2):
            y = _dot(act_b, wd_s[:, c * half:(c + 1) * half]) + bd_ref[0, :, c * half:(c + 1) * half]
            for j in range(TOK_ROWS // 2):
                jj = c * (TOK_ROWS // 2) + j
                ybuf.at[cur][pl.ds(jj, MOE_BM, stride=TOK_ROWS), :] = y[:, j * LANES:(j + 1) * LANES]

    @pl.when(i == n_used)
    def _():
        wait_gather(gnext)
        wait_gather(gnext2)
        wait_scatter(nxt)


def _moe_experts(h, slots, block_plan, wg, bg, wu, bu, wd, bd):
    bspec = pl.BlockSpec((1, 1, D_MODEL), lambda i, be, *_: (be[i], 0, 0))
    sspec = lambda f: pl.BlockSpec((1, 1, MOE_BM), lambda i, *_: (f(i), 0, 0), memory_space=pltpu.SMEM)
    hbm = pl.BlockSpec(memory_space=pl.ANY)
    b3 = lambda b: b.reshape(N_EXPERTS, 1, D_MODEL)
    slots3 = slots.reshape(MOE_BLOCKS + 1, 1, MOE_BM)
    wbuf = pltpu.VMEM((2, D_MODEL, D_MODEL), F32)
    return pl.pallas_call(
        _moe_kernel,
        grid_spec=pltpu.PrefetchScalarGridSpec(
            num_scalar_prefetch=4,
            grid=(MOE_BLOCKS + 1,),
            in_specs=[sspec(lambda i: jnp.where(i == 0, MOE_BLOCKS, i - 1)), sspec(lambda i: i),
                      sspec(lambda i: jnp.minimum(i + 1, MOE_BLOCKS)), sspec(lambda i: jnp.minimum(i + 2, MOE_BLOCKS)),
                      hbm, hbm, hbm, hbm, bspec, bspec, bspec],
            out_specs=hbm,
            scratch_shapes=[pltpu.VMEM((ACT_BUF + 1, BLOCK_TILE_ROWS, LANES), F32),
                            pltpu.VMEM((2, BLOCK_TILE_ROWS, LANES), F32),
                            pltpu.VMEM((MOE_BM, D_MODEL), BF16),
                            wbuf, wbuf, wbuf,
                            pltpu.VMEM((D_MODEL, 2 * D_MODEL), BF16), pltpu.VMEM((D_MODEL, D_MODEL), BF16),
                            pltpu.SemaphoreType.DMA((3,)), pltpu.SemaphoreType.DMA((2,)),
                            pltpu.SemaphoreType.DMA((2,))]),
        out_shape=jax.ShapeDtypeStruct((YG_TOKENS * TOK_ROWS, LANES), F32),
        compiler_params=_params("arbitrary"),
        name="moe_experts",
    )(*block_plan, slots3, slots3, slots3, slots3, h, wg, wu, wd, b3(bg), b3(bu), b3(bd))


def _combine_kernel(*refs, final, n_x):
    x_refs = refs[:n_x]
    if final:
        y0_ref, y1_ref, y2_ref, y3_ref, gate_ref, mod_ref, fg_ref, o_ref = refs[n_x:]
    else:
        y0_ref, y1_ref, y2_ref, y3_ref, gate_ref, mod_ref, o_ref = refs[n_x:]
    gate = gate_ref[...]
    n = o_ref.shape[0]
    g2 = mod_ref[0][5:6]
    sumsq = jnp.zeros((n, 1), F32)
    for j in range(TOK_ROWS):
        rows = pl.ds(j, n, stride=TOK_ROWS)
        sl = slice(j * LANES, (j + 1) * LANES)
        acc = y0_ref[rows, :] * gate[:, 0:1]
        for k, y_ref in enumerate((y1_ref, y2_ref, y3_ref), start=1):
            acc = acc + y_ref[rows, :] * gate[:, k:k + 1]
        xn = _x_tile(x_refs, TM, sl) + g2[:, sl] * acc
        o_ref[:, sl] = xn
        if final:
            sumsq = sumsq + jnp.sum(xn * xn, axis=-1, keepdims=True)
    if final:
        inv = lax.rsqrt(sumsq * (1.0 / D_MODEL) + NORM_EPS)
        o_ref[...] = o_ref[...] * inv * fg_ref[...]


def _combine(x, yg, gates, mods, final_g=None, base_rows=0, n_rows=T_ALL):
    final = final_g is not None
    base = base_rows // TM
    n_all = T_ALL // TM
    row = lambda i: (i + base, 0)
    ysp = lambda k: pl.BlockSpec((TM * TOK_ROWS, LANES), lambda i: (k * n_all + i + base, 0))
    if isinstance(x, tuple):
        x_specs, x_args = _x_specs(x, TM)
    else:
        x_specs, x_args = [pl.BlockSpec((TM, D_MODEL), row)], [x]
    in_specs = x_specs + [ysp(0), ysp(1), ysp(2), ysp(3), pl.BlockSpec((TM, TOP_K), row), _mod_spec(TM, base)]
    args = x_args + [yg, yg, yg, yg, gates, mods]
    if final:
        in_specs.append(_full((1, D_MODEL)))
        args.append(final_g.reshape(1, -1))
    return pl.pallas_call(
        functools.partial(_combine_kernel, final=final, n_x=len(x_args)),
        grid=(n_rows // TM,),
        in_specs=in_specs,
        out_specs=pl.BlockSpec((TM, D_MODEL), (lambda i: (i, 0)) if final else row),
        out_shape=jax.ShapeDtypeStruct((n_rows, D_MODEL), F32),
        input_output_aliases={} if final or len(x_args) == 2 else {0: 0},
        compiler_params=_params("arbitrary"),
        name="moe_combine_final" if final else "moe_combine",
    )(*args)


def _moe_layer(x, mods, norm2, router_w, router_b, wg, bg, wu, bu, wd, bd, final_g=None):
    h, idx_t, gate_t, counts = _router(x, mods, norm2, router_w, router_b)
    pad_start, block_plan = _block_layout(counts)
    slots = _slot_assignments(_assignment_slots(idx_t, pad_start))
    yg = _moe_experts(h, slots, block_plan, wg, bg, wu, bu, wd, bd)
    gates = gate_t.T
    if final_g is None:
        return _combine(x, yg, gates, mods)
    return (_combine(x, yg, gates, mods, final_g, 0, T_CTX),
            _combine(x, yg, gates, mods, final_g, T_CTX, T_LAT))


def _pad_lanes(w, lead_pad, width):
    tail = width - lead_pad - w.shape[-1]
    cfg = [(0, 0)] * (w.ndim - 1) + [(lead_pad, tail)]
    return jnp.pad(w, cfg)


def _mla_weights(wa, wuq, wukv, wo):
    krope_w = wa[:, MLA_Q_RANK + MLA_KV_RANK:]
    wa_ext = jnp.concatenate([wa[:, :MLA_Q_RANK + MLA_KV_RANK],
                              _pad_lanes(krope_w, MLA_ROPE_OFF, LANES),
                              _pad_lanes(krope_w, 0, LANES)], axis=1).astype(BF16)
    q3 = wuq.reshape(MLA_Q_RANK, MLA_HEADS, MLA_NOPE + MLA_ROPE)
    wuq_ext = jnp.concatenate([q3[..., :MLA_NOPE], _pad_lanes(q3[..., MLA_NOPE:], MLA_ROPE // 2, LANES - MLA_NOPE)],
                              axis=-1).reshape(MLA_Q_RANK, MLA_HEADS * LANES).astype(BF16)
    kv3 = wukv.reshape(MLA_KV_RANK, MLA_HEADS, MLA_NOPE + MLA_V)
    wk_ext = _pad_lanes(kv3[..., :MLA_NOPE], 0, LANES).reshape(MLA_KV_RANK, MLA_HEADS * LANES).astype(BF16)
    wv = kv3[..., MLA_NOPE:].reshape(MLA_KV_RANK, MLA_HEADS * MLA_V).astype(BF16)
    return wa_ext, wuq_ext, wk_ext, wv, wo.astype(BF16)


def _swa_weights(wqkv, bqkv, wo):
    nq = SWA_HEADS * SWA_HEAD_DIM
    nkv = SWA_KV_HEADS * SWA_HEAD_DIM

    def slabs(w, heads):
        lead = w.shape[:-1]
        return _pad_lanes(w.reshape(lead + (heads, SWA_HEAD_DIM)), 0, LANES).reshape(lead + (heads * LANES,))

    def ext(w, with_compact):
        parts = [slabs(w[..., :nq], SWA_HEADS), slabs(w[..., nq:nq + nkv], SWA_KV_HEADS),
                 slabs(w[..., nq + nkv:], SWA_KV_HEADS)]
        if with_compact:
            parts.append(w[..., nq:])
        return jnp.concatenate(parts, axis=-1)

    b2 = bqkv.reshape(1, -1)
    wo3 = wo.reshape(SWA_HEADS, SWA_HEAD_DIM, D_MODEL)
    wo_ext = jnp.pad(wo3, ((0, 0), (0, LANES - SWA_HEAD_DIM), (0, 0))).reshape(SWA_QW, D_MODEL).astype(BF16)
    return (ext(wqkv, True).astype(BF16), ext(b2, True), ext(wqkv, False).astype(BF16), ext(b2, False), wo_ext)


def kernel(x_prompt, x_sample, cache_l0_ckv, cache_l0_krope, cache_l1_k, cache_l1_v, cache_l3_ckv, cache_l3_krope, c, c_ctx, l0_mla_wa, l0_mla_q_norm, l0_mla_wuq, l0_mla_kv_norm, l0_mla_wukv, l0_mla_wo, l0_mod_w, l0_mod_b, l0_norm1, l0_norm2, l0_router_w, l0_router_b, l0_moe_wg, l0_moe_bg, l0_moe_wu, l0_moe_bu, l0_moe_wd, l0_moe_bd, l1_swa_wqkv, l1_swa_bqkv, l1_swa_sink, l1_swa_wo, l1_swa_bo, l1_mod_w, l1_mod_b, l1_norm1, l1_norm2, l1_router_w, l1_router_b, l1_moe_wg, l1_moe_bg, l1_moe_wu, l1_moe_bu, l1_moe_wd, l1_moe_bd, l2_conv_w1, l2_conv_b1, l2_conv_dw, l2_conv_dwb, l2_conv_ln_g, l2_conv_ln_b, l2_conv_w2, l2_conv_b2, l2_mod_w, l2_mod_b, l2_norm1, l2_norm2, l2_router_w, l2_router_b, l2_moe_wg, l2_moe_bg, l2_moe_wu, l2_moe_bu, l2_moe_wd, l2_moe_bd, l3_mla_wa, l3_mla_q_norm, l3_mla_wuq, l3_mla_kv_norm, l3_mla_wukv, l3_mla_wo, l3_mod_w, l3_mod_b, l3_norm1, l3_norm2, l3_router_w, l3_router_b, l3_moe_wg, l3_moe_bg, l3_moe_wu, l3_moe_bu, l3_moe_wd, l3_moe_bd, final_norm):
    x = (x_prompt.reshape(T_CTX, D_MODEL), x_sample.reshape(T_LAT, D_MODEL))
    cond =jnp.concatenate([c, c_ctx[None, :], jnp.zeros((N_COND - DEC_BATCH - 1, D_MODEL), F32)], axis=0)

    mla_tabs = (_rope_tables(MLA_ROPE, MLA_ROPE_LEAD, TM, MLA_SCALE) + _rope_tables(MLA_ROPE, MLA_ROPE_LEAD, TM))
    swa_tabs = _rope_tables(SWA_HEAD_DIM, 0, TM, SWA_SCALE) + _rope_tables(SWA_HEAD_DIM, 0, TM)

    def mla_layer(x, mods, norm1, wa, q_norm, wuq, kv_norm, wukv, wo, cache_ckv, cache_krope):
        wa_ext, wuq_ext, wk_ext, wv_ext, wo_ext = _mla_weights(wa, wuq, wukv, wo)
        q, ckv, kp, kst = _mla_proj(x, mods, norm1, wa_ext, q_norm, wuq_ext, kv_norm, mla_tabs)
        cache = (cache_ckv, _pad_lanes(cache_krope, MLA_ROPE_OFF, LANES).astype(BF16))
        if isinstance(x, tuple):
            x = (_mla_attn(x[0], mods, q, ckv, kp, wk_ext, wv_ext, wo_ext, own_rows=True),
                 _mla_attn(x[1], mods, q, ckv, kp, wk_ext, wv_ext, wo_ext, cache, own_rows=True))
        else:
            x = _mla_attn(x, mods, q, ckv, kp, wk_ext, wv_ext, wo_ext)
            x = _mla_attn(x, mods, q, ckv, kp, wk_ext, wv_ext, wo_ext, cache)
        return x, (ckv[:T_CTX].reshape(BATCH, SEQ, MLA_KV_RANK), kst[:T_CTX].reshape(BATCH, SEQ, MLA_ROPE))

    def moe(x, mods, norm2, rw, rb, wg, bg, wu, bu, wd, bd):
        return _moe_layer(x, mods, norm2, rw, rb, wg, bg, wu, bu, wd, bd)

    mods = _modulation(cond, l0_mod_w, l0_mod_b)
    x, (st0_ckv, st0_krope) = mla_layer(x, mods, l0_norm1, l0_mla_wa, l0_mla_q_norm, l0_mla_wuq, l0_mla_kv_norm,
                                        l0_mla_wukv, l0_mla_wo, cache_l0_ckv, cache_l0_krope)
    x = moe(x, mods, l0_norm2, l0_router_w, l0_router_b, l0_moe_wg, l0_moe_bg, l0_moe_wu, l0_moe_bu,
            l0_moe_wd, l0_moe_bd)

    mods = _modulation(cond, l1_mod_w, l1_mod_b)
    w_ctx, b_ctx, w_lat, b_lat, swa_wo_ext = _swa_weights(l1_swa_wqkv, l1_swa_bqkv, l1_swa_wo)
    q_c, k_c, v_c, kv_c = _swa_proj(x, mods, l1_norm1, w_ctx, b_ctx, None, rope=False)
    q_l, k_l, v_l = _swa_proj(x, mods, l1_norm1, w_lat, b_lat, swa_tabs, rope=True)
    x = _swa_attn(x, mods, q_c, k_c, v_c, l1_swa_sink, swa_wo_ext, l1_swa_bo)
    nkv = SWA_KV_HEADS * SWA_HEAD_DIM
    cache_k = _pad_lanes(cache_l1_k, 0, LANES).reshape(DEC_BATCH, PAST_LEN, SWA_KW).astype(BF16)
    cache_v = _pad_lanes(cache_l1_v, 0, LANES).reshape(DEC_BATCH, PAST_LEN, SWA_KW).astype(BF16)
    x = _swa_attn(x, mods, q_l, k_l, v_l, l1_swa_sink, swa_wo_ext, l1_swa_bo, (cache_k, cache_v))
    st1_k = kv_c[:, :nkv].reshape(BATCH, SEQ, SWA_KV_HEADS, SWA_HEAD_DIM)
    st1_v = kv_c[:, nkv:].reshape(BATCH, SEQ, SWA_KV_HEADS, SWA_HEAD_DIM)
    x = moe(x, mods, l1_norm2, l1_router_w, l1_router_b, l1_moe_wg, l1_moe_bg, l1_moe_wu, l1_moe_bu,
            l1_moe_wd, l1_moe_bd)

    mods = _modulation(cond, l2_mod_w, l2_mod_b)
    x = _conv_layer(x, mods, l2_norm1, l2_conv_w1, l2_conv_b1, l2_conv_dw, l2_conv_dwb, l2_conv_ln_g,
                    l2_conv_ln_b, l2_conv_w2, l2_conv_b2)
    x = moe(x, mods, l2_norm2, l2_router_w, l2_router_b, l2_moe_wg, l2_moe_bg, l2_moe_wu, l2_moe_bu,
            l2_moe_wd, l2_moe_bd)

    mods = _modulation(cond, l3_mod_w, l3_mod_b)
    x, (st3_ckv, st3_krope) = mla_layer(x, mods, l3_norm1, l3_mla_wa, l3_mla_q_norm, l3_mla_wuq, l3_mla_kv_norm,
                                        l3_mla_wukv, l3_mla_wo, cache_l3_ckv, cache_l3_krope)
    y_ctx, y_lat = _moe_layer(x, mods, l3_norm2, l3_router_w, l3_router_b, l3_moe_wg, l3_moe_bg, l3_moe_wu,
                              l3_moe_bu, l3_moe_wd, l3_moe_bd, final_g=final_norm)
    y_prompt = y_ctx.reshape(BATCH, SEQ, D_MODEL)
    y_sample = y_lat.reshape(DEC_BATCH, DEC_SEQ, D_MODEL)
    return (y_prompt, y_sample, st0_ckv, st0_krope, st1_k, st1_v, st3_ckv, st3_krope)
```

```python
import functools

import jax
import jax.numpy as jnp
import numpy as np
from jax import lax
from jax.experimental import pallas as pl
from jax.experimental.pallas import tpu as pltpu

F32 = jnp.float32
BF16 = jnp.bfloat16

D_MODEL = 1024
BATCH = 32
SEQ = 256
DEPTH = 4
DEC_BATCH = 4
DEC_SEQ = 2048
PAST_LEN = 256
GRID_W = 64
NORM_EPS = 1e-6
ROPE_THETA = 10000.0

MLA_HEADS = 16
MLA_NOPE = 64
MLA_ROPE = 32
MLA_V = 64
MLA_Q_RANK = 384
MLA_KV_RANK = 256
MLA_SCALE = (MLA_NOPE + MLA_ROPE) ** -0.5

SWA_HEADS = 16
SWA_KV_HEADS = 4
SWA_GROUP = 4
SWA_HEAD_DIM = 64
SWA_WINDOW = 128
SWA_SCALE = SWA_HEAD_DIM ** -0.5

CONV_WIDTH = 31
CONV_PAD = CONV_WIDTH // 2

N_EXPERTS = 32
TOP_K = 4
SWIGLU_ALPHA = 1.702
SWIGLU_LIMIT = 7.0

LANES = 128
T_CTX = BATCH * SEQ
T_LAT = DEC_BATCH * DEC_SEQ
T_ALL = T_CTX + T_LAT
CTX_COND = DEC_BATCH
N_COND = 8

TM = 512
TQ = 256
MLA_LAT_TQ = 256
CONV_TM = 256
CONV_HALO = 16
MOE_BM = 256
MOE_BLOCKS = (T_ALL * TOP_K + N_EXPERTS * (MOE_BM - 1) + MOE_BM - 1) // MOE_BM
MOE_CAP = MOE_BLOCKS * MOE_BM
VMEM_LIMIT = 56 * 1024 * 1024


def _params(*sem):
    return pltpu.CompilerParams(dimension_semantics=sem, vmem_limit_bytes=VMEM_LIMIT)


def _cond_of_row(row0):
    return jnp.where(row0 < T_CTX, CTX_COND, (row0 - T_CTX) // DEC_SEQ)


def _pos_block(i, tm):
    row0 = i * tm
    return jnp.where(row0 < T_CTX, DEC_SEQ // tm, ((row0 - T_CTX) % DEC_SEQ) // tm)


def _rms(x, g):
    ms = jnp.mean(x * x, axis=-1, keepdims=True)
    return x * lax.rsqrt(ms + NORM_EPS) * g


def _modulated(x, g, mod, j):
    return _rms(x, g) * (1.0 + mod[j + 1:j + 2]) + mod[j:j + 1]


def _sigmoid(x):
    return 1.0 / (1.0 + jnp.exp(-x))


def _dot(a, b):
    return jnp.dot(a, b, preferred_element_type=F32)


def _dot_nt(a, b):
    return lax.dot_general(a, b, (((1,), (1,)), ((), ())), preferred_element_type=F32)


TOK_ROWS = D_MODEL // LANES


def _store_token_tiles(ref, v):
    n = v.shape[0]
    for j in range(TOK_ROWS):
        ref[pl.ds(j, n, stride=TOK_ROWS), :] = v[:, j * LANES:(j + 1) * LANES]


def _load_token_tiles(ref, n):
    return jnp.concatenate([ref[pl.ds(j, n, stride=TOK_ROWS), :] for j in range(TOK_ROWS)], axis=-1)


def _rope_slab(a, c, s, shift):
    return a * c + (pltpu.roll(a, shift, 1) - pltpu.roll(a, LANES - shift, 1)) * s


def _mod_kernel(c_ref, w_ref, b_ref, o_ref):
    c = c_ref[...]
    s = (c * _sigmoid(c)).astype(BF16)
    o_ref[...] = _dot(s, w_ref[...].astype(BF16)) + b_ref[...]


def _modulation(cond, w, b):
    tn = 1536
    out = pl.pallas_call(
        _mod_kernel,
        grid=(6 * D_MODEL // tn,),
        in_specs=[pl.BlockSpec((N_COND, D_MODEL), lambda j: (0, 0)),
                  pl.BlockSpec((D_MODEL, tn), lambda j: (0, j)),
                  pl.BlockSpec((1, tn), lambda j: (0, j))],
        out_specs=pl.BlockSpec((N_COND, tn), lambda j: (0, j)),
        out_shape=jax.ShapeDtypeStruct((N_COND, 6 * D_MODEL), F32),
        compiler_params=_params("arbitrary"),
        name="modulation",
    )(cond, w, b.reshape(1, -1))
    return out.reshape(N_COND, 6, D_MODEL)


def _mod_spec(tm, base=0):
    return pl.BlockSpec((1, 6, D_MODEL), lambda i, *_: (_cond_of_row((i + base) * tm), 0, 0))


def _x_specs(x, tm):
    if isinstance(x, tuple):
        n_ctx = T_CTX // tm
        return ([pl.BlockSpec((tm, D_MODEL), lambda i, *_: (jnp.minimum(i, n_ctx - 1), 0)),
                 pl.BlockSpec((tm, D_MODEL), lambda i, *_: (jnp.maximum(i - n_ctx, 0), 0))], list(x))
    return [pl.BlockSpec((tm, D_MODEL), lambda i, *_: (i, 0))], [x]


def _x_tile(x_refs, tm, cols=slice(None)):
    if len(x_refs) == 2:
        return jnp.where(pl.program_id(0) < T_CTX // tm, x_refs[0][:, cols], x_refs[1][:, cols])
    return x_refs[0][:, cols]


def _full(shape):
    nd = len(shape)
    return pl.BlockSpec(shape, lambda *_: (0,) * nd, pipeline_mode=pl.Buffered(1))


def _rope_cos_sin(n_tok, rot_dim):
    rows = n_tok // GRID_W
    row = np.repeat(np.arange(rows, dtype=np.float32), GRID_W)
    col = np.tile(np.arange(GRID_W, dtype=np.float32), rows)
    n_freq = rot_dim // 4
    inv_freq = np.float32(ROPE_THETA) ** (-np.arange(n_freq, dtype=np.float32) / np.float32(n_freq))
    ang = np.concatenate([row[:, None] * inv_freq, col[:, None] * inv_freq], axis=-1).astype(np.float32)
    return np.cos(ang), np.sin(ang)


def _rope_tables(rot_dim, lead, tm, scale=1.0):
    cos, sin = _rope_cos_sin(DEC_SEQ, rot_dim)
    gap = rot_dim // 2 if lead else 0
    tail = LANES - lead - gap - rot_dim

    def slab(lead_val, rot):
        n = rot.shape[0]
        return np.concatenate([np.full((n, lead), lead_val, np.float32), np.zeros((n, gap), np.float32), rot,
                               np.zeros((n, tail), np.float32)], axis=-1)

    c = slab(1.0, np.concatenate([cos, cos], axis=-1))
    s = slab(0.0, np.concatenate([sin, sin], axis=-1))
    c_id = slab(1.0, np.ones((tm, rot_dim), np.float32))
    s_id = np.zeros((tm, LANES), np.float32)
    scale = np.float32(scale)
    return (jnp.asarray(np.concatenate([c, c_id], axis=0) * scale),
            jnp.asarray(np.concatenate([s, s_id], axis=0) * scale))


MLA_A_COLS = MLA_Q_RANK + MLA_KV_RANK + 2 * LANES
MLA_ROPE_LEAD = MLA_NOPE
MLA_ROPE_OFF = MLA_NOPE + MLA_ROPE // 2


def _mla_proj_kernel(*refs, n_x):
    (mod_ref, n1_ref, wa_ref, qn_ref, wuq_ref, kvn_ref, cq_ref, sq_ref, ck_ref, sk_ref,
     q_ref, ckv_ref, kp_ref, kst_ref) = refs[n_x:]
    h = _modulated(_x_tile(refs[:n_x], TM), n1_ref[...], mod_ref[0], 0).astype(BF16)
    a = _dot(h, wa_ref[...])
    cq = _rms(a[:, :MLA_Q_RANK], qn_ref[...]).astype(BF16)
    ckv_ref[...] = _rms(a[:, MLA_Q_RANK:MLA_Q_RANK + MLA_KV_RANK], kvn_ref[...])
    k0 = MLA_Q_RANK + MLA_KV_RANK
    kp = a[:, k0:k0 + LANES]
    kst_ref[...] = a[:, k0 + LANES:k0 + LANES + MLA_ROPE]
    qa = _dot(cq, wuq_ref[...])
    cq_t = cq_ref[...]
    is_ctx = pl.program_id(0) < T_CTX // TM

    @pl.when(is_ctx)
    def _():
        kp_ref[...] = (kp * ck_ref[...]).astype(BF16)
        for hd in range(MLA_HEADS):
            sl = slice(hd * LANES, (hd + 1) * LANES)
            q_ref[:, sl] = (qa[:, sl] * cq_t).astype(BF16)

    @pl.when(jnp.logical_not(is_ctx))
    def _():
        sq_t = sq_ref[...]
        kp_ref[...] = _rope_slab(kp, ck_ref[...], sk_ref[...], MLA_ROPE // 2).astype(BF16)
        for hd in range(MLA_HEADS):
            sl = slice(hd * LANES, (hd + 1) * LANES)
            q_ref[:, sl] = _rope_slab(qa[:, sl], cq_t, sq_t, MLA_ROPE // 2).astype(BF16)


def _mla_proj(x, mods, norm1, wa_ext, q_norm, wuq_ext, kv_norm, tabs):
    cq, sq, ck, sk = tabs
    n_tiles = T_ALL // TM
    row = lambda i: (i, 0)
    tab = pl.BlockSpec((TM, LANES), lambda i: (_pos_block(i, TM), 0))
    x_specs, x_args = _x_specs(x, TM)
    return pl.pallas_call(
        functools.partial(_mla_proj_kernel, n_x=len(x_args)),
        grid=(n_tiles,),
        in_specs=x_specs + [_mod_spec(TM), _full((1, D_MODEL)),
                            _full((D_MODEL, MLA_A_COLS)), _full((1, MLA_Q_RANK)),
                            _full((MLA_Q_RANK, MLA_HEADS * LANES)), _full((1, MLA_KV_RANK)),
                            tab, tab, tab, tab],
        out_specs=[pl.BlockSpec((TM, MLA_HEADS * LANES), row), pl.BlockSpec((TM, MLA_KV_RANK), row),
                   pl.BlockSpec((TM, LANES), row), pl.BlockSpec((TM, MLA_ROPE), row)],
        out_shape=[jax.ShapeDtypeStruct((T_ALL, MLA_HEADS * LANES), BF16),
                   jax.ShapeDtypeStruct((T_ALL, MLA_KV_RANK), F32),
                   jax.ShapeDtypeStruct((T_ALL, LANES), BF16),
                   jax.ShapeDtypeStruct((T_ALL, MLA_ROPE), F32)],
        compiler_params=_params("arbitrary"),
        name="mla_proj",
    )(*x_args, mods, norm1.reshape(1, -1), wa_ext, q_norm.reshape(1, -1), wuq_ext, kv_norm.reshape(1, -1),
      cq, sq, ck, sk)


def _mla_attn_kernel(*refs, n_lat, n_cache):
    if n_cache:
        (q_ref, ckv_ref, kp_ref, cckv_ref, ckp_ref, wk_ref, wv_ref, wo_ref, x_ref, mod_ref,
         o_ref, k_s, v_s, o_s) = refs
    else:
        (q_ref, ckv_ref, kp_ref, wk_ref, wv_ref, wo_ref, x_ref, mod_ref, o_ref, k_s, v_s, o_s) = refs

    def expand(ckv_of, kp_of, off, rows):
        rc = min(rows, 256)
        for r0 in range(0, rows, rc):
            c = ckv_of(r0, rc).astype(BF16)
            kp = kp_of(r0, rc).astype(F32)
            kn = _dot(c, wk_ref[...])
            v_s[off + r0:off + r0 + rc, :] = _dot(c, wv_ref[...]).astype(BF16)
            for hd in range(MLA_HEADS):
                sl = slice(hd * LANES, (hd + 1) * LANES)
                k_s[off + r0:off + r0 + rc, sl] = (kn[:, sl] + kp).astype(BF16)

    @pl.when(pl.program_id(1) == 0)
    def _():
        if n_cache:
            expand(lambda r, n: cckv_ref[0, r:r + n, :], lambda r, n: ckp_ref[0, r:r + n, :], 0, n_cache)
        expand(lambda r, n: ckv_ref[r:r + n, :], lambda r, n: kp_ref[r:r + n, :], n_cache, n_lat)

    def head(hd, v_pair):
        sl = slice(hd * LANES, (hd + 1) * LANES)
        s = _dot_nt(q_ref[:, sl], k_s[:, sl])
        m = jnp.max(s, axis=-1, keepdims=True)
        p = jnp.exp(s - m)
        l = jnp.sum(p, axis=-1, keepdims=True)
        return _dot(p.astype(BF16), v_pair) / l

    low_half = lax.broadcasted_iota(jnp.int32, (q_ref.shape[0], LANES), 1) < MLA_V
    for hp in range(MLA_HEADS // 2):
        sl = slice(hp * LANES, (hp + 1) * LANES)
        v_pair = v_s[:, sl]
        o_s[:, sl] = jnp.where(low_half, head(2 * hp, v_pair), head(2 * hp + 1, v_pair)).astype(BF16)
    mix = _dot(o_s[...], wo_ref[...])
    o_ref[...] = x_ref[...] + mod_ref[0][2:3] * mix


def _mla_attn(x, mods, q, ckv, kp, wk_ext, wv_ext, wo_ext, cache=None, own_rows=False):
    hw = MLA_HEADS * LANES
    tq = TQ if cache is None else MLA_LAT_TQ
    if cache is None:
        n_lat, n_cache, nb, nq, base = SEQ, 0, BATCH, SEQ // tq, 0
    else:
        n_lat, n_cache, nb, nq, base = DEC_SEQ, PAST_LEN, DEC_BATCH, DEC_SEQ // tq, T_CTX // tq
    seq_base = base * tq // n_lat
    tile = lambda b, qi: (base + b * nq + qi, 0)
    seq = lambda b, qi: (seq_base + b, 0)
    seq_mode = dict(pipeline_mode=pl.Buffered(1)) if cache is not None else {}
    in_specs = [pl.BlockSpec((tq, hw), tile),
                pl.BlockSpec((n_lat, MLA_KV_RANK), seq, **seq_mode),
                pl.BlockSpec((n_lat, LANES), seq, **seq_mode)]
    args = [q, ckv, kp]
    if cache is not None:
        in_specs += [pl.BlockSpec((1, n_cache, MLA_KV_RANK), lambda b, qi: (b, 0, 0)),
                     pl.BlockSpec((1, n_cache, LANES), lambda b, qi: (b, 0, 0))]
        args += list(cache)
    vw = MLA_HEADS * MLA_V
    x_tile = (lambda b, qi: (b * nq + qi, 0)) if own_rows else tile
    in_specs += [_full((MLA_KV_RANK, hw)), _full((MLA_KV_RANK, vw)), _full((vw, D_MODEL)),
                 pl.BlockSpec((tq, D_MODEL), x_tile),
                 pl.BlockSpec((1, 6, D_MODEL), lambda b, qi: (_cond_of_row((base + b * nq + qi) * tq), 0, 0))]
    args += [wk_ext, wv_ext, wo_ext, x, mods]
    sk = n_cache + n_lat
    return pl.pallas_call(
        functools.partial(_mla_attn_kernel, n_lat=n_lat, n_cache=n_cache),
        grid=(nb, nq),
        in_specs=in_specs,
        out_specs=pl.BlockSpec((tq, D_MODEL), x_tile),
        out_shape=jax.ShapeDtypeStruct(x.shape, F32),
        scratch_shapes=[pltpu.VMEM((sk, hw), BF16), pltpu.VMEM((sk, vw), BF16), pltpu.VMEM((tq, vw), BF16)],
        input_output_aliases={} if own_rows else {len(args) - 2: 0},
        compiler_params=_params("arbitrary", "arbitrary"),
        name="mla_attn_lat" if cache is not None else "mla_attn_ctx",
    )(*args)


SWA_QW = SWA_HEADS * LANES
SWA_KW = SWA_KV_HEADS * LANES
SWA_CW = 2 * SWA_KV_HEADS * SWA_HEAD_DIM


def _swa_proj_kernel(*refs, rope):
    if rope:
        (x_ref, mod_ref, n1_ref, w_ref, b_ref, cq_ref, sq_ref, ck_ref, sk_ref, q_ref, k_ref, v_ref) = refs
    else:
        (x_ref, mod_ref, n1_ref, w_ref, b_ref, q_ref, k_ref, v_ref, kv_ref) = refs
    h = _modulated(x_ref[...], n1_ref[...], mod_ref[0], 0).astype(BF16)
    a = _dot(h, w_ref[...]) + b_ref[...]
    if rope:
        cq, sq, ck, sk = cq_ref[...], sq_ref[...], ck_ref[...], sk_ref[...]
        for hd in range(SWA_HEADS):
            sl = slice(hd * LANES, (hd + 1) * LANES)
            q_ref[:, sl] = _rope_slab(a[:, sl], cq, sq, SWA_HEAD_DIM // 2).astype(BF16)
        for hd in range(SWA_KV_HEADS):
            sl = slice(hd * LANES, (hd + 1) * LANES)
            k_ref[:, sl] = _rope_slab(a[:, SWA_QW + hd * LANES:SWA_QW + (hd + 1) * LANES], ck, sk,
                                      SWA_HEAD_DIM // 2).astype(BF16)
    else:
        q_ref[...] = (a[:, :SWA_QW] * SWA_SCALE).astype(BF16)
        k_ref[...] = a[:, SWA_QW:SWA_QW + SWA_KW].astype(BF16)
        kv_ref[...] = a[:, SWA_QW + 2 * SWA_KW:]
    v_ref[...] = a[:, SWA_QW + SWA_KW:SWA_QW + 2 * SWA_KW].astype(BF16)


def _swa_proj(x, mods, norm1, w_ext, b_ext, tabs, rope):
    base = T_CTX // TM if rope else 0
    n_tiles = (T_LAT if rope else T_CTX) // TM
    n_rows = n_tiles * TM
    ncols = w_ext.shape[1]
    row = lambda i: (i, 0)
    in_specs = [pl.BlockSpec((TM, D_MODEL), lambda i: (i + base, 0)), _mod_spec(TM, base), _full((1, D_MODEL)),
                _full((D_MODEL, ncols)), _full((1, ncols))]
    args = [x, mods, norm1.reshape(1, -1), w_ext, b_ext]
    out_specs = [pl.BlockSpec((TM, SWA_QW), row), pl.BlockSpec((TM, SWA_KW), row), pl.BlockSpec((TM, SWA_KW), row)]
    out_shape = [jax.ShapeDtypeStruct((n_rows, SWA_QW), BF16), jax.ShapeDtypeStruct((n_rows, SWA_KW), BF16),
                 jax.ShapeDtypeStruct((n_rows, SWA_KW), BF16)]
    if rope:
        tab = pl.BlockSpec((TM, LANES), lambda i: (_pos_block(i + base, TM), 0))
        in_specs += [tab] * 4
        args += list(tabs)
    else:
        out_specs.append(pl.BlockSpec((TM, SWA_CW), row))
        out_shape.append(jax.ShapeDtypeStruct((n_rows, SWA_CW), F32))
    return pl.pallas_call(
        functools.partial(_swa_proj_kernel, rope=rope),
        grid=(n_tiles,),
        in_specs=in_specs, out_specs=out_specs, out_shape=out_shape,
        compiler_params=_params("arbitrary"),
        name="swa_proj_lat" if rope else "swa_proj_ctx",
    )(*args)


def _swa_ctx_kernel(q_ref, k_ref, v_ref, sink_ref, wo_ref, bo_ref, x_ref, mod_ref, o_ref, o_s):
    for n in range(SWA_KV_HEADS):
        kn = k_ref[:, n * LANES:(n + 1) * LANES]
        vn = v_ref[:, n * LANES:(n + 1) * LANES]
        for g in range(SWA_GROUP):
            j = n * SWA_GROUP + g
            sl = slice(j * LANES, (j + 1) * LANES)
            s = _dot_nt(q_ref[:, sl], kn)
            sink = sink_ref[j]
            m = jnp.maximum(jnp.max(s, axis=-1, keepdims=True), sink)
            p = jnp.exp(s - m)
            l = jnp.sum(p, axis=-1, keepdims=True) + jnp.exp(sink - m)
            o_s[:, sl] = (_dot(p.astype(BF16), vn) / l).astype(BF16)
    mix = _dot(o_s[...], wo_ref[...]) + bo_ref[...]
    o_ref[...] = x_ref[...] + mod_ref[0][2:3] * mix


def _swa_lat_kernel(q_ref, k_ref, v_ref, ck_ref, cv_ref, sink_ref, wo_ref, bo_ref, x_ref, mod_ref, o_ref, o_s):
    span = TQ + 2 * SWA_WINDOW
    start = pl.program_id(1) * TQ
    ks = pl.multiple_of(jnp.clip(start - SWA_WINDOW, 0, DEC_SEQ - span), SWA_WINDOW)
    qpos = start + lax.broadcasted_iota(jnp.int32, (TQ, span), 0)
    kpos = ks + lax.broadcasted_iota(jnp.int32, (TQ, span), 1)
    valid = jnp.abs(kpos - qpos) <= SWA_WINDOW
    for n in range(SWA_KV_HEADS):
        nl = slice(n * LANES, (n + 1) * LANES)
        kc = ck_ref[0, :, nl]
        vc = cv_ref[0, :, nl]
        kl = k_ref[pl.ds(ks, span), nl]
        vl = v_ref[pl.ds(ks, span), nl]
        for g in range(SWA_GROUP):
            j = n * SWA_GROUP + g
            sl = slice(j * LANES, (j + 1) * LANES)
            qj = q_ref[:, sl]
            s_c = _dot_nt(qj, kc)
            s_l = jnp.where(valid, _dot_nt(qj, kl), -jnp.inf)
            sink = sink_ref[j]
            m = jnp.maximum(jnp.maximum(jnp.max(s_c, axis=-1, keepdims=True),
                                        jnp.max(s_l, axis=-1, keepdims=True)), sink)
            p_c = jnp.exp(s_c - m)
            p_l = jnp.exp(s_l - m)
            l = (jnp.sum(p_c, axis=-1, keepdims=True) + jnp.sum(p_l, axis=-1, keepdims=True)
                 + jnp.exp(sink - m))
            o = _dot(p_c.astype(BF16), vc) + _dot(p_l.astype(BF16), vl)
            o_s[:, sl] = (o / l).astype(BF16)
    mix = _dot(o_s[...], wo_ref[...]) + bo_ref[...]
    o_ref[...] = x_ref[...] + mod_ref[0][2:3] * mix


def _swa_attn(x, mods, q, k, v, sink, wo_ext, bo, cache=None):
    smem = pl.BlockSpec(memory_space=pltpu.SMEM)
    if cache is None:
        nq, base = 1, 0
        tile = lambda b, qi: (b, 0)
        in_specs = [pl.BlockSpec((TQ, SWA_QW), tile), pl.BlockSpec((SEQ, SWA_KW), tile),
                    pl.BlockSpec((SEQ, SWA_KW), tile)]
        args = [q, k, v]
        kern, nb, name = _swa_ctx_kernel, BATCH, "swa_attn_ctx"
    else:
        nq, base = DEC_SEQ // TQ, T_CTX // TQ
        tile = lambda b, qi: (b * nq + qi, 0)
        seq = lambda b, qi: (b, 0)
        cspec = pl.BlockSpec((1, PAST_LEN, SWA_KW), lambda b, qi: (b, 0, 0))
        in_specs = [pl.BlockSpec((TQ, SWA_QW), tile), pl.BlockSpec((DEC_SEQ, SWA_KW), seq),
                    pl.BlockSpec((DEC_SEQ, SWA_KW), seq), cspec, cspec]
        args = [q, k, v, cache[0], cache[1]]
        kern, nb, name = _swa_lat_kernel, DEC_BATCH, "swa_attn_lat"
    xtile = lambda b, qi: (base + b * nq + qi, 0)
    in_specs += [smem, _full((SWA_QW, D_MODEL)), _full((1, D_MODEL)), pl.BlockSpec((TQ, D_MODEL), xtile),
                 pl.BlockSpec((1, 6, D_MODEL), lambda b, qi: (_cond_of_row((base + b * nq + qi) * TQ), 0, 0))]
    args += [sink, wo_ext, bo.reshape(1, -1), x, mods]
    return pl.pallas_call(
        kern,
        grid=(nb, nq),
        in_specs=in_specs,
        out_specs=pl.BlockSpec((TQ, D_MODEL), xtile),
        out_shape=jax.ShapeDtypeStruct((T_ALL, D_MODEL), F32),
        scratch_shapes=[pltpu.VMEM((TQ, SWA_QW), BF16)],
        input_output_aliases={len(args) - 2: 0},
        compiler_params=_params("arbitrary", "arbitrary"),
        name=name,
    )(*args)


CONV_EXT = CONV_TM + 2 * CONV_HALO


def _conv_kernel(xp_ref, x_ref, xn_ref, mod_ref, n1_ref, w1_ref, b1_ref, dw_ref, dwb_ref, lng_ref, lnb_ref,
                 w2_ref, b2_ref, o_ref, ext_s, y_s):
    i = pl.program_id(0)
    tiles_per_seq = DEC_SEQ // CONV_TM
    j = (i - T_CTX // CONV_TM) % tiles_per_seq
    latent = i >= T_CTX // CONV_TM
    left_ok = jnp.logical_and(latent, j > 0)
    right_ok = jnp.logical_and(latent, j < tiles_per_seq - 1)
    mod = mod_ref[0]

    def glu_of(xv):
        h = _modulated(xv, n1_ref[...], mod, 0).astype(BF16)
        u = _dot(h, w1_ref[...]) + b1_ref[...]
        return u[:, :D_MODEL] * _sigmoid(u[:, D_MODEL:])

    ext_s[CONV_HALO:CONV_HALO + CONV_TM, :] = glu_of(x_ref[...])
    ext_s[0:CONV_HALO, :] = jnp.where(left_ok, glu_of(xp_ref[...]), 0.0)
    ext_s[CONV_HALO + CONV_TM:, :] = jnp.where(right_ok, glu_of(xn_ref[...]), 0.0)

    rows = 128
    sub = 8
    first = CONV_HALO - CONV_PAD
    span = rows + (first + CONV_WIDTH - 1) // sub * sub
    for c in range(D_MODEL // LANES):
        cl = slice(c * LANES, (c + 1) * LANES)
        dwc = dw_ref[:, cl]
        for r in range(CONV_TM // rows):
            window = ext_s[r * rows:r * rows + span + sub, cl]
            acc = jnp.zeros((rows, LANES), F32)
            for res in range(sub):
                shifted = window if res == 0 else pltpu.roll(window, span + sub - res, 0)
                for q in range(span // sub - rows // sub + 1):
                    w = q * sub + res - first
                    if 0 <= w < CONV_WIDTH:
                        acc = acc + shifted[q * sub:q * sub + rows] * dwc[w:w + 1]
            y_s[r * rows:(r + 1) * rows, cl] = acc
    y = y_s[...] + dwb_ref[...]
    mu = jnp.mean(y, axis=-1, keepdims=True)
    yc = y - mu
    var = jnp.mean(yc * yc, axis=-1, keepdims=True)
    yn = yc * lax.rsqrt(var + NORM_EPS) * lng_ref[...] + lnb_ref[...]
    z = (yn * _sigmoid(yn)).astype(BF16)
    mix = _dot(z, w2_ref[...]) + b2_ref[...]
    o_ref[...] = x_ref[...] + mod[2:3] * mix


def _conv_layer(x, mods, norm1, w1, b1, dw, dwb, ln_g, ln_b, w2, b2):
    n_tiles = T_ALL // CONV_TM
    per = CONV_TM // CONV_HALO
    n_halo_blocks = T_ALL // CONV_HALO
    row = lambda i: (i, 0)
    vec = lambda v: v.reshape(1, -1)
    return pl.pallas_call(
        _conv_kernel,
        grid=(n_tiles,),
        in_specs=[pl.BlockSpec((CONV_HALO, D_MODEL), lambda i: (jnp.maximum(i * per - 1, 0), 0)),
                  pl.BlockSpec((CONV_TM, D_MODEL), row),
                  pl.BlockSpec((CONV_HALO, D_MODEL), lambda i: (jnp.minimum((i + 1) * per, n_halo_blocks - 1), 0)),
                  _mod_spec(CONV_TM), _full((1, D_MODEL)),
                  _full((D_MODEL, 2 * D_MODEL)), _full((1, 2 * D_MODEL)),
                  _full((CONV_WIDTH, D_MODEL)), _full((1, D_MODEL)), _full((1, D_MODEL)), _full((1, D_MODEL)),
                  _full((D_MODEL, D_MODEL)), _full((1, D_MODEL))],
        out_specs=pl.BlockSpec((CONV_TM, D_MODEL), row),
        out_shape=jax.ShapeDtypeStruct((T_ALL, D_MODEL), F32),
        scratch_shapes=[pltpu.VMEM((CONV_EXT, D_MODEL), F32), pltpu.VMEM((CONV_TM, D_MODEL), F32)],
        compiler_params=_params("arbitrary"),
        name="conv_module",
    )(x, x, x, mods, vec(norm1), w1.astype(BF16), vec(b1), dw, vec(dwb), vec(ln_g), vec(ln_b),
      w2.astype(BF16), vec(b2))


def _router_kernel(*refs, n_x):
    mod_ref, n2_ref, rw_ref, rb_ref, h_ref, idx_ref, gate_ref, cnt_ref = refs[n_x:]
    h = _modulated(_x_tile(refs[:n_x], TM), n2_ref[...], mod_ref[0], 3)
    _store_token_tiles(h_ref, h)
    cur = _dot_nt(rw_ref[...], h.astype(BF16)) + rb_ref[...]
    e_iota = lax.broadcasted_iota(jnp.int32, cur.shape, 0)
    vals, idxs = [], []
    for _ in range(TOP_K):
        m = jnp.max(cur, axis=0, keepdims=True)
        idx = jnp.min(jnp.where(cur == m, e_iota, N_EXPERTS), axis=0, keepdims=True)
        vals.append(m)
        idxs.append(idx)
        cur = jnp.where(e_iota == idx, -jnp.inf, cur)
    ex = [jnp.exp(v - vals[0]) for v in vals]
    tot = ex[0] + ex[1] + ex[2] + ex[3]
    for k in range(TOP_K):
        idx_ref[k:k + 1, :] = idxs[k]
        gate_ref[k:k + 1, :] = ex[k] / tot

    picked = jnp.sum(jnp.where(cur == -jnp.inf, 1.0, 0.0), axis=1, keepdims=True)

    @pl.when(pl.program_id(0) == 0)
    def _():
        cnt_ref[...] = jnp.zeros_like(cnt_ref)

    cnt_ref[...] += picked


def _router(x, mods, norm2, router_w, router_b):
    n_tiles = T_ALL // TM
    row = lambda i: (i, 0)
    col = lambda i: (0, i)
    x_specs, x_args = _x_specs(x, TM)
    return pl.pallas_call(
        functools.partial(_router_kernel, n_x=len(x_args)),
        grid=(n_tiles,),
        in_specs=x_specs + [_mod_spec(TM), _full((1, D_MODEL)),
                            _full((N_EXPERTS, D_MODEL)), _full((N_EXPERTS, 1))],
        out_specs=[pl.BlockSpec((TM * TOK_ROWS, LANES), row), pl.BlockSpec((TOP_K, TM), col),
                   pl.BlockSpec((TOP_K, TM), col), pl.BlockSpec((N_EXPERTS, 1), lambda i: (0, 0))],
        out_shape=[jax.ShapeDtypeStruct((T_ALL * TOK_ROWS, LANES), F32),
                   jax.ShapeDtypeStruct((TOP_K, T_ALL), jnp.int32),
                   jax.ShapeDtypeStruct((TOP_K, T_ALL), F32),
                   jax.ShapeDtypeStruct((N_EXPERTS, 1), F32)],
        compiler_params=_params("arbitrary"),
        name="router",
    )(*x_args, mods, norm2.reshape(1, -1), router_w.T.astype(BF16), router_b.reshape(-1, 1))


N_ASSIGN = TOP_K * T_ALL
RANK_CHUNK = 256
RANK_BLOCK = 2048


def _block_layout(counts):
    counts = counts.reshape(N_EXPERTS).astype(jnp.int32)
    padded = (counts + MOE_BM - 1) // MOE_BM * MOE_BM
    pad_end = jnp.cumsum(padded)
    pad_start = pad_end - padded
    starts = jnp.arange(MOE_BLOCKS + 1, dtype=jnp.int32) * MOE_BM
    e_ids = jnp.arange(N_EXPERTS, dtype=jnp.int32)
    used = padded > 0
    last_used = jnp.max(jnp.where(used, e_ids, 0))
    block_e = jnp.minimum(jnp.sum((pad_end[None, :] <= starts[:, None]).astype(jnp.int32), axis=1), last_used)
    n_used = (pad_end[-1] // MOE_BM).reshape(1)
    run = jnp.cumsum(used.astype(jnp.int32)) - 1
    later_used = jnp.where(jnp.logical_and(used[None, :], e_ids[None, :] > e_ids[:, None]), e_ids[None, :], N_EXPERTS)
    next_used = jnp.min(later_used, axis=1)
    next_used = jnp.where(next_used == N_EXPERTS, e_ids, next_used)
    onehot = (block_e[:, None] == e_ids[None, :]).astype(jnp.int32)
    block_next = jnp.sum(onehot * next_used[None, :], axis=1)
    block_par = jnp.sum(onehot * run[None, :], axis=1) & 1
    block_plan = (block_e, block_next, block_par, n_used)
    return pad_start.astype(F32).reshape(N_EXPERTS, 1), block_plan


def _rank_kernel(idx_ref, ps_ref, u_ref, dest_ref, carry):
    @pl.when(pl.program_id(0) == 0)
    def _():
        carry[...] = ps_ref[...]

    e_iota = lax.broadcasted_iota(jnp.int32, (N_EXPERTS, RANK_CHUNK), 0)
    for c in range(RANK_BLOCK // RANK_CHUNK):
        sl = slice(c * RANK_CHUNK, (c + 1) * RANK_CHUNK)
        hit = e_iota == idx_ref[:, sl]
        pref = _dot(jnp.where(hit, 1.0, 0.0).astype(BF16), u_ref[...])
        base = carry[...]
        dest = jnp.sum(jnp.where(hit, pref + base, 0.0), axis=0, keepdims=True) - 1.0
        dest_ref[:, sl] = dest.astype(jnp.int32)
        carry[...] = base + pref[:, RANK_CHUNK - 1:RANK_CHUNK]


def _assignment_slots(idx_t, pad_start):
    tri = jnp.asarray(np.triu(np.ones((RANK_CHUNK, RANK_CHUNK), np.float32)), dtype=BF16)
    blk = pl.BlockSpec((1, RANK_BLOCK), lambda i: (0, i))
    return pl.pallas_call(
        _rank_kernel,
        grid=(N_ASSIGN // RANK_BLOCK,),
        in_specs=[blk, _full((N_EXPERTS, 1)), _full((RANK_CHUNK, RANK_CHUNK))],
        out_specs=blk,
        out_shape=jax.ShapeDtypeStruct((1, N_ASSIGN), jnp.int32),
        scratch_shapes=[pltpu.VMEM((N_EXPERTS, 1), F32)],
        compiler_params=_params("arbitrary"),
        name="moe_rank",
    )(idx_t.reshape(1, N_ASSIGN), pad_start, tri)


INV_UNROLL = 32
INV_CHUNK = 4096
INV_STEPS = N_ASSIGN // INV_CHUNK
N_SLOTS = MOE_CAP + MOE_BM


def _inverse_kernel(dest_ref, pad_hbm, slot_ref, sem):
    i = pl.program_id(0)

    @pl.when(i == 0)
    def _():
        fill = pltpu.make_async_copy(pad_hbm, slot_ref, sem)
        fill.start()
        fill.wait()

    base = i * INV_CHUNK

    def place(j, carry):
        for u in range(INV_UNROLL):
            o = j * INV_UNROLL + u
            slot_ref[dest_ref[0, 0, o]] = base + o
        return carry

    lax.fori_loop(0, INV_CHUNK // INV_UNROLL, place, 0)


def _slot_assignments(dest):
    s = np.arange(N_SLOTS, dtype=np.int32)
    pad_ids = jnp.asarray(N_ASSIGN + (s & (MOE_BM - 1)) + np.where(s >= MOE_CAP, MOE_BM, 0).astype(np.int32))
    return pl.pallas_call(
        _inverse_kernel,
        grid=(INV_STEPS,),
        in_specs=[pl.BlockSpec((1, 1, INV_CHUNK), lambda i: (i, 0, 0), memory_space=pltpu.SMEM),
                  pl.BlockSpec(memory_space=pl.ANY)],
        out_specs=pl.BlockSpec(memory_space=pltpu.SMEM),
        out_shape=jax.ShapeDtypeStruct((N_SLOTS,), jnp.int32),
        scratch_shapes=[pltpu.SemaphoreType.DMA(())],
        compiler_params=_params("arbitrary"),
        name="moe_slots",
    )(dest.reshape(INV_STEPS, 1, INV_CHUNK), pad_ids)


YG_TOKENS = N_ASSIGN + 2 * MOE_BM
BLOCK_TILE_ROWS = MOE_BM * TOK_ROWS
MOE_CHUNK = 256
ACT_BUF = 3


def _moe_kernel(be_ref, ne_ref, par_ref, nu_ref, sa_prev_ref, sa_ref, sa_next_ref, sa_next2_ref, h_hbm,
                wg_hbm, wu_hbm, wd_hbm, bg_ref, bu_ref, bd_ref, yg_hbm,
                xbuf, ybuf, x_s, wg_f, wu_f, wd_f, wgu_s, wd_s, gsem, ssem, wsem):
    i = pl.program_id(0)
    n_used = nu_ref[0]
    cur = i % 2
    nxt = 1 - cur
    gcur = lax.rem(i, 3)
    gnext = lax.rem(i + 1, 3)
    gnext2 = lax.rem(i + 2, 3)

    def weight_copies(e, slot):
        return [pltpu.make_async_copy(src.at[e], dst.at[slot], wsem.at[slot])
                for src, dst in ((wg_hbm, wg_f), (wu_hbm, wu_f), (wd_hbm, wd_f))]

    def start_gather(sa, buf, lo=0, hi=MOE_BM):
        for r in range(lo, hi):
            tok = sa[0, 0, r] & (T_ALL - 1)
            pltpu.make_async_copy(h_hbm.at[pl.ds(tok * TOK_ROWS, TOK_ROWS)],
                                  xbuf.at[buf, pl.ds(r * TOK_ROWS, TOK_ROWS)], gsem.at[buf]).start()

    def start_scatter(sa, buf):
        for r in range(MOE_BM):
            pltpu.async_copy(ybuf.at[buf, pl.ds(r * TOK_ROWS, TOK_ROWS)],
                             yg_hbm.at[pl.ds(sa[0, 0, r] * TOK_ROWS, TOK_ROWS)], ssem.at[buf], priority=1)

    def wait_gather(buf):
        pltpu.make_async_copy(h_hbm.at[pl.ds(0, BLOCK_TILE_ROWS)], xbuf.at[buf], gsem.at[buf]).wait()

    def wait_scatter(buf):
        pltpu.make_async_copy(ybuf.at[buf], yg_hbm.at[pl.ds(0, BLOCK_TILE_ROWS)], ssem.at[buf]).wait()

    @pl.when(i == 0)
    def _():
        for cp in weight_copies(be_ref[0], 0):
            cp.start()
        start_gather(sa_ref, 0)
        start_gather(sa_next_ref, 1)
        ybuf[...] = jnp.zeros_like(ybuf)
        pltpu.make_async_copy(ybuf.at[0], yg_hbm.at[pl.ds(N_ASSIGN * TOK_ROWS, BLOCK_TILE_ROWS)], ssem.at[0]).start()

    @pl.when(i <= n_used)
    def _():
        e = be_ref[i]

        @pl.when(jnp.logical_or(i == 0, e != be_ref[jnp.maximum(i - 1, 0)]))
        def _():
            slot = par_ref[i]
            for cp in weight_copies(e, slot):
                cp.wait()
            for c in range(D_MODEL // MOE_CHUNK):
                cs = slice(c * MOE_CHUNK, (c + 1) * MOE_CHUNK)
                wgu_s[:, 2 * c * MOE_CHUNK:(2 * c + 1) * MOE_CHUNK] = wg_f[slot, :, cs].astype(BF16)
                wgu_s[:, (2 * c + 1) * MOE_CHUNK:(2 * c + 2) * MOE_CHUNK] = wu_f[slot, :, cs].astype(BF16)
            wd_s[...] = wd_f[slot].astype(BF16)

            @pl.when(ne_ref[i] != e)
            def _():
                for cp in weight_copies(ne_ref[i], 1 - slot):
                    cp.start()

        wait_gather(gcur)
        x_s[...] = _load_token_tiles(xbuf.at[gcur], MOE_BM).astype(BF16)
        n_chunks = D_MODEL // MOE_CHUNK
        slabs = MOE_CHUNK // LANES
        for c in range(n_chunks):
            cs = slice(c * MOE_CHUNK, (c + 1) * MOE_CHUNK)
            start_gather(sa_next2_ref, gnext2, c * MOE_BM // n_chunks, (c + 1) * MOE_BM // n_chunks)
            gu = _dot(x_s[...], wgu_s[:, 2 * c * MOE_CHUNK:(2 * c + 2) * MOE_CHUNK])
            g = jnp.minimum(gu[:, :MOE_CHUNK] + bg_ref[0, :, cs], SWIGLU_LIMIT)
            u = jnp.clip(gu[:, MOE_CHUNK:] + bu_ref[0, :, cs], -SWIGLU_LIMIT, SWIGLU_LIMIT)
            act = (u + 1.0) * (g * _sigmoid(SWIGLU_ALPHA * g))
            for j in range(slabs):
                jj = c * slabs + j
                xbuf[ACT_BUF, jj * MOE_BM:(jj + 1) * MOE_BM, :] = act[:, j * LANES:(j + 1) * LANES]
        wait_scatter(cur)
        start_scatter(sa_prev_ref, nxt)
        act_b =jnp.concatenate([xbuf[ACT_BUF, j * MOE_BM:(j + 1) * MOE_BM, :] for j in range(TOK_ROWS)],
                                axis=-1).astype(BF16)
        half = D_MODEL // 2
        for c in range(2):
            y = _dot(act_b, wd_s[:, c * half:(c + 1) * half]) + bd_ref[0, :, c * half:(c + 1) * half]
            for j in range(TOK_ROWS // 2):
                jj = c * (TOK_ROWS // 2) + j
                ybuf.at[cur][pl.ds(jj, MOE_BM, stride=TOK_ROWS), :] = y[:, j * LANES:(j + 1) * LANES]

    @pl.when(i == n_used)
    def _():
        wait_gather(gnext)
        wait_gather(gnext2)
        wait_scatter(nxt)


def _moe_experts(h, slots, block_plan, wg, bg, wu, bu, wd, bd):
    bspec = pl.BlockSpec((1, 1, D_MODEL), lambda i, be, *_: (be[i], 0, 0))
    sspec = lambda f: pl.BlockSpec((1, 1, MOE_BM), lambda i, *_: (f(i), 0, 0), memory_space=pltpu.SMEM)
    hbm = pl.BlockSpec(memory_space=pl.ANY)
    b3 = lambda b: b.reshape(N_EXPERTS, 1, D_MODEL)
    slots3 = slots.reshape(MOE_BLOCKS + 1, 1, MOE_BM)
    wbuf = pltpu.VMEM((2, D_MODEL, D_MODEL), F32)
    return pl.pallas_call(
        _moe_kernel,
        grid_spec=pltpu.PrefetchScalarGridSpec(
            num_scalar_prefetch=4,
            grid=(MOE_BLOCKS + 1,),
            in_specs=[sspec(lambda i: jnp.where(i == 0, MOE_BLOCKS, i - 1)), sspec(lambda i: i),
                      sspec(lambda i: jnp.minimum(i + 1, MOE_BLOCKS)), sspec(lambda i: jnp.minimum(i + 2, MOE_BLOCKS)),
                      hbm, hbm, hbm, hbm, bspec, bspec, bspec],
            out_specs=hbm,
            scratch_shapes=[pltpu.VMEM((ACT_BUF + 1, BLOCK_TILE_ROWS, LANES), F32),
                            pltpu.VMEM((2, BLOCK_TILE_ROWS, LANES), F32),
                            pltpu.VMEM((MOE_BM, D_MODEL), BF16),
                            wbuf, wbuf, wbuf,
                            pltpu.VMEM((D_MODEL, 2 * D_MODEL), BF16), pltpu.VMEM((D_MODEL, D_MODEL), BF16),
                            pltpu.SemaphoreType.DMA((3,)), pltpu.SemaphoreType.DMA((2,)),
                            pltpu.SemaphoreType.DMA((2,))]),
        out_shape=jax.ShapeDtypeStruct((YG_TOKENS * TOK_ROWS, LANES), F32),
        compiler_params=_params("arbitrary"),
        name="moe_experts",
    )(*block_plan, slots3, slots3, slots3, slots3, h, wg, wu, wd, b3(bg), b3(bu), b3(bd))


def _combine_kernel(*refs, final, n_x):
    x_refs = refs[:n_x]
    if final:
        y0_ref, y1_ref, y2_ref, y3_ref, gate_ref, mod_ref, fg_ref, o_ref = refs[n_x:]
    else:
        y0_ref, y1_ref, y2_ref, y3_ref, gate_ref, mod_ref, o_ref = refs[n_x:]
    gate = gate_ref[...]
    n = o_ref.shape[0]
    g2 = mod_ref[0][5:6]
    sumsq = jnp.zeros((n, 1), F32)
    for j in range(TOK_ROWS):
        rows = pl.ds(j, n, stride=TOK_ROWS)
        sl = slice(j * LANES, (j + 1) * LANES)
        acc = y0_ref[rows, :] * gate[:, 0:1]
        for k, y_ref in enumerate((y1_ref, y2_ref, y3_ref), start=1):
            acc = acc + y_ref[rows, :] * gate[:, k:k + 1]
        xn = _x_tile(x_refs, TM, sl) + g2[:, sl] * acc
        o_ref[:, sl] = xn
        if final:
            sumsq = sumsq + jnp.sum(xn * xn, axis=-1, keepdims=True)
    if final:
        inv = lax.rsqrt(sumsq * (1.0 / D_MODEL) + NORM_EPS)
        o_ref[...] = o_ref[...] * inv * fg_ref[...]


def _combine(x, yg, gates, mods, final_g=None, base_rows=0, n_rows=T_ALL):
    final = final_g is not None
    base = base_rows // TM
    n_all = T_ALL // TM
    row = lambda i: (i + base, 0)
    ysp = lambda k: pl.BlockSpec((TM * TOK_ROWS, LANES), lambda i: (k * n_all + i + base, 0))
    if isinstance(x, tuple):
        x_specs, x_args = _x_specs(x, TM)
    else:
        x_specs, x_args = [pl.BlockSpec((TM, D_MODEL), row)], [x]
    in_specs = x_specs + [ysp(0), ysp(1), ysp(2), ysp(3), pl.BlockSpec((TM, TOP_K), row), _mod_spec(TM, base)]
    args = x_args + [yg, yg, yg, yg, gates, mods]
    if final:
        in_specs.append(_full((1, D_MODEL)))
        args.append(final_g.reshape(1, -1))
    return pl.pallas_call(
        functools.partial(_combine_kernel, final=final, n_x=len(x_args)),
        grid=(n_rows // TM,),
        in_specs=in_specs,
        out_specs=pl.BlockSpec((TM, D_MODEL), (lambda i: (i, 0)) if final else row),
        out_shape=jax.ShapeDtypeStruct((n_rows, D_MODEL), F32),
        input_output_aliases={} if final or len(x_args) == 2 else {0: 0},
        compiler_params=_params("arbitrary"),
        name="moe_combine_final" if final else "moe_combine",
    )(*args)


def _moe_layer(x, mods, norm2, router_w, router_b, wg, bg, wu, bu, wd, bd, final_g=None):
    h, idx_t, gate_t, counts = _router(x, mods, norm2, router_w, router_b)
    pad_start, block_plan = _block_layout(counts)
    slots = _slot_assignments(_assignment_slots(idx_t, pad_start))
    yg = _moe_experts(h, slots, block_plan, wg, bg, wu, bu, wd, bd)
    gates = gate_t.T
    if final_g is None:
        return _combine(x, yg, gates, mods)
    return (_combine(x, yg, gates, mods, final_g, 0, T_CTX),
            _combine(x, yg, gates, mods, final_g, T_CTX, T_LAT))


def _pad_lanes(w, lead_pad, width):
    tail = width - lead_pad - w.shape[-1]
    cfg = [(0, 0)] * (w.ndim - 1) + [(lead_pad, tail)]
    return jnp.pad(w, cfg)


def _mla_weights(wa, wuq, wukv, wo):
    krope_w = wa[:, MLA_Q_RANK + MLA_KV_RANK:]
    wa_ext = jnp.concatenate([wa[:, :MLA_Q_RANK + MLA_KV_RANK],
                              _pad_lanes(krope_w, MLA_ROPE_OFF, LANES),
                              _pad_lanes(krope_w, 0, LANES)], axis=1).astype(BF16)
    q3 = wuq.reshape(MLA_Q_RANK, MLA_HEADS, MLA_NOPE + MLA_ROPE)
    wuq_ext = jnp.concatenate([q3[..., :MLA_NOPE], _pad_lanes(q3[..., MLA_NOPE:], MLA_ROPE // 2, LANES - MLA_NOPE)],
                              axis=-1).reshape(MLA_Q_RANK, MLA_HEADS * LANES).astype(BF16)
    kv3 = wukv.reshape(MLA_KV_RANK, MLA_HEADS, MLA_NOPE + MLA_V)
    wk_ext = _pad_lanes(kv3[..., :MLA_NOPE], 0, LANES).reshape(MLA_KV_RANK, MLA_HEADS * LANES).astype(BF16)
    wv = kv3[..., MLA_NOPE:].reshape(MLA_KV_RANK, MLA_HEADS * MLA_V).astype(BF16)
    return wa_ext, wuq_ext, wk_ext, wv, wo.astype(BF16)


def _swa_weights(wqkv, bqkv, wo):
    nq = SWA_HEADS * SWA_HEAD_DIM
    nkv = SWA_KV_HEADS * SWA_HEAD_DIM

    def slabs(w, heads):
        lead = w.shape[:-1]
        return _pad_lanes(w.reshape(lead + (heads, SWA_HEAD_DIM)), 0, LANES).reshape(lead + (heads * LANES,))

    def ext(w, with_compact):
        parts = [slabs(w[..., :nq], SWA_HEADS), slabs(w[..., nq:nq + nkv], SWA_KV_HEADS),
                 slabs(w[..., nq + nkv:], SWA_KV_HEADS)]
        if with_compact:
            parts.append(w[..., nq:])
        return jnp.concatenate(parts, axis=-1)

    b2 = bqkv.reshape(1, -1)
    wo3 = wo.reshape(SWA_HEADS, SWA_HEAD_DIM, D_MODEL)
    wo_ext = jnp.pad(wo3, ((0, 0), (0, LANES - SWA_HEAD_DIM), (0, 0))).reshape(SWA_QW, D_MODEL).astype(BF16)
    return (ext(wqkv, True).astype(BF16), ext(b2, True), ext(wqkv, False).astype(BF16), ext(b2, False), wo_ext)


def kernel(x_prompt, x_sample, cache_l0_ckv, cache_l0_krope, cache_l1_k, cache_l1_v, cache_l3_ckv, cache_l3_krope, c, c_ctx, l0_mla_wa, l0_mla_q_norm, l0_mla_wuq, l0_mla_kv_norm, l0_mla_wukv, l0_mla_wo, l0_mod_w, l0_mod_b, l0_norm1, l0_norm2, l0_router_w, l0_router_b, l0_moe_wg, l0_moe_bg, l0_moe_wu, l0_moe_bu, l0_moe_wd, l0_moe_bd, l1_swa_wqkv, l1_swa_bqkv, l1_swa_sink, l1_swa_wo, l1_swa_bo, l1_mod_w, l1_mod_b, l1_norm1, l1_norm2, l1_router_w, l1_router_b, l1_moe_wg, l1_moe_bg, l1_moe_wu, l1_moe_bu, l1_moe_wd, l1_moe_bd, l2_conv_w1, l2_conv_b1, l2_conv_dw, l2_conv_dwb, l2_conv_ln_g, l2_conv_ln_b, l2_conv_w2, l2_conv_b2, l2_mod_w, l2_mod_b, l2_norm1, l2_norm2, l2_router_w, l2_router_b, l2_moe_wg, l2_moe_bg, l2_moe_wu, l2_moe_bu, l2_moe_wd, l2_moe_bd, l3_mla_wa, l3_mla_q_norm, l3_mla_wuq, l3_mla_kv_norm, l3_mla_wukv, l3_mla_wo, l3_mod_w, l3_mod_b, l3_norm1, l3_norm2, l3_router_w, l3_router_b, l3_moe_wg, l3_moe_bg, l3_moe_wu, l3_moe_bu, l3_moe_wd, l3_moe_bd, final_norm):
    x = (x_prompt.reshape(T_CTX, D_MODEL), x_sample.reshape(T_LAT, D_MODEL))
    cond =jnp.concatenate([c, c_ctx[None, :], jnp.zeros((N_COND - DEC_BATCH - 1, D_MODEL), F32)], axis=0)

    mla_tabs = (_rope_tables(MLA_ROPE, MLA_ROPE_LEAD, TM, MLA_SCALE) + _rope_tables(MLA_ROPE, MLA_ROPE_LEAD, TM))
    swa_tabs = _rope_tables(SWA_HEAD_DIM, 0, TM, SWA_SCALE) + _rope_tables(SWA_HEAD_DIM, 0, TM)

    def mla_layer(x, mods, norm1, wa, q_norm, wuq, kv_norm, wukv, wo, cache_ckv, cache_krope):
        wa_ext, wuq_ext, wk_ext, wv_ext, wo_ext = _mla_weights(wa, wuq, wukv, wo)
        q, ckv, kp, kst = _mla_proj(x, mods, norm1, wa_ext, q_norm, wuq_ext, kv_norm, mla_tabs)
        cache = (cache_ckv, _pad_lanes(cache_krope, MLA_ROPE_OFF, LANES).astype(BF16))
        if isinstance(x, tuple):
            x = (_mla_attn(x[0], mods, q, ckv, kp, wk_ext, wv_ext, wo_ext, own_rows=True),
                 _mla_attn(x[1], mods, q, ckv, kp, wk_ext, wv_ext, wo_ext, cache, own_rows=True))
        else:
            x = _mla_attn(x, mods, q, ckv, kp, wk_ext, wv_ext, wo_ext)
            x = _mla_attn(x, mods, q, ckv, kp, wk_ext, wv_ext, wo_ext, cache)
        return x, (ckv[:T_CTX].reshape(BATCH, SEQ, MLA_KV_RANK), kst[:T_CTX].reshape(BATCH, SEQ, MLA_ROPE))

    def moe(x, mods, norm2, rw, rb, wg, bg, wu, bu, wd, bd):
        return _moe_layer(x, mods, norm2, rw, rb, wg, bg, wu, bu, wd, bd)

    mods = _modulation(cond, l0_mod_w, l0_mod_b)
    x, (st0_ckv, st0_krope) = mla_layer(x, mods, l0_norm1, l0_mla_wa, l0_mla_q_norm, l0_mla_wuq, l0_mla_kv_norm,
                                        l0_mla_wukv, l0_mla_wo, cache_l0_ckv, cache_l0_krope)
    x = moe(x, mods, l0_norm2, l0_router_w, l0_router_b, l0_moe_wg, l0_moe_bg, l0_moe_wu, l0_moe_bu,
            l0_moe_wd, l0_moe_bd)

    mods = _modulation(cond, l1_mod_w, l1_mod_b)
    w_ctx, b_ctx, w_lat, b_lat, swa_wo_ext = _swa_weights(l1_swa_wqkv, l1_swa_bqkv, l1_swa_wo)
    q_c, k_c, v_c, kv_c = _swa_proj(x, mods, l1_norm1, w_ctx, b_ctx, None, rope=False)
    q_l, k_l, v_l = _swa_proj(x, mods, l1_norm1, w_lat, b_lat, swa_tabs, rope=True)
    x = _swa_attn(x, mods, q_c, k_c, v_c, l1_swa_sink, swa_wo_ext, l1_swa_bo)
    nkv = SWA_KV_HEADS * SWA_HEAD_DIM
    cache_k = _pad_lanes(cache_l1_k, 0, LANES).reshape(DEC_BATCH, PAST_LEN, SWA_KW).astype(BF16)
    cache_v = _pad_lanes(cache_l1_v, 0, LANES).reshape(DEC_BATCH, PAST_LEN, SWA_KW).astype(BF16)
    x = _swa_attn(x, mods, q_l, k_l, v_l, l1_swa_sink, swa_wo_ext, l1_swa_bo, (cache_k, cache_v))
    st1_k = kv_c[:, :nkv].reshape(BATCH, SEQ, SWA_KV_HEADS, SWA_HEAD_DIM)
    st1_v = kv_c[:, nkv:].reshape(BATCH, SEQ, SWA_KV_HEADS, SWA_HEAD_DIM)
    x = moe(x, mods, l1_norm2, l1_router_w, l1_router_b, l1_moe_wg, l1_moe_bg, l1_moe_wu, l1_moe_bu,
            l1_moe_wd, l1_moe_bd)

    mods = _modulation(cond, l2_mod_w, l2_mod_b)
    x = _conv_layer(x, mods, l2_norm1, l2_conv_w1, l2_conv_b1, l2_conv_dw, l2_conv_dwb, l2_conv_ln_g,
                    l2_conv_ln_b, l2_conv_w2, l2_conv_b2)
    x = moe(x, mods, l2_norm2, l2_router_w, l2_router_b, l2_moe_wg, l2_moe_bg, l2_moe_wu, l2_moe_bu,
            l2_moe_wd, l2_moe_bd)

    mods = _modulation(cond, l3_mod_w, l3_mod_b)
    x, (st3_ckv, st3_krope) = mla_layer(x, mods, l3_norm1, l3_mla_wa, l3_mla_q_norm, l3_mla_wuq, l3_mla_kv_norm,
                                        l3_mla_wukv, l3_mla_wo, cache_l3_ckv, cache_l3_krope)
    y_ctx, y_lat = _moe_layer(x, mods, l3_norm2, l3_router_w, l3_router_b, l3_moe_wg, l3_moe_bg, l3_moe_wu,
                              l3_moe_bu, l3_moe_wd, l3_moe_bd, final_g=final_norm)
    y_prompt = y_ctx.reshape(BATCH, SEQ, D_MODEL)
    y_sample = y_lat.reshape(DEC_BATCH, DEC_SEQ, D_MODEL)
    return (y_prompt, y_sample, st0_ckv, st0_krope, st1_k, st1_v, st3_ckv, st3_krope)
```

```python
import functools

import jax
import jax.numpy as jnp
import numpy as np
from jax import lax
from jax.experimental import pallas as pl
from jax.experimental.pallas import tpu as pltpu

F32 = jnp.float32
BF16 = jnp.bfloat16

D_MODEL = 1024
BATCH = 32
SEQ = 256
DEPTH = 4
DEC_BATCH = 4
DEC_SEQ = 2048
PAST_LEN = 256
GRID_W = 64
NORM_EPS = 1e-6
ROPE_THETA = 10000.0

MLA_HEADS = 16
MLA_NOPE = 64
MLA_ROPE = 32
MLA_V = 64
MLA_Q_RANK = 384
MLA_KV_RANK = 256
MLA_SCALE = (MLA_NOPE + MLA_ROPE) ** -0.5

SWA_HEADS = 16
SWA_KV_HEADS = 4
SWA_GROUP = 4
SWA_HEAD_DIM = 64
SWA_WINDOW = 128
SWA_SCALE = SWA_HEAD_DIM ** -0.5

CONV_WIDTH = 31
CONV_PAD = CONV_WIDTH // 2

N_EXPERTS = 32
TOP_K = 4
SWIGLU_ALPHA = 1.702
SWIGLU_LIMIT = 7.0

LANES = 128
T_CTX = BATCH * SEQ
T_LAT = DEC_BATCH * DEC_SEQ
T_ALL = T_CTX + T_LAT
CTX_COND = DEC_BATCH
N_COND = 8

TM = 512
TQ = 256
MLA_LAT_TQ = 256
CONV_TM = 256
CONV_HALO = 16
MOE_BM = 256
MOE_BLOCKS = (T_ALL * TOP_K + N_EXPERTS * (MOE_BM - 1) + MOE_BM - 1) // MOE_BM
MOE_CAP = MOE_BLOCKS * MOE_BM
VMEM_LIMIT = 56 * 1024 * 1024


def _params(*sem):
    return pltpu.CompilerParams(dimension_semantics=sem, vmem_limit_bytes=VMEM_LIMIT)


def _cond_of_row(row0):
    return jnp.where(row0 < T_CTX, CTX_COND, (row0 - T_CTX) // DEC_SEQ)


def _pos_block(i, tm):
    row0 = i * tm
    return jnp.where(row0 < T_CTX, DEC_SEQ // tm, ((row0 - T_CTX) % DEC_SEQ) // tm)


def _rms(x, g):
    ms = jnp.mean(x * x, axis=-1, keepdims=True)
    return x * lax.rsqrt(ms + NORM_EPS) * g


def _modulated(x, g, mod, j):
    return _rms(x, g) * (1.0 + mod[j + 1:j + 2]) + mod[j:j + 1]


def _sigmoid(x):
    return 1.0 / (1.0 + jnp.exp(-x))


def _dot(a, b):
    return jnp.dot(a, b, preferred_element_type=F32)


def _dot_nt(a, b):
    return lax.dot_general(a, b, (((1,), (1,)), ((), ())), preferred_element_type=F32)


TOK_ROWS = D_MODEL // LANES


def _store_token_tiles(ref, v):
    n = v.shape[0]
    for j in range(TOK_ROWS):
        ref[pl.ds(j, n, stride=TOK_ROWS), :] = v[:, j * LANES:(j + 1) * LANES]


def _load_token_tiles(ref, n):
    return jnp.concatenate([ref[pl.ds(j, n, stride=TOK_ROWS), :] for j in range(TOK_ROWS)], axis=-1)


def _rope_slab(a, c, s, shift):
    return a * c + (pltpu.roll(a, shift, 1) - pltpu.roll(a, LANES - shift, 1)) * s


def _mod_kernel(c_ref, w_ref, b_ref, o_ref):
    c = c_ref[...]
    s = (c * _sigmoid(c)).astype(BF16)
    o_ref[...] = _dot(s, w_ref[...].astype(BF16)) + b_ref[...]


def _modulation(cond, w, b):
    tn = 1536
    out = pl.pallas_call(
        _mod_kernel,
        grid=(6 * D_MODEL // tn,),
        in_specs=[pl.BlockSpec((N_COND, D_MODEL), lambda j: (0, 0)),
                  pl.BlockSpec((D_MODEL, tn), lambda j: (0, j)),
                  pl.BlockSpec((1, tn), lambda j: (0, j))],
        out_specs=pl.BlockSpec((N_COND, tn), lambda j: (0, j)),
        out_shape=jax.ShapeDtypeStruct((N_COND, 6 * D_MODEL), F32),
        compiler_params=_params("arbitrary"),
        name="modulation",
    )(cond, w, b.reshape(1, -1))
    return out.reshape(N_COND, 6, D_MODEL)


def _mod_spec(tm, base=0):
    return pl.BlockSpec((1, 6, D_MODEL), lambda i, *_: (_cond_of_row((i + base) * tm), 0, 0))


def _x_specs(x, tm):
    if isinstance(x, tuple):
        n_ctx = T_CTX // tm
        return ([pl.BlockSpec((tm, D_MODEL), lambda i, *_: (jnp.minimum(i, n_ctx - 1), 0)),
                 pl.BlockSpec((tm, D_MODEL), lambda i, *_: (jnp.maximum(i - n_ctx, 0), 0))], list(x))
    return [pl.BlockSpec((tm, D_MODEL), lambda i, *_: (i, 0))], [x]


def _x_tile(x_refs, tm, cols=slice(None)):
    if len(x_refs) == 2:
        return jnp.where(pl.program_id(0) < T_CTX // tm, x_refs[0][:, cols], x_refs[1][:, cols])
    return x_refs[0][:, cols]


def _full(shape):
    nd = len(shape)
    return pl.BlockSpec(shape, lambda *_: (0,) * nd, pipeline_mode=pl.Buffered(1))


def _rope_cos_sin(n_tok, rot_dim):
    rows = n_tok // GRID_W
    row = np.repeat(np.arange(rows, dtype=np.float32), GRID_W)
    col = np.tile(np.arange(GRID_W, dtype=np.float32), rows)
    n_freq = rot_dim // 4
    inv_freq = np.float32(ROPE_THETA) ** (-np.arange(n_freq, dtype=np.float32) / np.float32(n_freq))
    ang = np.concatenate([row[:, None] * inv_freq, col[:, None] * inv_freq], axis=-1).astype(np.float32)
    return np.cos(ang), np.sin(ang)


def _rope_tables(rot_dim, lead, tm, scale=1.0):
    cos, sin = _rope_cos_sin(DEC_SEQ, rot_dim)
    gap = rot_dim // 2 if lead else 0
    tail = LANES - lead - gap - rot_dim

    def slab(lead_val, rot):
        n = rot.shape[0]
        return np.concatenate([np.full((n, lead), lead_val, np.float32), np.zeros((n, gap), np.float32), rot,
                               np.zeros((n, tail), np.float32)], axis=-1)

    c = slab(1.0, np.concatenate([cos, cos], axis=-1))
    s = slab(0.0, np.concatenate([sin, sin], axis=-1))
    c_id = slab(1.0, np.ones((tm, rot_dim), np.float32))
    s_id = np.zeros((tm, LANES), np.float32)
    scale = np.float32(scale)
    return (jnp.asarray(np.concatenate([c, c_id], axis=0) * scale),
            jnp.asarray(np.concatenate([s, s_id], axis=0) * scale))


MLA_A_COLS = MLA_Q_RANK + MLA_KV_RANK + 2 * LANES
MLA_ROPE_LEAD = MLA_NOPE
MLA_ROPE_OFF = MLA_NOPE + MLA_ROPE // 2


def _mla_proj_kernel(*refs, n_x):
    (mod_ref, n1_ref, wa_ref, qn_ref, wuq_ref, kvn_ref, cq_ref, sq_ref, ck_ref, sk_ref,
     q_ref, ckv_ref, kp_ref, kst_ref) = refs[n_x:]
    h = _modulated(_x_tile(refs[:n_x], TM), n1_ref[...], mod_ref[0], 0).astype(BF16)
    a = _dot(h, wa_ref[...])
    cq = _rms(a[:, :MLA_Q_RANK], qn_ref[...]).astype(BF16)
    ckv_ref[...] = _rms(a[:, MLA_Q_RANK:MLA_Q_RANK + MLA_KV_RANK], kvn_ref[...])
    k0 = MLA_Q_RANK + MLA_KV_RANK
    kp = a[:, k0:k0 + LANES]
    kst_ref[...] = a[:, k0 + LANES:k0 + LANES + MLA_ROPE]
    qa = _dot(cq, wuq_ref[...])
    cq_t = cq_ref[...]
    is_ctx = pl.program_id(0) < T_CTX // TM

    @pl.when(is_ctx)
    def _():
        kp_ref[...] = (kp * ck_ref[...]).astype(BF16)
        for hd in range(MLA_HEADS):
            sl = slice(hd * LANES, (hd + 1) * LANES)
            q_ref[:, sl] = (qa[:, sl] * cq_t).astype(BF16)

    @pl.when(jnp.logical_not(is_ctx))
    def _():
        sq_t = sq_ref[...]
        kp_ref[...] = _rope_slab(kp, ck_ref[...], sk_ref[...], MLA_ROPE // 2).astype(BF16)
        for hd in range(MLA_HEADS):
            sl = slice(hd * LANES, (hd + 1) * LANES)
            q_ref[:, sl] = _rope_slab(qa[:, sl], cq_t, sq_t, MLA_ROPE // 2).astype(BF16)


def _mla_proj(x, mods, norm1, wa_ext, q_norm, wuq_ext, kv_norm, tabs):
    cq, sq, ck, sk = tabs
    n_tiles = T_ALL // TM
    row = lambda i: (i, 0)
    tab = pl.BlockSpec((TM, LANES), lambda i: (_pos_block(i, TM), 0))
    x_specs, x_args = _x_specs(x, TM)
    return pl.pallas_call(
        functools.partial(_mla_proj_kernel, n_x=len(x_args)),
        grid=(n_tiles,),
        in_specs=x_specs + [_mod_spec(TM), _full((1, D_MODEL)),
                            _full((D_MODEL, MLA_A_COLS)), _full((1, MLA_Q_RANK)),
                            _full((MLA_Q_RANK, MLA_HEADS * LANES)), _full((1, MLA_KV_RANK)),
                            tab, tab, tab, tab],
        out_specs=[pl.BlockSpec((TM, MLA_HEADS * LANES), row), pl.BlockSpec((TM, MLA_KV_RANK), row),
                   pl.BlockSpec((TM, LANES), row), pl.BlockSpec((TM, MLA_ROPE), row)],
        out_shape=[jax.ShapeDtypeStruct((T_ALL, MLA_HEADS * LANES), BF16),
                   jax.ShapeDtypeStruct((T_ALL, MLA_KV_RANK), F32),
                   jax.ShapeDtypeStruct((T_ALL, LANES), BF16),
                   jax.ShapeDtypeStruct((T_ALL, MLA_ROPE), F32)],
        compiler_params=_params("arbitrary"),
        name="mla_proj",
    )(*x_args, mods, norm1.reshape(1, -1), wa_ext, q_norm.reshape(1, -1), wuq_ext, kv_norm.reshape(1, -1),
      cq, sq, ck, sk)


def _mla_attn_kernel(*refs, n_lat, n_cache):
    if n_cache:
        (q_ref, ckv_ref, kp_ref, cckv_ref, ckp_ref, wk_ref, wv_ref, wo_ref, x_ref, mod_ref,
         o_ref, k_s, v_s, o_s) = refs
    else:
        (q_ref, ckv_ref, kp_ref, wk_ref, wv_ref, wo_ref, x_ref, mod_ref, o_ref, k_s, v_s, o_s) = refs

    def expand(ckv_of, kp_of, off, rows):
        rc = min(rows, 256)
        for r0 in range(0, rows, rc):
            c = ckv_of(r0, rc).astype(BF16)
            kp = kp_of(r0, rc).astype(F32)
            kn = _dot(c, wk_ref[...])
            v_s[off + r0:off + r0 + rc, :] = _dot(c, wv_ref[...]).astype(BF16)
            for hd in range(MLA_HEADS):
                sl = slice(hd * LANES, (hd + 1) * LANES)
                k_s[off + r0:off + r0 + rc, sl] = (kn[:, sl] + kp).astype(BF16)

    @pl.when(pl.program_id(1) == 0)
    def _():
        if n_cache:
            expand(lambda r, n: cckv_ref[0, r:r + n, :], lambda r, n: ckp_ref[0, r:r + n, :], 0, n_cache)
        expand(lambda r, n: ckv_ref[r:r + n, :], lambda r, n: kp_ref[r:r + n, :], n_cache, n_lat)

    def probs(hd):
        sl = slice(hd * LANES, (hd + 1) * LANES)
        s = _dot_nt(q_ref[:, sl], k_s[:, sl])
        m = jnp.max(s, axis=-1, keepdims=True)
        p = jnp.exp(s - m)
        return p.astype(BF16), jnp.sum(p, axis=-1, keepdims=True)

    nq_rows = q_ref.shape[0]
    low_half = lax.broadcasted_iota(jnp.int32, (nq_rows, LANES), 1) < MLA_V
    for hp in range(MLA_HEADS // 2):
        sl = slice(hp * LANES, (hp + 1) * LANES)
        p0, l0 = probs(2 * hp)
        p1, l1 = probs(2 * hp + 1)
        both = _dot(jnp.concatenate([p0, p1], axis=0), v_s[:, sl])
        o_s[:, sl] = jnp.where(low_half, both[:nq_rows] / l0, both[nq_rows:] / l1).astype(BF16)
    mix = _dot(o_s[...], wo_ref[...])
    o_ref[...] = x_ref[...] + mod_ref[0][2:3] * mix


def _mla_attn(x, mods, q, ckv, kp, wk_ext, wv_ext, wo_ext, cache=None, own_rows=False):
    hw = MLA_HEADS * LANES
    tq = TQ if cache is None else MLA_LAT_TQ
    if cache is None:
        n_lat, n_cache, nb, nq, base = SEQ, 0, BATCH, SEQ // tq, 0
    else:
        n_lat, n_cache, nb, nq, base = DEC_SEQ, PAST_LEN, DEC_BATCH, DEC_SEQ // tq, T_CTX // tq
    seq_base = base * tq // n_lat
    tile = lambda b, qi: (base + b * nq + qi, 0)
    seq = lambda b, qi: (seq_base + b, 0)
    seq_mode = dict(pipeline_mode=pl.Buffered(1)) if cache is not None else {}
    in_specs = [pl.BlockSpec((tq, hw), tile),
                pl.BlockSpec((n_lat, MLA_KV_RANK), seq, **seq_mode),
                pl.BlockSpec((n_lat, LANES), seq, **seq_mode)]
    args = [q, ckv, kp]
    if cache is not None:
        in_specs += [pl.BlockSpec((1, n_cache, MLA_KV_RANK), lambda b, qi: (b, 0, 0)),
                     pl.BlockSpec((1, n_cache, LANES), lambda b, qi: (b, 0, 0))]
        args += list(cache)
    vw = MLA_HEADS * MLA_V
    x_tile = (lambda b, qi: (b * nq + qi, 0)) if own_rows else tile
    in_specs += [_full((MLA_KV_RANK, hw)), _full((MLA_KV_RANK, vw)), _full((vw, D_MODEL)),
                 pl.BlockSpec((tq, D_MODEL), x_tile),
                 pl.BlockSpec((1, 6, D_MODEL), lambda b, qi: (_cond_of_row((base + b * nq + qi) * tq), 0, 0))]
    args += [wk_ext, wv_ext, wo_ext, x, mods]
    sk = n_cache + n_lat
    return pl.pallas_call(
        functools.partial(_mla_attn_kernel, n_lat=n_lat, n_cache=n_cache),
        grid=(nb, nq),
        in_specs=in_specs,
        out_specs=pl.BlockSpec((tq, D_MODEL), x_tile),
        out_shape=jax.ShapeDtypeStruct(x.shape, F32),
        scratch_shapes=[pltpu.VMEM((sk, hw), BF16), pltpu.VMEM((sk, vw), BF16), pltpu.VMEM((tq, vw), BF16)],
        input_output_aliases={} if own_rows else {len(args) - 2: 0},
        compiler_params=_params("arbitrary", "arbitrary"),
        name="mla_attn_lat" if cache is not None else "mla_attn_ctx",
    )(*args)


SWA_QW = SWA_HEADS * LANES
SWA_KW = SWA_KV_HEADS * LANES
SWA_CW = 2 * SWA_KV_HEADS * SWA_HEAD_DIM


def _swa_proj_kernel(*refs, rope):
    if rope:
        (x_ref, mod_ref, n1_ref, w_ref, b_ref, cq_ref, sq_ref, ck_ref, sk_ref, q_ref, k_ref, v_ref) = refs
    else:
        (x_ref, mod_ref, n1_ref, w_ref, b_ref, q_ref, k_ref, v_ref, kv_ref) = refs
    h = _modulated(x_ref[...], n1_ref[...], mod_ref[0], 0).astype(BF16)
    a = _dot(h, w_ref[...]) + b_ref[...]
    if rope:
        cq, sq, ck, sk = cq_ref[...], sq_ref[...], ck_ref[...], sk_ref[...]
        for hd in range(SWA_HEADS):
            sl = slice(hd * LANES, (hd + 1) * LANES)
            q_ref[:, sl] = _rope_slab(a[:, sl], cq, sq, SWA_HEAD_DIM // 2).astype(BF16)
        for hd in range(SWA_KV_HEADS):
            sl = slice(hd * LANES, (hd + 1) * LANES)
            k_ref[:, sl] = _rope_slab(a[:, SWA_QW + hd * LANES:SWA_QW + (hd + 1) * LANES], ck, sk,
                                      SWA_HEAD_DIM // 2).astype(BF16)
    else:
        q_ref[...] = (a[:, :SWA_QW] * SWA_SCALE).astype(BF16)
        k_ref[...] = a[:, SWA_QW:SWA_QW + SWA_KW].astype(BF16)
        kv_ref[...] = a[:, SWA_QW + 2 * SWA_KW:]
    v_ref[...] = a[:, SWA_QW + SWA_KW:SWA_QW + 2 * SWA_KW].astype(BF16)


def _swa_proj(x, mods, norm1, w_ext, b_ext, tabs, rope):
    base = T_CTX // TM if rope else 0
    n_tiles = (T_LAT if rope else T_CTX) // TM
    n_rows = n_tiles * TM
    ncols = w_ext.shape[1]
    row = lambda i: (i, 0)
    in_specs = [pl.BlockSpec((TM, D_MODEL), lambda i: (i + base, 0)), _mod_spec(TM, base), _full((1, D_MODEL)),
                _full((D_MODEL, ncols)), _full((1, ncols))]
    args = [x, mods, norm1.reshape(1, -1), w_ext, b_ext]
    out_specs = [pl.BlockSpec((TM, SWA_QW), row), pl.BlockSpec((TM, SWA_KW), row), pl.BlockSpec((TM, SWA_KW), row)]
    out_shape = [jax.ShapeDtypeStruct((n_rows, SWA_QW), BF16), jax.ShapeDtypeStruct((n_rows, SWA_KW), BF16),
                 jax.ShapeDtypeStruct((n_rows, SWA_KW), BF16)]
    if rope:
        tab = pl.BlockSpec((TM, LANES), lambda i: (_pos_block(i + base, TM), 0))
        in_specs += [tab] * 4
        args += list(tabs)
    else:
        out_specs.append(pl.BlockSpec((TM, SWA_CW), row))
        out_shape.append(jax.ShapeDtypeStruct((n_rows, SWA_CW), F32))
    return pl.pallas_call(
        functools.partial(_swa_proj_kernel, rope=rope),
        grid=(n_tiles,),
        in_specs=in_specs, out_specs=out_specs, out_shape=out_shape,
        compiler_params=_params("arbitrary"),
        name="swa_proj_lat" if rope else "swa_proj_ctx",
    )(*args)


def _swa_ctx_kernel(q_ref, k_ref, v_ref, sink_ref, wo_ref, bo_ref, x_ref, mod_ref, o_ref, o_s):
    for n in range(SWA_KV_HEADS):
        kn = k_ref[:, n * LANES:(n + 1) * LANES]
        vn = v_ref[:, n * LANES:(n + 1) * LANES]
        for g in range(SWA_GROUP):
            j = n * SWA_GROUP + g
            sl = slice(j * LANES, (j + 1) * LANES)
            s = _dot_nt(q_ref[:, sl], kn)
            sink = sink_ref[j]
            m = jnp.maximum(jnp.max(s, axis=-1, keepdims=True), sink)
            p = jnp.exp(s - m)
            l = jnp.sum(p, axis=-1, keepdims=True) + jnp.exp(sink - m)
            o_s[:, sl] = (_dot(p.astype(BF16), vn) / l).astype(BF16)
    mix = _dot(o_s[...], wo_ref[...]) + bo_ref[...]
    o_ref[...] = x_ref[...] + mod_ref[0][2:3] * mix


def _swa_lat_kernel(q_ref, k_ref, v_ref, ck_ref, cv_ref, sink_ref, wo_ref, bo_ref, x_ref, mod_ref, o_ref, o_s):
    span = TQ + 2 * SWA_WINDOW
    start = pl.program_id(1) * TQ
    ks = pl.multiple_of(jnp.clip(start - SWA_WINDOW, 0, DEC_SEQ - span), SWA_WINDOW)
    qpos = start + lax.broadcasted_iota(jnp.int32, (TQ, span), 0)
    kpos = ks + lax.broadcasted_iota(jnp.int32, (TQ, span), 1)
    valid = jnp.abs(kpos - qpos) <= SWA_WINDOW
    for n in range(SWA_KV_HEADS):
        nl = slice(n * LANES, (n + 1) * LANES)
        kc = ck_ref[0, :, nl]
        vc = cv_ref[0, :, nl]
        kl = k_ref[pl.ds(ks, span), nl]
        vl = v_ref[pl.ds(ks, span), nl]
        for g in range(SWA_GROUP):
            j = n * SWA_GROUP + g
            sl = slice(j * LANES, (j + 1) * LANES)
            qj = q_ref[:, sl]
            s_c = _dot_nt(qj, kc)
            s_l = jnp.where(valid, _dot_nt(qj, kl), -jnp.inf)
            sink = sink_ref[j]
            m = jnp.maximum(jnp.maximum(jnp.max(s_c, axis=-1, keepdims=True),
                                        jnp.max(s_l, axis=-1, keepdims=True)), sink)
            p_c = jnp.exp(s_c - m)
            p_l = jnp.exp(s_l - m)
            l = (jnp.sum(p_c, axis=-1, keepdims=True) + jnp.sum(p_l, axis=-1, keepdims=True)
                 + jnp.exp(sink - m))
            o = _dot(p_c.astype(BF16), vc) + _dot(p_l.astype(BF16), vl)
            o_s[:, sl] = (o / l).astype(BF16)
    mix = _dot(o_s[...], wo_ref[...]) + bo_ref[...]
    o_ref[...] = x_ref[...] + mod_ref[0][2:3] * mix


def _swa_attn(x, mods, q, k, v, sink, wo_ext, bo, cache=None):
    smem = pl.BlockSpec(memory_space=pltpu.SMEM)
    if cache is None:
        nq, base = 1, 0
        tile = lambda b, qi: (b, 0)
        in_specs = [pl.BlockSpec((TQ, SWA_QW), tile), pl.BlockSpec((SEQ, SWA_KW), tile),
                    pl.BlockSpec((SEQ, SWA_KW), tile)]
        args = [q, k, v]
        kern, nb, name = _swa_ctx_kernel, BATCH, "swa_attn_ctx"
    else:
        nq, base = DEC_SEQ // TQ, T_CTX // TQ
        tile = lambda b, qi: (b * nq + qi, 0)
        seq = lambda b, qi: (b, 0)
        cspec = pl.BlockSpec((1, PAST_LEN, SWA_KW), lambda b, qi: (b, 0, 0))
        in_specs = [pl.BlockSpec((TQ, SWA_QW), tile), pl.BlockSpec((DEC_SEQ, SWA_KW), seq),
                    pl.BlockSpec((DEC_SEQ, SWA_KW), seq), cspec, cspec]
        args = [q, k, v, cache[0], cache[1]]
        kern, nb, name = _swa_lat_kernel, DEC_BATCH, "swa_attn_lat"
    xtile = lambda b, qi: (base + b * nq + qi, 0)
    in_specs += [smem, _full((SWA_QW, D_MODEL)), _full((1, D_MODEL)), pl.BlockSpec((TQ, D_MODEL), xtile),
                 pl.BlockSpec((1, 6, D_MODEL), lambda b, qi: (_cond_of_row((base + b * nq + qi) * TQ), 0, 0))]
    args += [sink, wo_ext, bo.reshape(1, -1), x, mods]
    return pl.pallas_call(
        kern,
        grid=(nb, nq),
        in_specs=in_specs,
        out_specs=pl.BlockSpec((TQ, D_MODEL), xtile),
        out_shape=jax.ShapeDtypeStruct((T_ALL, D_MODEL), F32),
        scratch_shapes=[pltpu.VMEM((TQ, SWA_QW), BF16)],
        input_output_aliases={len(args) - 2: 0},
        compiler_params=_params("arbitrary", "arbitrary"),
        name=name,
    )(*args)


CONV_EXT = CONV_TM + 2 * CONV_HALO


def _conv_kernel(xp_ref, x_ref, xn_ref, mod_ref, n1_ref, w1_ref, b1_ref, dw_ref, dwb_ref, lng_ref, lnb_ref,
                 w2_ref, b2_ref, o_ref, ext_s, y_s):
    i = pl.program_id(0)
    tiles_per_seq = DEC_SEQ // CONV_TM
    j = (i - T_CTX // CONV_TM) % tiles_per_seq
    latent = i >= T_CTX // CONV_TM
    left_ok = jnp.logical_and(latent, j > 0)
    right_ok = jnp.logical_and(latent, j < tiles_per_seq - 1)
    mod = mod_ref[0]

    def glu_of(xv):
        h = _modulated(xv, n1_ref[...], mod, 0).astype(BF16)
        u = _dot(h, w1_ref[...]) + b1_ref[...]
        return u[:, :D_MODEL] * _sigmoid(u[:, D_MODEL:])

    ext_s[CONV_HALO:CONV_HALO + CONV_TM, :] = glu_of(x_ref[...])
    ext_s[0:CONV_HALO, :] = jnp.where(left_ok, glu_of(xp_ref[...]), 0.0)
    ext_s[CONV_HALO + CONV_TM:, :] = jnp.where(right_ok, glu_of(xn_ref[...]), 0.0)

    rows = 128
    sub = 8
    first = CONV_HALO - CONV_PAD
    span = rows + (first + CONV_WIDTH - 1) // sub * sub
    for c in range(D_MODEL // LANES):
        cl = slice(c * LANES, (c + 1) * LANES)
        dwc = dw_ref[:, cl]
        for r in range(CONV_TM // rows):
            window = ext_s[r * rows:r * rows + span + sub, cl]
            acc = jnp.zeros((rows, LANES), F32)
            for res in range(sub):
                shifted = window if res == 0 else pltpu.roll(window, span + sub - res, 0)
                for q in range(span // sub - rows // sub + 1):
                    w = q * sub + res - first
                    if 0 <= w < CONV_WIDTH:
                        acc = acc + shifted[q * sub:q * sub + rows] * dwc[w:w + 1]
            y_s[r * rows:(r + 1) * rows, cl] = acc
    y = y_s[...] + dwb_ref[...]
    mu = jnp.mean(y, axis=-1, keepdims=True)
    yc = y - mu
    var = jnp.mean(yc * yc, axis=-1, keepdims=True)
    yn = yc * lax.rsqrt(var + NORM_EPS) * lng_ref[...] + lnb_ref[...]
    z = (yn * _sigmoid(yn)).astype(BF16)
    mix = _dot(z, w2_ref[...]) + b2_ref[...]
    o_ref[...] = x_ref[...] + mod[2:3] * mix


def _conv_layer(x, mods, norm1, w1, b1, dw, dwb, ln_g, ln_b, w2, b2):
    n_tiles = T_ALL // CONV_TM
    per = CONV_TM // CONV_HALO
    n_halo_blocks = T_ALL // CONV_HALO
    row = lambda i: (i, 0)
    vec = lambda v: v.reshape(1, -1)
    return pl.pallas_call(
        _conv_kernel,
        grid=(n_tiles,),
        in_specs=[pl.BlockSpec((CONV_HALO, D_MODEL), lambda i: (jnp.maximum(i * per - 1, 0), 0)),
                  pl.BlockSpec((CONV_TM, D_MODEL), row),
                  pl.BlockSpec((CONV_HALO, D_MODEL), lambda i: (jnp.minimum((i + 1) * per, n_halo_blocks - 1), 0)),
                  _mod_spec(CONV_TM), _full((1, D_MODEL)),
                  _full((D_MODEL, 2 * D_MODEL)), _full((1, 2 * D_MODEL)),
                  _full((CONV_WIDTH, D_MODEL)), _full((1, D_MODEL)), _full((1, D_MODEL)), _full((1, D_MODEL)),
                  _full((D_MODEL, D_MODEL)), _full((1, D_MODEL))],
        out_specs=pl.BlockSpec((CONV_TM, D_MODEL), row),
        out_shape=jax.ShapeDtypeStruct((T_ALL, D_MODEL), F32),
        scratch_shapes=[pltpu.VMEM((CONV_EXT, D_MODEL), F32), pltpu.VMEM((CONV_TM, D_MODEL), F32)],
        compiler_params=_params("arbitrary"),
        name="conv_module",
    )(x, x, x, mods, vec(norm1), w1.astype(BF16), vec(b1), dw, vec(dwb), vec(ln_g), vec(ln_b),
      w2.astype(BF16), vec(b2))


def _router_kernel(*refs, n_x):
    mod_ref, n2_ref, rw_ref, rb_ref, h_ref, idx_ref, gate_ref, cnt_ref = refs[n_x:]
    h = _modulated(_x_tile(refs[:n_x], TM), n2_ref[...], mod_ref[0], 3)
    _store_token_tiles(h_ref, h)
    cur = _dot_nt(rw_ref[...], h.astype(BF16)) + rb_ref[...]
    e_iota = lax.broadcasted_iota(jnp.int32, cur.shape, 0)
    vals, idxs = [], []
    for _ in range(TOP_K):
        m = jnp.max(cur, axis=0, keepdims=True)
        idx = jnp.min(jnp.where(cur == m, e_iota, N_EXPERTS), axis=0, keepdims=True)
        vals.append(m)
        idxs.append(idx)
        cur = jnp.where(e_iota == idx, -jnp.inf, cur)
    ex = [jnp.exp(v - vals[0]) for v in vals]
    tot = ex[0] + ex[1] + ex[2] + ex[3]
    for k in range(TOP_K):
        idx_ref[k:k + 1, :] = idxs[k]
        gate_ref[k:k + 1, :] = ex[k] / tot

    picked = jnp.sum(jnp.where(cur == -jnp.inf, 1.0, 0.0), axis=1, keepdims=True)

    @pl.when(pl.program_id(0) == 0)
    def _():
        cnt_ref[...] = jnp.zeros_like(cnt_ref)

    cnt_ref[...] += picked


def _router(x, mods, norm2, router_w, router_b):
    n_tiles = T_ALL // TM
    row = lambda i: (i, 0)
    col = lambda i: (0, i)
    x_specs, x_args = _x_specs(x, TM)
    return pl.pallas_call(
        functools.partial(_router_kernel, n_x=len(x_args)),
        grid=(n_tiles,),
        in_specs=x_specs + [_mod_spec(TM), _full((1, D_MODEL)),
                            _full((N_EXPERTS, D_MODEL)), _full((N_EXPERTS, 1))],
        out_specs=[pl.BlockSpec((TM * TOK_ROWS, LANES), row), pl.BlockSpec((TOP_K, TM), col),
                   pl.BlockSpec((TOP_K, TM), col), pl.BlockSpec((N_EXPERTS, 1), lambda i: (0, 0))],
        out_shape=[jax.ShapeDtypeStruct((T_ALL * TOK_ROWS, LANES), F32),
                   jax.ShapeDtypeStruct((TOP_K, T_ALL), jnp.int32),
                   jax.ShapeDtypeStruct((TOP_K, T_ALL), F32),
                   jax.ShapeDtypeStruct((N_EXPERTS, 1), F32)],
        compiler_params=_params("arbitrary"),
        name="router",
    )(*x_args, mods, norm2.reshape(1, -1), router_w.T.astype(BF16), router_b.reshape(-1, 1))


N_ASSIGN = TOP_K * T_ALL
RANK_CHUNK = 256
RANK_BLOCK = 2048


def _block_layout(counts):
    counts = counts.reshape(N_EXPERTS).astype(jnp.int32)
    padded = (counts + MOE_BM - 1) // MOE_BM * MOE_BM
    pad_end = jnp.cumsum(padded)
    pad_start = pad_end - padded
    starts = jnp.arange(MOE_BLOCKS + 1, dtype=jnp.int32) * MOE_BM
    e_ids = jnp.arange(N_EXPERTS, dtype=jnp.int32)
    used = padded > 0
    last_used = jnp.max(jnp.where(used, e_ids, 0))
    block_e = jnp.minimum(jnp.sum((pad_end[None, :] <= starts[:, None]).astype(jnp.int32), axis=1), last_used)
    n_used = (pad_end[-1] // MOE_BM).reshape(1)
    run = jnp.cumsum(used.astype(jnp.int32)) - 1
    later_used = jnp.where(jnp.logical_and(used[None, :], e_ids[None, :] > e_ids[:, None]), e_ids[None, :], N_EXPERTS)
    next_used = jnp.min(later_used, axis=1)
    next_used = jnp.where(next_used == N_EXPERTS, e_ids, next_used)
    onehot = (block_e[:, None] == e_ids[None, :]).astype(jnp.int32)
    block_next = jnp.sum(onehot * next_used[None, :], axis=1)
    block_par = jnp.sum(onehot * run[None, :], axis=1) & 1
    block_plan = (block_e, block_next, block_par, n_used)
    return pad_start.astype(F32).reshape(N_EXPERTS, 1), block_plan


def _rank_kernel(idx_ref, ps_ref, u_ref, dest_ref, carry):
    @pl.when(pl.program_id(0) == 0)
    def _():
        carry[...] = ps_ref[...]

    e_iota = lax.broadcasted_iota(jnp.int32, (N_EXPERTS, RANK_CHUNK), 0)
    for c in range(RANK_BLOCK // RANK_CHUNK):
        sl = slice(c * RANK_CHUNK, (c + 1) * RANK_CHUNK)
        hit = e_iota == idx_ref[:, sl]
        pref = _dot(jnp.where(hit, 1.0, 0.0).astype(BF16), u_ref[...])
        base = carry[...]
        dest = jnp.sum(jnp.where(hit, pref + base, 0.0), axis=0, keepdims=True) - 1.0
        dest_ref[:, sl] = dest.astype(jnp.int32)
        carry[...] = base + pref[:, RANK_CHUNK - 1:RANK_CHUNK]


def _assignment_slots(idx_t, pad_start):
    tri = jnp.asarray(np.triu(np.ones((RANK_CHUNK, RANK_CHUNK), np.float32)), dtype=BF16)
    blk = pl.BlockSpec((1, RANK_BLOCK), lambda i: (0, i))
    return pl.pallas_call(
        _rank_kernel,
        grid=(N_ASSIGN // RANK_BLOCK,),
        in_specs=[blk, _full((N_EXPERTS, 1)), _full((RANK_CHUNK, RANK_CHUNK))],
        out_specs=blk,
        out_shape=jax.ShapeDtypeStruct((1, N_ASSIGN), jnp.int32),
        scratch_shapes=[pltpu.VMEM((N_EXPERTS, 1), F32)],
        compiler_params=_params("arbitrary"),
        name="moe_rank",
    )(idx_t.reshape(1, N_ASSIGN), pad_start, tri)


INV_UNROLL = 32
INV_CHUNK = 4096
INV_STEPS = N_ASSIGN // INV_CHUNK
N_SLOTS = MOE_CAP + MOE_BM


def _inverse_kernel(dest_ref, pad_hbm, slot_ref, sem):
    i = pl.program_id(0)

    @pl.when(i == 0)
    def _():
        fill = pltpu.make_async_copy(pad_hbm, slot_ref, sem)
        fill.start()
        fill.wait()

    base = i * INV_CHUNK

    def place(j, carry):
        for u in range(INV_UNROLL):
            o = j * INV_UNROLL + u
            slot_ref[dest_ref[0, 0, o]] = base + o
        return carry

    lax.fori_loop(0, INV_CHUNK // INV_UNROLL, place, 0)


def _slot_assignments(dest):
    s = np.arange(N_SLOTS, dtype=np.int32)
    pad_ids = jnp.asarray(N_ASSIGN + (s & (MOE_BM - 1)) + np.where(s >= MOE_CAP, MOE_BM, 0).astype(np.int32))
    return pl.pallas_call(
        _inverse_kernel,
        grid=(INV_STEPS,),
        in_specs=[pl.BlockSpec((1, 1, INV_CHUNK), lambda i: (i, 0, 0), memory_space=pltpu.SMEM),
                  pl.BlockSpec(memory_space=pl.ANY)],
        out_specs=pl.BlockSpec(memory_space=pltpu.SMEM),
        out_shape=jax.ShapeDtypeStruct((N_SLOTS,), jnp.int32),
        scratch_shapes=[pltpu.SemaphoreType.DMA(())],
        compiler_params=_params("arbitrary"),
        name="moe_slots",
    )(dest.reshape(INV_STEPS, 1, INV_CHUNK), pad_ids)


YG_TOKENS = N_ASSIGN + 2 * MOE_BM
BLOCK_TILE_ROWS = MOE_BM * TOK_ROWS
MOE_CHUNK = 256
ACT_BUF = 3


def _moe_kernel(be_ref, ne_ref, par_ref, nu_ref, sa_prev_ref, sa_ref, sa_next_ref, sa_next2_ref, h_hbm,
                wg_hbm, wu_hbm, wd_hbm, bg_ref, bu_ref, bd_ref, yg_hbm,
                xbuf, ybuf, x_s, wg_f, wu_f, wd_f, wgu_s, wd_s, gsem, ssem, wsem):
    i = pl.program_id(0)
    n_used = nu_ref[0]
    cur = i % 2
    nxt = 1 - cur
    gcur = lax.rem(i, 3)
    gnext = lax.rem(i + 1, 3)
    gnext2 = lax.rem(i + 2, 3)

    def weight_copies(e, slot):
        return [pltpu.make_async_copy(src.at[e], dst.at[slot], wsem.at[slot])
                for src, dst in ((wg_hbm, wg_f), (wu_hbm, wu_f), (wd_hbm, wd_f))]

    def start_gather(sa, buf, lo=0, hi=MOE_BM):
        for r in range(lo, hi):
            tok = sa[0, 0, r] & (T_ALL - 1)
            pltpu.make_async_copy(h_hbm.at[pl.ds(tok * TOK_ROWS, TOK_ROWS)],
                                  xbuf.at[buf, pl.ds(r * TOK_ROWS, TOK_ROWS)], gsem.at[buf]).start()

    def start_scatter(sa, buf):
        for r in range(MOE_BM):
            pltpu.async_copy(ybuf.at[buf, pl.ds(r * TOK_ROWS, TOK_ROWS)],
                             yg_hbm.at[pl.ds(sa[0, 0, r] * TOK_ROWS, TOK_ROWS)], ssem.at[buf], priority=1)

    def wait_gather(buf):
        pltpu.make_async_copy(h_hbm.at[pl.ds(0, BLOCK_TILE_ROWS)], xbuf.at[buf], gsem.at[buf]).wait()

    def wait_scatter(buf):
        pltpu.make_async_copy(ybuf.at[buf], yg_hbm.at[pl.ds(0, BLOCK_TILE_ROWS)], ssem.at[buf]).wait()

    @pl.when(i == 0)
    def _():
        for cp in weight_copies(be_ref[0], 0):
            cp.start()
        start_gather(sa_ref, 0)
        start_gather(sa_next_ref, 1)
        ybuf[...] = jnp.zeros_like(ybuf)
        pltpu.make_async_copy(ybuf.at[0], yg_hbm.at[pl.ds(N_ASSIGN * TOK_ROWS, BLOCK_TILE_ROWS)], ssem.at[0]).start()

    @pl.when(i <= n_used)
    def _():
        e = be_ref[i]

        @pl.when(jnp.logical_or(i == 0, e != be_ref[jnp.maximum(i - 1, 0)]))
        def _():
            slot = par_ref[i]
            for cp in weight_copies(e, slot):
                cp.wait()
            for c in range(D_MODEL // MOE_CHUNK):
                cs = slice(c * MOE_CHUNK, (c + 1) * MOE_CHUNK)
                wgu_s[:, 2 * c * MOE_CHUNK:(2 * c + 1) * MOE_CHUNK] = wg_f[slot, :, cs].astype(BF16)
                wgu_s[:, (2 * c + 1) * MOE_CHUNK:(2 * c + 2) * MOE_CHUNK] = wu_f[slot, :, cs].astype(BF16)
            wd_s[...] = wd_f[slot].astype(BF16)

            @pl.when(ne_ref[i] != e)
            def _():
                for cp in weight_copies(ne_ref[i], 1 - slot):
                    cp.start()

        wait_gather(gcur)
        x_s[...] = _load_token_tiles(xbuf.at[gcur], MOE_BM).astype(BF16)
        n_chunks = D_MODEL // MOE_CHUNK
        slabs = MOE_CHUNK // LANES
        for c in range(n_chunks):
            cs = slice(c * MOE_CHUNK, (c + 1) * MOE_CHUNK)
            start_gather(sa_next2_ref, gnext2, c * MOE_BM // n_chunks, (c + 1) * MOE_BM // n_chunks)
            gu = _dot(x_s[...], wgu_s[:, 2 * c * MOE_CHUNK:(2 * c + 2) * MOE_CHUNK])
            g = jnp.minimum(gu[:, :MOE_CHUNK] + bg_ref[0, :, cs], SWIGLU_LIMIT)
            u = jnp.clip(gu[:, MOE_CHUNK:] + bu_ref[0, :, cs], -SWIGLU_LIMIT, SWIGLU_LIMIT)
            act = (u + 1.0) * (g * _sigmoid(SWIGLU_ALPHA * g))
            for j in range(slabs):
                jj = c * slabs + j
                xbuf[ACT_BUF, jj * MOE_BM:(jj + 1) * MOE_BM, :] = act[:, j * LANES:(j + 1) * LANES]
        wait_scatter(cur)
        start_scatter(sa_prev_ref, nxt)
        act_b =jnp.concatenate([xbuf[ACT_BUF, j * MOE_BM:(j + 1) * MOE_BM, :] for j in range(TOK_ROWS)],
                                axis=-1).astype(BF16)
        half = D_MODEL // 2
        for c in range(2):
            y = _dot(act_b, wd_s[:, c * half:(c + 1) * half]) + bd_ref[0, :, c * half:(c + 1) * half]
            for j in range(TOK_ROWS // 2):
                jj = c * (TOK_ROWS // 2) + j
                ybuf.at[cur][pl.ds(jj, MOE_BM, stride=TOK_ROWS), :] = y[:, j * LANES:(j + 1) * LANES]

    @pl.when(i == n_used)
    def _():
        wait_gather(gnext)
        wait_gather(gnext2)
        wait_scatter(nxt)


def _moe_experts(h, slots, block_plan, wg, bg, wu, bu, wd, bd):
    bspec = pl.BlockSpec((1, 1, D_MODEL), lambda i, be, *_: (be[i], 0, 0))
    sspec = lambda f: pl.BlockSpec((1, 1, MOE_BM), lambda i, *_: (f(i), 0, 0), memory_space=pltpu.SMEM)
    hbm = pl.BlockSpec(memory_space=pl.ANY)
    b3 = lambda b: b.reshape(N_EXPERTS, 1, D_MODEL)
    slots3 = slots.reshape(MOE_BLOCKS + 1, 1, MOE_BM)
    wbuf = pltpu.VMEM((2, D_MODEL, D_MODEL), F32)
    return pl.pallas_call(
        _moe_kernel,
        grid_spec=pltpu.PrefetchScalarGridSpec(
            num_scalar_prefetch=4,
            grid=(MOE_BLOCKS + 1,),
            in_specs=[sspec(lambda i: jnp.where(i == 0, MOE_BLOCKS, i - 1)), sspec(lambda i: i),
                      sspec(lambda i: jnp.minimum(i + 1, MOE_BLOCKS)), sspec(lambda i: jnp.minimum(i + 2, MOE_BLOCKS)),
                      hbm, hbm, hbm, hbm, bspec, bspec, bspec],
            out_specs=hbm,
            scratch_shapes=[pltpu.VMEM((ACT_BUF + 1, BLOCK_TILE_ROWS, LANES), F32),
                            pltpu.VMEM((2, BLOCK_TILE_ROWS, LANES), F32),
                            pltpu.VMEM((MOE_BM, D_MODEL), BF16),
                            wbuf, wbuf, wbuf,
                            pltpu.VMEM((D_MODEL, 2 * D_MODEL), BF16), pltpu.VMEM((D_MODEL, D_MODEL), BF16),
                            pltpu.SemaphoreType.DMA((3,)), pltpu.SemaphoreType.DMA((2,)),
                            pltpu.SemaphoreType.DMA((2,))]),
        out_shape=jax.ShapeDtypeStruct((YG_TOKENS * TOK_ROWS, LANES), F32),
        compiler_params=_params("arbitrary"),
        name="moe_experts",
    )(*block_plan, slots3, slots3, slots3, slots3, h, wg, wu, wd, b3(bg), b3(bu), b3(bd))


def _combine_kernel(*refs, final, n_x):
    x_refs = refs[:n_x]
    if final:
        y0_ref, y1_ref, y2_ref, y3_ref, gate_ref, mod_ref, fg_ref, o_ref = refs[n_x:]
    else:
        y0_ref, y1_ref, y2_ref, y3_ref, gate_ref, mod_ref, o_ref = refs[n_x:]
    gate = gate_ref[...]
    n = o_ref.shape[0]
    g2 = mod_ref[0][5:6]
    sumsq = jnp.zeros((n, 1), F32)
    for j in range(TOK_ROWS):
        rows = pl.ds(j, n, stride=TOK_ROWS)
        sl = slice(j * LANES, (j + 1) * LANES)
        acc = y0_ref[rows, :] * gate[:, 0:1]
        for k, y_ref in enumerate((y1_ref, y2_ref, y3_ref), start=1):
            acc = acc + y_ref[rows, :] * gate[:, k:k + 1]
        xn = _x_tile(x_refs, TM, sl) + g2[:, sl] * acc
        o_ref[:, sl] = xn
        if final:
            sumsq = sumsq + jnp.sum(xn * xn, axis=-1, keepdims=True)
    if final:
        inv = lax.rsqrt(sumsq * (1.0 / D_MODEL) + NORM_EPS)
        o_ref[...] = o_ref[...] * inv * fg_ref[...]


def _combine(x, yg, gates, mods, final_g=None, base_rows=0, n_rows=T_ALL):
    final = final_g is not None
    base = base_rows // TM
    n_all = T_ALL // TM
    row = lambda i: (i + base, 0)
    ysp = lambda k: pl.BlockSpec((TM * TOK_ROWS, LANES), lambda i: (k * n_all + i + base, 0))
    if isinstance(x, tuple):
        x_specs, x_args = _x_specs(x, TM)
    else:
        x_specs, x_args = [pl.BlockSpec((TM, D_MODEL), row)], [x]
    in_specs = x_specs + [ysp(0), ysp(1), ysp(2), ysp(3), pl.BlockSpec((TM, TOP_K), row), _mod_spec(TM, base)]
    args = x_args + [yg, yg, yg, yg, gates, mods]
    if final:
        in_specs.append(_full((1, D_MODEL)))
        args.append(final_g.reshape(1, -1))
    return pl.pallas_call(
        functools.partial(_combine_kernel, final=final, n_x=len(x_args)),
        grid=(n_rows // TM,),
        in_specs=in_specs,
        out_specs=pl.BlockSpec((TM, D_MODEL), (lambda i: (i, 0)) if final else row),
        out_shape=jax.ShapeDtypeStruct((n_rows, D_MODEL), F32),
        input_output_aliases={} if final or len(x_args) == 2 else {0: 0},
        compiler_params=_params("arbitrary"),
        name="moe_combine_final" if final else "moe_combine",
    )(*args)


def _moe_layer(x, mods, norm2, router_w, router_b, wg, bg, wu, bu, wd, bd, final_g=None):
    h, idx_t, gate_t, counts = _router(x, mods, norm2, router_w, router_b)
    pad_start, block_plan = _block_layout(counts)
    slots = _slot_assignments(_assignment_slots(idx_t, pad_start))
    yg = _moe_experts(h, slots, block_plan, wg, bg, wu, bu, wd, bd)
    gates = gate_t.T
    if final_g is None:
        return _combine(x, yg, gates, mods)
    return (_combine(x, yg, gates, mods, final_g, 0, T_CTX),
            _combine(x, yg, gates, mods, final_g, T_CTX, T_LAT))


def _pad_lanes(w, lead_pad, width):
    tail = width - lead_pad - w.shape[-1]
    cfg = [(0, 0)] * (w.ndim - 1) + [(lead_pad, tail)]
    return jnp.pad(w, cfg)


def _mla_weights(wa, wuq, wukv, wo):
    krope_w = wa[:, MLA_Q_RANK + MLA_KV_RANK:]
    wa_ext = jnp.concatenate([wa[:, :MLA_Q_RANK + MLA_KV_RANK],
                              _pad_lanes(krope_w, MLA_ROPE_OFF, LANES),
                              _pad_lanes(krope_w, 0, LANES)], axis=1).astype(BF16)
    q3 = wuq.reshape(MLA_Q_RANK, MLA_HEADS, MLA_NOPE + MLA_ROPE)
    wuq_ext = jnp.concatenate([q3[..., :MLA_NOPE], _pad_lanes(q3[..., MLA_NOPE:], MLA_ROPE // 2, LANES - MLA_NOPE)],
                              axis=-1).reshape(MLA_Q_RANK, MLA_HEADS * LANES).astype(BF16)
    kv3 = wukv.reshape(MLA_KV_RANK, MLA_HEADS, MLA_NOPE + MLA_V)
    wk_ext = _pad_lanes(kv3[..., :MLA_NOPE], 0, LANES).reshape(MLA_KV_RANK, MLA_HEADS * LANES).astype(BF16)
    wv = kv3[..., MLA_NOPE:].reshape(MLA_KV_RANK, MLA_HEADS * MLA_V).astype(BF16)
    return wa_ext, wuq_ext, wk_ext, wv, wo.astype(BF16)


def _swa_weights(wqkv, bqkv, wo):
    nq = SWA_HEADS * SWA_HEAD_DIM
    nkv = SWA_KV_HEADS * SWA_HEAD_DIM

    def slabs(w, heads):
        lead = w.shape[:-1]
        return _pad_lanes(w.reshape(lead + (heads, SWA_HEAD_DIM)), 0, LANES).reshape(lead + (heads * LANES,))

    def ext(w, with_compact):
        parts = [slabs(w[..., :nq], SWA_HEADS), slabs(w[..., nq:nq + nkv], SWA_KV_HEADS),
                 slabs(w[..., nq + nkv:], SWA_KV_HEADS)]
        if with_compact:
            parts.append(w[..., nq:])
        return jnp.concatenate(parts, axis=-1)

    b2 = bqkv.reshape(1, -1)
    wo3 = wo.reshape(SWA_HEADS, SWA_HEAD_DIM, D_MODEL)
    wo_ext = jnp.pad(wo3, ((0, 0), (0, LANES - SWA_HEAD_DIM), (0, 0))).reshape(SWA_QW, D_MODEL).astype(BF16)
    return (ext(wqkv, True).astype(BF16), ext(b2, True), ext(wqkv, False).astype(BF16), ext(b2, False), wo_ext)


def kernel(x_prompt, x_sample, cache_l0_ckv, cache_l0_krope, cache_l1_k, cache_l1_v, cache_l3_ckv, cache_l3_krope, c, c_ctx, l0_mla_wa, l0_mla_q_norm, l0_mla_wuq, l0_mla_kv_norm, l0_mla_wukv, l0_mla_wo, l0_mod_w, l0_mod_b, l0_norm1, l0_norm2, l0_router_w, l0_router_b, l0_moe_wg, l0_moe_bg, l0_moe_wu, l0_moe_bu, l0_moe_wd, l0_moe_bd, l1_swa_wqkv, l1_swa_bqkv, l1_swa_sink, l1_swa_wo, l1_swa_bo, l1_mod_w, l1_mod_b, l1_norm1, l1_norm2, l1_router_w, l1_router_b, l1_moe_wg, l1_moe_bg, l1_moe_wu, l1_moe_bu, l1_moe_wd, l1_moe_bd, l2_conv_w1, l2_conv_b1, l2_conv_dw, l2_conv_dwb, l2_conv_ln_g, l2_conv_ln_b, l2_conv_w2, l2_conv_b2, l2_mod_w, l2_mod_b, l2_norm1, l2_norm2, l2_router_w, l2_router_b, l2_moe_wg, l2_moe_bg, l2_moe_wu, l2_moe_bu, l2_moe_wd, l2_moe_bd, l3_mla_wa, l3_mla_q_norm, l3_mla_wuq, l3_mla_kv_norm, l3_mla_wukv, l3_mla_wo, l3_mod_w, l3_mod_b, l3_norm1, l3_norm2, l3_router_w, l3_router_b, l3_moe_wg, l3_moe_bg, l3_moe_wu, l3_moe_bu, l3_moe_wd, l3_moe_bd, final_norm):
    x = (x_prompt.reshape(T_CTX, D_MODEL), x_sample.reshape(T_LAT, D_MODEL))
    cond =jnp.concatenate([c, c_ctx[None, :], jnp.zeros((N_COND - DEC_BATCH - 1, D_MODEL), F32)], axis=0)

    mla_tabs = (_rope_tables(MLA_ROPE, MLA_ROPE_LEAD, TM, MLA_SCALE) + _rope_tables(MLA_ROPE, MLA_ROPE_LEAD, TM))
    swa_tabs = _rope_tables(SWA_HEAD_DIM, 0, TM, SWA_SCALE) + _rope_tables(SWA_HEAD_DIM, 0, TM)

    def mla_layer(x, mods, norm1, wa, q_norm, wuq, kv_norm, wukv, wo, cache_ckv, cache_krope):
        wa_ext, wuq_ext, wk_ext, wv_ext, wo_ext = _mla_weights(wa, wuq, wukv, wo)
        q, ckv, kp, kst = _mla_proj(x, mods, norm1, wa_ext, q_norm, wuq_ext, kv_norm, mla_tabs)
        cache = (cache_ckv, _pad_lanes(cache_krope, MLA_ROPE_OFF, LANES).astype(BF16))
        if isinstance(x, tuple):
            x = (_mla_attn(x[0], mods, q, ckv, kp, wk_ext, wv_ext, wo_ext, own_rows=True),
                 _mla_attn(x[1], mods, q, ckv, kp, wk_ext, wv_ext, wo_ext, cache, own_rows=True))
        else:
            x = _mla_attn(x, mods, q, ckv, kp, wk_ext, wv_ext, wo_ext)
            x = _mla_attn(x, mods, q, ckv, kp, wk_ext, wv_ext, wo_ext, cache)
        return x, (ckv[:T_CTX].reshape(BATCH, SEQ, MLA_KV_RANK), kst[:T_CTX].reshape(BATCH, SEQ, MLA_ROPE))

    def moe(x, mods, norm2, rw, rb, wg, bg, wu, bu, wd, bd):
        return _moe_layer(x, mods, norm2, rw, rb, wg, bg, wu, bu, wd, bd)

    mods = _modulation(cond, l0_mod_w, l0_mod_b)
    x, (st0_ckv, st0_krope) = mla_layer(x, mods, l0_norm1, l0_mla_wa, l0_mla_q_norm, l0_mla_wuq, l0_mla_kv_norm,
                                        l0_mla_wukv, l0_mla_wo, cache_l0_ckv, cache_l0_krope)
    x = moe(x, mods, l0_norm2, l0_router_w, l0_router_b, l0_moe_wg, l0_moe_bg, l0_moe_wu, l0_moe_bu,
            l0_moe_wd, l0_moe_bd)

    mods = _modulation(cond, l1_mod_w, l1_mod_b)
    w_ctx, b_ctx, w_lat, b_lat, swa_wo_ext = _swa_weights(l1_swa_wqkv, l1_swa_bqkv, l1_swa_wo)
    q_c, k_c, v_c, kv_c = _swa_proj(x, mods, l1_norm1, w_ctx, b_ctx, None, rope=False)
    q_l, k_l, v_l = _swa_proj(x, mods, l1_norm1, w_lat, b_lat, swa_tabs, rope=True)
    x = _swa_attn(x, mods, q_c, k_c, v_c, l1_swa_sink, swa_wo_ext, l1_swa_bo)
    nkv = SWA_KV_HEADS * SWA_HEAD_DIM
    cache_k = _pad_lanes(cache_l1_k, 0, LANES).reshape(DEC_BATCH, PAST_LEN, SWA_KW).astype(BF16)
    cache_v = _pad_lanes(cache_l1_v, 0, LANES).reshape(DEC_BATCH, PAST_LEN, SWA_KW).astype(BF16)
    x = _swa_attn(x, mods, q_l, k_l, v_l, l1_swa_sink, swa_wo_ext, l1_swa_bo, (cache_k, cache_v))
    st1_k = kv_c[:, :nkv].reshape(BATCH, SEQ, SWA_KV_HEADS, SWA_HEAD_DIM)
    st1_v = kv_c[:, nkv:].reshape(BATCH, SEQ, SWA_KV_HEADS, SWA_HEAD_DIM)
    x = moe(x, mods, l1_norm2, l1_router_w, l1_router_b, l1_moe_wg, l1_moe_bg, l1_moe_wu, l1_moe_bu,
            l1_moe_wd, l1_moe_bd)

    mods = _modulation(cond, l2_mod_w, l2_mod_b)
    x = _conv_layer(x, mods, l2_norm1, l2_conv_w1, l2_conv_b1, l2_conv_dw, l2_conv_dwb, l2_conv_ln_g,
                    l2_conv_ln_b, l2_conv_w2, l2_conv_b2)
    x = moe(x, mods, l2_norm2, l2_router_w, l2_router_b, l2_moe_wg, l2_moe_bg, l2_moe_wu, l2_moe_bu,
            l2_moe_wd, l2_moe_bd)

    mods = _modulation(cond, l3_mod_w, l3_mod_b)
    x, (st3_ckv, st3_krope) = mla_layer(x, mods, l3_norm1, l3_mla_wa, l3_mla_q_norm, l3_mla_wuq, l3_mla_kv_norm,
                                        l3_mla_wukv, l3_mla_wo, cache_l3_ckv, cache_l3_krope)
    y_ctx, y_lat = _moe_layer(x, mods, l3_norm2, l3_router_w, l3_router_b, l3_moe_wg, l3_moe_bg, l3_moe_wu,
                              l3_moe_bu, l3_moe_wd, l3_moe_bd, final_g=final_norm)
    y_prompt = y_ctx.reshape(BATCH, SEQ, D_MODEL)
    y_sample = y_lat.reshape(DEC_BATCH, DEC_SEQ, D_MODEL)
    return (y_prompt, y_sample, st0_ckv, st0_krope, st1_k, st1_v, st3_ckv, st3_krope)
```
